```python
import jax
import jax.numpy as jnp
from jax import lax
import numpy as np

D_MODEL = 2048
BATCH = 4
SEQ = 4096
DEPTH = 1

HEAD_DIM = 128
ATT_GROUPS = ((128, 1), (512, 4), (2048, 16))
ATT_HEADS_PER_GROUP = 4
ATT_HEADS = ATT_HEADS_PER_GROUP * len(ATT_GROUPS)
ATT_WIDTH = ATT_HEADS * HEAD_DIM
ATT_OUT_WIDTH = ATT_HEADS_PER_GROUP * HEAD_DIM
ATT_BLOCK = 128
HGRN_HEADS = 8
HGRN_DK = 128
HGRN_DV = 128
HGRN_WIDTH = HGRN_HEADS * HGRN_DK
HGRN_CHUNK = 64
N_EXPERTS = 64
N_GROUPS = 8
TOPK_GROUPS = 4
TOP_K = 8
D_EXPERT = 512
D_SHARED = 512
ROUTED_SCALE = 2.5
MOE_BLOCK = 128
NORM_EPS = 1e-6
IN_COLS = 3 * ATT_WIDTH + 4 * HGRN_WIDTH + 2 * D_MODEL

kernel_name = 'hybrid_dilated_attn_hgrn2_moe_block'


def rmsnorm(a, gain):
    af = a.astype(jnp.float32)
    y = af * lax.rsqrt(jnp.mean(af * af, axis=-1, keepdims=True) + NORM_EPS)
    return (y * gain.astype(jnp.float32)).astype(a.dtype)


def modulate(a, shift, scale):
    return a * (1.0 + scale[:, None, :]) + shift[:, None, :]


def alibi_slopes():
    h = jnp.arange(1, ATT_HEADS + 1, dtype=jnp.float32)
    return jnp.exp2(-8.0 * h / ATT_HEADS)


def dilated_window_attention(q, k, v, window, dilation, slopes):
    B, H, S, hd = q.shape
    w_sub = window // dilation
    L = S // dilation
    nb = -(-L // ATT_BLOCK)
    Lp = nb * ATT_BLOCK

    def to_blocks(a):
        a = a.reshape(B, H, L, dilation, hd).transpose(0, 1, 3, 2, 4)
        a = jnp.pad(a, ((0, 0), (0, 0), (0, 0), (0, Lp - L), (0, 0)))
        return a.reshape(B, H, dilation, nb, ATT_BLOCK, hd)

    qb, kb, vb = to_blocks(q), to_blocks(k), to_blocks(v)

    def with_prev(a):
        prev = jnp.pad(a[:, :, :, :-1], ((0, 0), (0, 0), (0, 0), (1, 0), (0, 0), (0, 0)))
        return jnp.concatenate([prev, a], axis=4)

    kk, vv = with_prev(kb), with_prev(vb)
    s = jnp.einsum('bhrnqd,bhrnkd->bhrnqk', qb, kk, preferred_element_type=jnp.float32) * (hd ** -0.5)
    qi = jnp.arange(ATT_BLOCK)[:, None]
    ki = jnp.arange(2 * ATT_BLOCK)[None, :]
    j = qi + ATT_BLOCK - ki
    key_pos = jnp.arange(nb)[:, None, None] * ATT_BLOCK + ki[None] - ATT_BLOCK
    valid = (j >= 0) & (j <= w_sub) & (key_pos >= 0)
    dist = (dilation * j).astype(jnp.float32)
    s = s - slopes.astype(jnp.float32)[None, :, None, None, None, None] * dist
    s = jnp.where(valid, s, -jnp.inf)
    lse = jax.nn.logsumexp(s, axis=-1)
    p = jnp.exp(s - lse[..., None])
    o = jnp.einsum('bhrnqk,bhrnkd->bhrnqd', p, vv.astype(jnp.float32))
    o = o.reshape(B, H, dilation, Lp, hd)[:, :, :, :L].transpose(0, 1, 3, 2, 4).reshape(B, H, S, hd)
    lse = lse.reshape(B, H, dilation, Lp)[:, :, :, :L].transpose(0, 1, 3, 2).reshape(B, H, S)
    return o, lse


def hgrn2_chunkwise(q, k, v, log_f):
    B, H, S, dk = q.shape
    dv = v.shape[-1]
    n_chunks = S // HGRN_CHUNK

    def to_chunks(a):
        return a.reshape(B, H, n_chunks, HGRN_CHUNK, a.shape[-1]).transpose(2, 0, 1, 3, 4)

    causal = jnp.tril(jnp.ones((HGRN_CHUNK, HGRN_CHUNK), dtype=bool))[:, :, None]

    def step(state, inp):
        qc, kc, vc, gc = inp
        b = jnp.cumsum(gc, axis=2)
        o_inter = jnp.einsum('bhtk,bhkv->bhtv', qc * jnp.exp(b), state)
        rel = b[:, :, :, None, :] - b[:, :, None, :, :]
        decay = jnp.exp(jnp.where(causal, rel, -jnp.inf))
        scores = jnp.einsum('bhtk,bhsk,bhtsk->bhts', qc, kc, decay)
        o_intra = jnp.einsum('bhts,bhsv->bhtv', scores, vc)
        b_last = b[:, :, -1:, :]
        state = (jnp.exp(b_last[:, :, 0, :])[..., None] * state
                 + jnp.einsum('bhsk,bhsv->bhkv', kc * jnp.exp(b_last - b), vc))
        return state, o_inter + o_intra

    state0 = jnp.zeros((B, H, dk, dv), jnp.float32)
    _, o = lax.scan(step, state0, (to_chunks(q), to_chunks(k), to_chunks(v), to_chunks(log_f)))
    return o.transpose(1, 2, 0, 3, 4).reshape(B, H, S, dv)


def moe_ffn(h, w_router, router_bias, w_gate_e, w_up_e, w_down_e, w_gate_s, w_up_s, w_down_s):
    B, S, D = h.shape
    T = B * S
    xt = h.reshape(T, D)
    scores = jax.nn.sigmoid((xt @ w_router).astype(jnp.float32))
    choice = scores + router_bias.astype(jnp.float32)
    per_group = N_EXPERTS // N_GROUPS
    group_score = lax.top_k(choice.reshape(T, N_GROUPS, per_group), 2)[0].sum(-1)
    _, top_groups = lax.top_k(group_score, TOPK_GROUPS)
    group_mask = jnp.any(top_groups[:, :, None] == jnp.arange(N_GROUPS)[None, None, :], axis=1)
    expert_mask = jnp.repeat(group_mask, per_group, axis=1)
    _, top_idx = lax.top_k(jnp.where(expert_mask, choice, -jnp.inf), TOP_K)
    gate = jnp.take_along_axis(scores, top_idx, axis=1)
    gate = gate / jnp.sum(gate, axis=1, keepdims=True) * ROUTED_SCALE

    n_assign = T * TOP_K
    n_blocks = -(-n_assign // MOE_BLOCK) + N_EXPERTS
    flat_e = top_idx.reshape(-1)
    flat_tok = jnp.repeat(jnp.arange(T, dtype=jnp.int32), TOP_K)
    flat_w = gate.reshape(-1)
    order = jnp.argsort(flat_e)
    e_sorted = flat_e[order]
    counts = jax.ops.segment_sum(jnp.ones_like(flat_e), flat_e, num_segments=N_EXPERTS)
    starts = jnp.cumsum(counts) - counts
    padded = (counts + MOE_BLOCK - 1) // MOE_BLOCK * MOE_BLOCK
    pad_end = jnp.cumsum(padded)
    dest = (pad_end - padded)[e_sorted] + jnp.arange(n_assign, dtype=jnp.int32) - starts[e_sorted]
    slot_tok = jnp.full((n_blocks * MOE_BLOCK,), T, jnp.int32).at[dest].set(flat_tok[order])
    slot_w = jnp.zeros((n_blocks * MOE_BLOCK,), jnp.float32).at[dest].set(flat_w[order])
    block_start = jnp.arange(n_blocks, dtype=jnp.int32) * MOE_BLOCK
    block_expert = jnp.minimum(jnp.searchsorted(pad_end, block_start, side='right'), N_EXPERTS - 1)
    x_pad = jnp.concatenate([xt, jnp.zeros((1, D), xt.dtype)], axis=0)

    def expert_block(acc, blk):
        tok, wts, e = blk
        xb = x_pad[tok]
        hid = jax.nn.silu(xb @ w_gate_e[e]) * (xb @ w_up_e[e])
        yb = (hid @ w_down_e[e]).astype(jnp.float32) * wts[:, None]
        return acc.at[tok].add(yb), None

    acc, _ = lax.scan(expert_block, jnp.zeros((T + 1, D), jnp.float32),
                      (slot_tok.reshape(n_blocks, MOE_BLOCK), slot_w.reshape(n_blocks, MOE_BLOCK), block_expert))
    shared = (jax.nn.silu(xt @ w_gate_s) * (xt @ w_up_s)) @ w_down_s
    return (acc[:T] + shared.astype(jnp.float32)).astype(h.dtype).reshape(B, S, D)


def hybrid_layer(x, c, lb, w_ada, b_ada, pre_norm_mix, post_norm_mix, w_in, hgrn_norm,
                 w_branch_attn, w_branch_hgrn, w_out, pre_norm_ffn, post_norm_ffn,
                 w_router, router_bias, w_gate_e, w_up_e, w_down_e, w_gate_s, w_up_s, w_down_s):
    B, S, D = x.shape
    mod = jax.nn.silu(c) @ w_ada + b_ada
    shift_m, scale_m, gate_m, shift_f, scale_f, gate_f = jnp.split(mod, 6, axis=-1)

    h = modulate(rmsnorm(x, pre_norm_mix), shift_m, scale_m)
    proj = h @ w_in
    cuts = [int(v) for v in np.cumsum([ATT_WIDTH] * 3 + [HGRN_WIDTH] * 4 + [D_MODEL])]
    q_a, k_a, v_a, q_r, f_r, i_r, g_r, gate_a, gate_b = jnp.split(proj, cuts, axis=-1)

    def att_heads(a):
        return a.reshape(B, S, ATT_HEADS, HEAD_DIM).transpose(0, 2, 1, 3)
    qh, kh, vh = att_heads(q_a), att_heads(k_a), att_heads(v_a)
    slopes = alibi_slopes()
    outs, lses = [], []
    for g, (window, dilation) in enumerate(ATT_GROUPS):
        sl = slice(g * ATT_HEADS_PER_GROUP, (g + 1) * ATT_HEADS_PER_GROUP)
        o_g, lse_g = dilated_window_attention(qh[:, sl], kh[:, sl], vh[:, sl], window, dilation, slopes[sl])
        outs.append(o_g)
        lses.append(lse_g)
    mix_w = jax.nn.softmax(jnp.stack(lses, axis=0), axis=0)
    o_att = jnp.sum(mix_w[..., None] * jnp.stack(outs, axis=0), axis=0)
    o_att = o_att.transpose(0, 2, 1, 3).reshape(B, S, ATT_OUT_WIDTH).astype(x.dtype)

    f = lb + (1.0 - lb) * jax.nn.sigmoid(f_r.astype(jnp.float32))
    def rec_heads(a):
        return a.reshape(B, S, HGRN_HEADS, -1).transpose(0, 2, 1, 3)
    o_rec = hgrn2_chunkwise(rec_heads(q_r.astype(jnp.float32)), rec_heads(1.0 - f),
                            rec_heads(i_r.astype(jnp.float32)), rec_heads(jnp.log(f)))
    o_rec = rmsnorm(o_rec, hgrn_norm.reshape(HGRN_HEADS, 1, HGRN_DV))
    o_rec = o_rec.transpose(0, 2, 1, 3).reshape(B, S, HGRN_HEADS * HGRN_DV).astype(x.dtype) * jax.nn.silu(g_r)

    merged = jax.nn.sigmoid(gate_a) * (o_att @ w_branch_attn) + jax.nn.sigmoid(gate_b) * (o_rec @ w_branch_hgrn)
    y = merged @ w_out
    x = x + gate_m[:, None, :] * rmsnorm(y, post_norm_mix)

    h2 = modulate(rmsnorm(x, pre_norm_ffn), shift_f, scale_f)
    y2 = moe_ffn(h2, w_router, router_bias, w_gate_e, w_up_e, w_down_e, w_gate_s, w_up_s, w_down_s)
    x = x + gate_f[:, None, :] * rmsnorm(y2, post_norm_ffn)
    return x


def setup_inputs(seed: int = 0) -> dict:
    key = jax.random.key(seed)
    ks = jax.random.split(key, 22)
    L, D = DEPTH, D_MODEL

    def nrm(k, shape, scale):
        return jax.random.normal(k, shape, jnp.float32) * scale

    return {
        'x': nrm(ks[0], (BATCH, SEQ, D), 1.0),
        'c': nrm(ks[1], (BATCH, D), 1.0),
        'w_ada': nrm(ks[2], (L, D, 6 * D), 0.5 * D ** -0.5),
        'b_ada': nrm(ks[3], (L, 6 * D), 0.02),
        'pre_norm_mix': 1.0 + nrm(ks[4], (L, D), 0.1),
        'post_norm_mix': 1.0 + nrm(ks[5], (L, D), 0.1),
        'w_in': nrm(ks[6], (L, D, IN_COLS), D ** -0.5),
        'hgrn_lb_logits': nrm(ks[7], (DEPTH + 1, HGRN_WIDTH), 0.5),
        'hgrn_norm': 1.0 + nrm(ks[8], (L, HGRN_HEADS * HGRN_DV), 0.1),
        'w_branch_attn': nrm(ks[9], (L, ATT_OUT_WIDTH, D), ATT_OUT_WIDTH ** -0.5),
        'w_branch_hgrn': nrm(ks[10], (L, HGRN_HEADS * HGRN_DV, D), (HGRN_HEADS * HGRN_DV) ** -0.5),
        'w_out': nrm(ks[11], (L, D, D), D ** -0.5),
        'pre_norm_ffn': 1.0 + nrm(ks[12], (L, D), 0.1),
        'post_norm_ffn': 1.0 + nrm(ks[13], (L, D), 0.1),
        'w_router': nrm(ks[14], (L, D, N_EXPERTS), D ** -0.5),
        'router_bias': nrm(ks[15], (L, N_EXPERTS), 0.01),
        'w_gate_e': nrm(ks[16], (L, N_EXPERTS, D, D_EXPERT), D ** -0.5),
        'w_up_e': nrm(ks[17], (L, N_EXPERTS, D, D_EXPERT), D ** -0.5),
        'w_down_e': nrm(ks[18], (L, N_EXPERTS, D_EXPERT, D), D_EXPERT ** -0.5),
        'w_gate_s': nrm(ks[19], (L, D, D_SHARED), D ** -0.5),
        'w_up_s': nrm(ks[20], (L, D, D_SHARED), D ** -0.5),
        'w_down_s': nrm(ks[21], (L, D_SHARED, D), D_SHARED ** -0.5),
    }


def reference(x, c, w_ada, b_ada, pre_norm_mix, post_norm_mix, w_in, hgrn_lb_logits, hgrn_norm,
              w_branch_attn, w_branch_hgrn, w_out, pre_norm_ffn, post_norm_ffn, w_router, router_bias,
              w_gate_e, w_up_e, w_down_e, w_gate_s, w_up_s, w_down_s):
    lb_table = jnp.cumsum(jax.nn.softmax(hgrn_lb_logits.astype(jnp.float32), axis=0), axis=0)
    for l in range(DEPTH):
        x = hybrid_layer(x, c, lb_table[l], w_ada[l], b_ada[l], pre_norm_mix[l], post_norm_mix[l], w_in[l],
                         hgrn_norm[l], w_branch_attn[l], w_branch_hgrn[l], w_out[l], pre_norm_ffn[l],
                         post_norm_ffn[l], w_router[l], router_bias[l], w_gate_e[l], w_up_e[l], w_down_e[l],
                         w_gate_s[l], w_up_s[l], w_down_s[l])
    return x
```

```python
import functools

import jax
import jax.numpy as jnp
from jax import lax
from jax.experimental import pallas as pl
from jax.experimental.pallas import tpu as pltpu

F32 = jnp.float32
BF16 = jnp.bfloat16

HEAD_DIM = 128
ATT_GROUPS = ((128, 1), (512, 4), (2048, 16))
ATT_HPG = 4
ATT_HEADS = ATT_HPG * len(ATT_GROUPS)
ATT_WIDTH = ATT_HEADS * HEAD_DIM
ATT_GW = ATT_HPG * HEAD_DIM
ATT_BLOCK = 128
ATT_OUT_COLS = ATT_GW + HEAD_DIM
LSE_LANES = HEAD_DIM // ATT_HPG
HGRN_HEADS = 8
HGRN_D = 128
HGRN_WIDTH = HGRN_HEADS * HGRN_D
HGRN_CHUNK = 64
HGRN_DIRECT = 8
N_EXPERTS = 64
N_GROUPS = 8
TOPK_GROUPS = 4
TOP_K = 8
ROUTED_SCALE = 2.5
MOE_BLOCK = 128
NORM_EPS = 1e-6
V7X_VMEM_LIMIT = 56 * 1024 * 1024

Q_OFF = 0
K_OFF = ATT_WIDTH
V_OFF = 2 * ATT_WIDTH
RQ_OFF = 3 * ATT_WIDTH
RF_OFF = RQ_OFF + HGRN_WIDTH
RI_OFF = RF_OFF + HGRN_WIDTH
RG_OFF = RI_OFF + HGRN_WIDTH
GATE_OFF = RG_OFF + HGRN_WIDTH


def _pick(n, cands):
    for c in cands:
        if n % c == 0:
            return c
    raise ValueError(f"no tile of {cands} divides {n}")


def _params(sem, vmem_mib=None):
    kw = dict(dimension_semantics=sem)
    if vmem_mib is not None:
        kw["vmem_limit_bytes"] = min(vmem_mib * 1024 * 1024, V7X_VMEM_LIMIT)
    return pltpu.CompilerParams(**kw)


def _dot(a, b):
    return jnp.dot(a, b, preferred_element_type=F32)


def _dot_nt(a, b):
    return lax.dot_general(a, b, (((1,), (1,)), ((), ())), preferred_element_type=F32)


def _dot_tn(a, b):
    return lax.dot_general(a, b, (((0,), (0,)), ((), ())), preferred_element_type=F32)


def _rms(y):
    return y * lax.rsqrt(jnp.mean(y * y, axis=-1, keepdims=True) + NORM_EPS)


def _silu(a):
    return a * jax.nn.sigmoid(a)


ADA_KCHUNK = 128


def _ada_kernel(ct_ref, w_ref, b_ref, o_ref, *, nb, d):
    a = _silu(ct_ref[...])
    tn = w_ref.shape[1]
    accs = [jnp.zeros((8, tn), F32) for _ in range(nb)]
    for kc in range(d // ADA_KCHUNK):
        wc = w_ref[kc * ADA_KCHUNK:(kc + 1) * ADA_KCHUNK, :]
        ac = a[kc * ADA_KCHUNK:(kc + 1) * ADA_KCHUNK, :]
        for b in range(nb):
            p = ac[:, b:b + 1] * wc
            accs[b] = accs[b] + p.reshape(ADA_KCHUNK // 8, 8, tn).sum(axis=0)
    rows = [jnp.sum(acc, axis=0, keepdims=True) for acc in accs]
    o_ref[...] = jnp.concatenate(rows, axis=0) + b_ref[...]


def _ada(c, w_ada, b_ada):
    nb, d = c.shape
    n = w_ada.shape[1]
    tn = _pick(n, (512, 256, 128))
    return pl.pallas_call(
        functools.partial(_ada_kernel, nb=nb, d=d),
        out_shape=jax.ShapeDtypeStruct((nb, n), F32),
        grid=(n // tn,),
        in_specs=[pl.BlockSpec((d, nb), lambda j: (0, 0)),
                  pl.BlockSpec((d, tn), lambda j: (0, j)),
                  pl.BlockSpec((1, tn), lambda j: (0, j))],
        out_specs=pl.BlockSpec((nb, tn), lambda j: (0, j)),
        compiler_params=_params(("arbitrary",), 32),
        name="ada_mod",
    )(c.T, w_ada, b_ada.reshape(1, n))


def _inproj_kernel(x_ref, g_ref, sh_ref, sc_ref, w_ref, o_ref, h_scr):
    @pl.when(pl.program_id(1) == 0)
    def _():
        y = _rms(x_ref[...]) * g_ref[...]
        h_scr[...] = (y * (1.0 + sc_ref[...]) + sh_ref[...]).astype(BF16)

    o_ref[...] = _dot(h_scr[...], w_ref[...]).astype(o_ref.dtype)


def _inproj(x2, gain, shift, scale, w_bf, seq):
    t, d = x2.shape
    n = w_bf.shape[1]
    tm = _pick(seq, (512, 256, 128))
    tn = _pick(n, (1280, 1024, 768, 512, 256, 128))
    per = seq // tm
    return pl.pallas_call(
        _inproj_kernel,
        out_shape=jax.ShapeDtypeStruct((t, n), BF16),
        grid=(t // tm, n // tn),
        in_specs=[pl.BlockSpec((tm, d), lambda i, j: (i, 0)),
                  pl.BlockSpec((1, d), lambda i, j: (0, 0)),
                  pl.BlockSpec((None, 1, d), lambda i, j: (i // per, 0, 0)),
                  pl.BlockSpec((None, 1, d), lambda i, j: (i // per, 0, 0)),
                  pl.BlockSpec((d, tn), lambda i, j: (0, j))],
        out_specs=pl.BlockSpec((tm, tn), lambda i, j: (i, j)),
        scratch_shapes=[pltpu.VMEM((tm, d), BF16)],
        compiler_params=_params(("parallel", "arbitrary"), 48),
        name="in_proj",
    )(x2, gain.reshape(1, d), shift, scale, w_bf)


def _attn_kernel(q_ref, kp_ref, kc_ref, vp_ref, vc_ref, o_ref, *, dil, slopes, scale):
    n = pl.program_id(2)
    qi = lax.broadcasted_iota(jnp.int32, (ATT_BLOCK, ATT_BLOCK), 0)
    ki = lax.broadcasted_iota(jnp.int32, (ATT_BLOCK, ATT_BLOCK), 1)
    jc = qi - ki
    jp = jc + ATT_BLOCK
    prev_lim = jnp.where(n > 0, ATT_BLOCK, -1)
    valid_c = jc >= 0
    valid_p = jp <= prev_lim
    lane = lax.broadcasted_iota(jnp.int32, (ATT_BLOCK, HEAD_DIM), 1)
    lse_blk = jnp.zeros((ATT_BLOCK, HEAD_DIM), F32)
    for h in range(ATT_HPG):
        hs = slice(h * HEAD_DIM, (h + 1) * HEAD_DIM)
        q = q_ref[:, hs]
        bias = slopes[h] * dil
        s_c = _dot_nt(q, kc_ref[:, hs]) * scale - bias * jc.astype(F32)
        s_p = _dot_nt(q, kp_ref[:, hs]) * scale - bias * jp.astype(F32)
        s_c = jnp.where(valid_c, s_c, -jnp.inf)
        s_p = jnp.where(valid_p, s_p, -jnp.inf)
        m = jnp.maximum(jnp.max(s_c, axis=-1, keepdims=True), jnp.max(s_p, axis=-1, keepdims=True))
        p_c = jnp.exp(s_c - m)
        p_p = jnp.exp(s_p - m)
        l = jnp.sum(p_c, axis=-1, keepdims=True) + jnp.sum(p_p, axis=-1, keepdims=True)
        o = _dot(p_c.astype(BF16), vc_ref[:, hs]) + _dot(p_p.astype(BF16), vp_ref[:, hs])
        o_ref[:, hs] = o / l
        lse = m + jnp.log(l)
        lse_blk = jnp.where((lane >= h * LSE_LANES) & (lane < (h + 1) * LSE_LANES), lse, lse_blk)
    o_ref[:, ATT_GW:] = lse_blk


def _attention_group(proj3, g, seq):
    nb_, s, nc = proj3.shape
    window, dil = ATT_GROUPS[g]
    assert window // dil == ATT_BLOCK and s % (dil * ATT_BLOCK) == 0 and nc % ATT_GW == 0
    l = s // dil
    nblk = l // ATT_BLOCK
    ncb = nc // ATT_GW
    pv = proj3.reshape(nb_, l, dil * nc)
    slopes = tuple(2.0 ** (-8.0 * (g * ATT_HPG + i + 1) / ATT_HEADS) for i in range(ATT_HPG))

    def spec(off, prev):
        cb = off // ATT_GW + g
        if prev:
            return pl.BlockSpec((None, ATT_BLOCK, ATT_GW),
                                lambda b, r, n: (b, jnp.maximum(n - 1, 0), r * ncb + cb))
        return pl.BlockSpec((None, ATT_BLOCK, ATT_GW), lambda b, r, n: (b, n, r * ncb + cb))

    out = pl.pallas_call(
        functools.partial(_attn_kernel, dil=float(dil), slopes=slopes, scale=HEAD_DIM ** -0.5),
        out_shape=jax.ShapeDtypeStruct((nb_, l, dil * ATT_OUT_COLS), F32),
        grid=(nb_, dil, nblk),
        in_specs=[spec(Q_OFF, False), spec(K_OFF, True), spec(K_OFF, False),
                  spec(V_OFF, True), spec(V_OFF, False)],
        out_specs=pl.BlockSpec((None, ATT_BLOCK, ATT_OUT_COLS), lambda b, r, n: (b, n, r)),
        compiler_params=_params(("parallel", "parallel", "arbitrary")),
        name=f"attn_g{g}",
    )(pv, pv, pv, pv, pv)
    return out.reshape(nb_ * s, ATT_OUT_COLS)


def _hgrn_kernel(q_ref, f_ref, i_ref, g_ref, lb_ref, gn_ref, tri_ref, o_ref, st_ref, sh_ref, *, tb):
    @pl.when(pl.program_id(2) == 0)
    def _():
        st_ref[...] = jnp.zeros_like(st_ref)
        sh_ref[:, 0:HGRN_DIRECT, :] = jnp.zeros((3, HGRN_DIRECT, HGRN_D), F32)

    c_ = HGRN_CHUNK
    nc = tb // c_
    q = q_ref[...].astype(F32)
    v = i_ref[...].astype(F32)
    lb = lb_ref[...]
    f = lb + (1.0 - lb) * jax.nn.sigmoid(f_ref[...].astype(F32))
    kk = 1.0 - f
    lf = jnp.log(f)

    hi = lf.astype(BF16)
    r1 = lf - hi.astype(F32)
    mid = r1.astype(BF16)
    lo = (r1 - mid.astype(F32)).astype(BF16)
    tri = tri_ref[...]
    b = _dot(tri, hi) + _dot(tri, mid) + _dot(tri, lo)

    def ref_rows(blk, row):
        b3 = b.reshape(tb // blk, blk, HGRN_D)
        return jnp.broadcast_to(b3[:, row:row + 1, :], (tb // blk, blk, HGRN_D)).reshape(tb, HGRN_D)

    ti = lax.broadcasted_iota(jnp.int32, (c_, c_), 0)
    si = lax.broadcasted_iota(jnp.int32, (c_, c_), 1)
    ssum = jnp.zeros((nc, c_, c_), F32)
    half = c_ // 2
    while half >= HGRN_DIRECT:
        bref = ref_rows(2 * half, half)
        ql = (q * jnp.exp(jnp.minimum(b - bref, 0.0))).astype(BF16).reshape(nc, c_, HGRN_D)
        kl = (kk * jnp.exp(jnp.minimum(bref - b, 0.0))).astype(BF16).reshape(nc, c_, HGRN_D)
        s_l = jnp.einsum("ctk,csk->cts", ql, kl, preferred_element_type=F32)
        mask = ((ti // (2 * half)) == (si // (2 * half))) & (((ti // half) % 2) == 1) & (((si // half) % 2) == 0)
        ssum = ssum + jnp.where(mask[None], s_l, 0.0)
        half //= 2
    v3 = v.astype(BF16).reshape(nc, c_, HGRN_D)
    o_acc = jnp.einsum("cts,csv->ctv", ssum.astype(BF16), v3, preferred_element_type=F32).reshape(tb, HGRN_D)

    sh_ref[0, HGRN_DIRECT:, :] = kk
    sh_ref[1, HGRN_DIRECT:, :] = b
    sh_ref[2, HGRN_DIRECT:, :] = v
    tmod = lax.broadcasted_iota(jnp.int32, (tb, HGRN_D), 0) % HGRN_DIRECT
    ones = jnp.ones((HGRN_D, HGRN_D), BF16)
    for lag in range(HGRN_DIRECT):
        st = HGRN_DIRECT - lag
        kd = sh_ref[0, st:st + tb, :]
        bd = sh_ref[1, st:st + tb, :]
        vd = sh_ref[2, st:st + tb, :]
        w = jnp.where(tmod >= lag, q * kd * jnp.exp(jnp.minimum(b - bd, 0.0)), 0.0)
        w_hi = w.astype(BF16)
        w_lo = (w - w_hi.astype(F32)).astype(BF16)
        o_acc = o_acc + (_dot(w_hi, ones) + _dot(w_lo, ones)) * vd

    blast = ref_rows(c_, c_ - 1)
    qe = (q * jnp.exp(b)).astype(BF16)
    kt = (kk * jnp.exp(blast - b)).astype(BF16)
    vb = v.astype(BF16)
    st_t = st_ref[...]
    inter = []
    for c in range(nc):
        rows = slice(c * c_, (c + 1) * c_)
        inter.append(_dot_nt(qe[rows], st_t.astype(BF16)))
        dec = jnp.exp(blast[c * c_:c * c_ + 1, :])
        st_t = st_t * dec + _dot_tn(vb[rows], kt[rows])
    st_ref[...] = st_t
    o = o_acc + jnp.concatenate(inter, axis=0)

    y = _rms(o) * gn_ref[...]
    o_ref[...] = (y * _silu(g_ref[...].astype(F32))).astype(o_ref.dtype)


def _hgrn(proj3, lb, gnorm):
    nb_, s, nc = proj3.shape
    tb = _pick(s, (512, 256, 128, 64))
    idx = jnp.arange(tb)
    tri = ((idx[:, None] // HGRN_CHUNK == idx[None, :] // HGRN_CHUNK) & (idx[None, :] <= idx[:, None])).astype(BF16)

    def spec(off):
        cb = off // HGRN_D
        return pl.BlockSpec((None, tb, HGRN_D), lambda b, h, n: (b, n, cb + h))

    vec = pl.BlockSpec((1, HGRN_D), lambda b, h, n: (0, h))
    out = pl.pallas_call(
        functools.partial(_hgrn_kernel, tb=tb),
        out_shape=jax.ShapeDtypeStruct((nb_, s, HGRN_WIDTH), BF16),
        grid=(nb_, HGRN_HEADS, s // tb),
        in_specs=[spec(RQ_OFF), spec(RF_OFF), spec(RI_OFF), spec(RG_OFF), vec, vec,
                  pl.BlockSpec((tb, tb), lambda b, h, n: (0, 0))],
        out_specs=pl.BlockSpec((None, tb, HGRN_D), lambda b, h, n: (b, n, h)),
        scratch_shapes=[pltpu.VMEM((HGRN_D, HGRN_D), F32),
                        pltpu.VMEM((3, tb + HGRN_DIRECT, HGRN_D), F32)],
        compiler_params=_params(("parallel", "parallel", "arbitrary"), 32),
        name="hgrn2",
    )(proj3, proj3, proj3, proj3, lb.reshape(1, HGRN_WIDTH), gnorm.reshape(1, HGRN_WIDTH), tri)
    return out.reshape(nb_ * s, HGRN_WIDTH)


def _branch_kernel(a0_ref, a1_ref, a2_ref, r_ref, ga_ref, gb_ref, wa_ref, wr_ref, o_ref, att_scr):
    @pl.when(pl.program_id(1) == 0)
    def _():
        refs = (a0_ref, a1_ref, a2_ref)
        for h in range(ATT_HPG):
            col = ATT_GW + h * LSE_LANES
            ls = [r[:, col:col + 1] for r in refs]
            m = jnp.maximum(jnp.maximum(ls[0], ls[1]), ls[2])
            es = [jnp.exp(l - m) for l in ls]
            den = es[0] + es[1] + es[2]
            hs = slice(h * HEAD_DIM, (h + 1) * HEAD_DIM)
            o = ((es[0] / den) * refs[0][:, hs] + (es[1] / den) * refs[1][:, hs]) + (es[2] / den) * refs[2][:, hs]
            att_scr[:, hs] = o.astype(BF16)

    a = _dot(att_scr[...], wa_ref[...])
    r = _dot(r_ref[...], wr_ref[...])
    o_ref[...] = (jax.nn.sigmoid(ga_ref[...].astype(F32)) * a
                  + jax.nn.sigmoid(gb_ref[...].astype(F32)) * r).astype(o_ref.dtype)


def _branch(att, orec, proj2, wa_bf, wr_bf, d):
    t = orec.shape[0]
    tm = _pick(t, (512, 256, 128))
    gw = _pick(d, (512, 256, 128))
    assert GATE_OFF % gw == 0
    ga0 = GATE_OFF // gw
    gb0 = (GATE_OFF + d) // gw
    aspec = pl.BlockSpec((tm, ATT_OUT_COLS), lambda i, j: (i, 0))
    return pl.pallas_call(
        _branch_kernel,
        out_shape=jax.ShapeDtypeStruct((t, d), BF16),
        grid=(t // tm, d // gw),
        in_specs=[aspec, aspec, aspec,
                  pl.BlockSpec((tm, HGRN_WIDTH), lambda i, j: (i, 0)),
                  pl.BlockSpec((tm, gw), lambda i, j: (i, ga0 + j)),
                  pl.BlockSpec((tm, gw), lambda i, j: (i, gb0 + j)),
                  pl.BlockSpec((ATT_GW, gw), lambda i, j: (0, j)),
                  pl.BlockSpec((HGRN_WIDTH, gw), lambda i, j: (0, j))],
        out_specs=pl.BlockSpec((tm, gw), lambda i, j: (i, j)),
        scratch_shapes=[pltpu.VMEM((tm, ATT_GW), BF16)],
        compiler_params=_params(("parallel", "arbitrary"), 32),
        name="branch_merge",
    )(att[0], att[1], att[2], orec, proj2, proj2, wa_bf, wr_bf)


def _outproj_kernel(m_ref, w_ref, x_ref, gm_ref, pnm_ref, pnf_ref, sh_ref, sc_ref, x1_ref, hf_ref, hb_ref):
    y = _dot(m_ref[...], w_ref[...])
    x1 = x_ref[...] + gm_ref[...] * (_rms(y) * pnm_ref[...])
    x1_ref[...] = x1
    h2 = (_rms(x1) * pnf_ref[...]) * (1.0 + sc_ref[...]) + sh_ref[...]
    hf_ref[...] = h2
    hb_ref[...] = h2.astype(BF16)


def _outproj(merged, w_bf, x2, gate_m, pnm, pnf, shift_f, scale_f, seq):
    t, d = x2.shape
    tm = _pick(seq, (256, 128))
    per = seq // tm
    row = pl.BlockSpec((tm, d), lambda i: (i, 0))
    vec = pl.BlockSpec((1, d), lambda i: (0, 0))
    bvec = pl.BlockSpec((None, 1, d), lambda i: (i // per, 0, 0))
    return pl.pallas_call(
        _outproj_kernel,
        out_shape=(jax.ShapeDtypeStruct((t, d), F32), jax.ShapeDtypeStruct((t, d), F32),
                   jax.ShapeDtypeStruct((t, d), BF16)),
        grid=(t // tm,),
        in_specs=[row, pl.BlockSpec((d, d), lambda i: (0, 0)), row, bvec, vec, vec, bvec, bvec],
        out_specs=(row, row, row),
        compiler_params=_params(("parallel",), 48),
        name="out_proj",
    )(merged, w_bf, x2, gate_m, pnm.reshape(1, d), pnf.reshape(1, d), shift_f, scale_f)


def _router_kernel(h_ref, w_ref, bias_ref, up_ref, eid_ref, gate_ref, pos_ref, cnt_ref, carry_ref, *, tr):
    @pl.when(pl.program_id(0) == 0)
    def _():
        carry_ref[...] = jnp.zeros_like(carry_ref)

    per_group = N_EXPERTS // N_GROUPS
    sig = jax.nn.sigmoid(_dot_nt(w_ref[...], h_ref[...]))
    choice = sig + bias_ref[...]
    eidx = lax.broadcasted_iota(jnp.int32, (N_EXPERTS, tr), 0)

    c3 = choice.reshape(N_GROUPS, per_group, tr)
    sub = lax.broadcasted_iota(jnp.int32, (N_GROUPS, per_group, tr), 1)
    m1 = jnp.max(c3, axis=1, keepdims=True)
    first = jnp.min(jnp.where(c3 == m1, sub, per_group), axis=1, keepdims=True)
    m2 = jnp.max(jnp.where(sub == first, -jnp.inf, c3), axis=1, keepdims=True)
    gs = (m1 + m2).reshape(N_GROUPS, tr)

    gidx = lax.broadcasted_iota(jnp.int32, (N_GROUPS, tr), 0)
    grank = jnp.zeros((N_GROUPS, tr), jnp.int32)
    for g in range(N_GROUPS):
        row = gs[g:g + 1, :]
        grank = grank + ((row > gs) | ((row == gs) & (gidx > g))).astype(jnp.int32)
    gsel = jnp.where(grank < TOPK_GROUPS, 1.0, 0.0)
    emask = jnp.broadcast_to(gsel.reshape(N_GROUPS, 1, tr), (N_GROUPS, per_group, tr)).reshape(N_EXPERTS, tr)
    cm = jnp.where(emask > 0.5, choice, -jnp.inf)

    rank = jnp.zeros((N_EXPERTS, tr), jnp.int32)
    for e in range(N_EXPERTS):
        row = cm[e:e + 1, :]
        rank = rank + ((row > cm) | ((row == cm) & (eidx > e))).astype(jnp.int32)
    sel = rank < TOP_K

    denom = jnp.sum(jnp.where(sel, sig, 0.0), axis=0, keepdims=True)
    gate_full = sig / denom * ROUTED_SCALE

    sel_b = jnp.where(sel, 1.0, 0.0).astype(BF16)
    carry = carry_ref[...]
    cum = _dot(sel_b, up_ref[...]) + jnp.concatenate([carry] * (tr // 128), axis=1)
    carry_new = carry + _dot(sel_b, jnp.ones((tr, 128), BF16))
    carry_ref[...] = carry_new
    cnt_ref[...] = carry_new.astype(jnp.int32)
    posi = cum.astype(jnp.int32)

    eids, gates, poss = [], [], []
    for r in range(TOP_K):
        hit = rank == r
        eids.append(jnp.sum(jnp.where(hit, eidx, 0), axis=0, keepdims=True))
        gates.append(jnp.sum(jnp.where(hit, gate_full, 0.0), axis=0, keepdims=True))
        poss.append(jnp.sum(jnp.where(hit, posi, 0), axis=0, keepdims=True))
    eid_ref[...] = jnp.concatenate(eids, axis=0)
    gate_ref[...] = jnp.concatenate(gates, axis=0)
    pos_ref[...] = jnp.concatenate(poss, axis=0)


def _router(h2b, w_router, router_bias):
    t, d = h2b.shape
    tr = _pick(t, (256, 128))
    idx = jnp.arange(tr)
    upper = (idx[:, None] < idx[None, :]).astype(BF16)
    kout = pl.BlockSpec((TOP_K, tr), lambda i: (0, i))
    return pl.pallas_call(
        functools.partial(_router_kernel, tr=tr),
        out_shape=(jax.ShapeDtypeStruct((TOP_K, t), jnp.int32), jax.ShapeDtypeStruct((TOP_K, t), F32),
                   jax.ShapeDtypeStruct((TOP_K, t), jnp.int32), jax.ShapeDtypeStruct((N_EXPERTS, 128), jnp.int32)),
        grid=(t // tr,),
        in_specs=[pl.BlockSpec((tr, d), lambda i: (i, 0)),
                  pl.BlockSpec((N_EXPERTS, d), lambda i: (0, 0)),
                  pl.BlockSpec((N_EXPERTS, 1), lambda i: (0, 0)),
                  pl.BlockSpec((tr, tr), lambda i: (0, 0))],
        out_specs=(kout, kout, kout, pl.BlockSpec((N_EXPERTS, 128), lambda i: (0, 0))),
        scratch_shapes=[pltpu.VMEM((N_EXPERTS, 128), F32)],
        compiler_params=_params(("arbitrary",), 32),
        name="router",
    )(h2b, w_router.T.astype(BF16), router_bias.reshape(N_EXPERTS, 1).astype(F32), upper)


def _row_copy(src_hbm, idx, dst, sem):
    return pltpu.make_async_copy(src_hbm.at[pl.ds(idx, 1), :], dst, sem)


def _expert_kernel(be_ref, nu_ref, tokc_ref, tokn_ref, w_ref, h_hbm, wg_ref, wu_ref, wd_ref, o_ref, buf, sem):
    i = pl.program_id(0)
    n_used = nu_ref[0]
    slot = i % 2

    def issue(tok_ref, s):
        def body(r, carry):
            _row_copy(h_hbm, tok_ref[0, r], buf.at[s, pl.ds(r, 1), :], sem.at[s]).start()
            return carry
        lax.fori_loop(0, MOE_BLOCK, body, 0)

    @pl.when(i == 0)
    def _():
        issue(tokc_ref, 0)

    @pl.when(i + 1 < n_used)
    def _():
        issue(tokn_ref, 1 - slot)

    @pl.when(i < n_used)
    def _():
        def wbody(r, carry):
            _row_copy(h_hbm, 0, buf.at[slot, pl.ds(r, 1), :], sem.at[slot]).wait()
            return carry
        lax.fori_loop(0, MOE_BLOCK, wbody, 0)
        xb = buf[slot].astype(BF16)
        hid = _silu(_dot(xb, wg_ref[...])) * _dot(xb, wu_ref[...])
        o_ref[...] = _dot(hid.astype(BF16), wd_ref[...]) * w_ref[...]

    @pl.when(i >= n_used)
    def _():
        o_ref[...] = jnp.zeros_like(o_ref)


def _experts(h2f, slot_tok, slot_w, block_expert, n_used, wg, wu, wd):
    t, d = h2f.shape
    n_blocks = block_expert.shape[0]
    de = wg.shape[-1]
    tok3 = slot_tok.reshape(n_blocks, 1, MOE_BLOCK)
    smem = functools.partial(pl.BlockSpec, memory_space=pltpu.SMEM)
    grid_spec = pltpu.PrefetchScalarGridSpec(
        num_scalar_prefetch=2,
        grid=(n_blocks,),
        in_specs=[smem((None, 1, MOE_BLOCK), lambda i, be, nu: (i, 0, 0)),
                  smem((None, 1, MOE_BLOCK), lambda i, be, nu: (jnp.minimum(i + 1, n_blocks - 1), 0, 0)),
                  pl.BlockSpec((MOE_BLOCK, 1), lambda i, be, nu: (i, 0)),
                  pl.BlockSpec(memory_space=pl.ANY),
                  pl.BlockSpec((None, d, de), lambda i, be, nu: (be[i], 0, 0)),
                  pl.BlockSpec((None, d, de), lambda i, be, nu: (be[i], 0, 0)),
                  pl.BlockSpec((None, de, d), lambda i, be, nu: (be[i], 0, 0))],
        out_specs=pl.BlockSpec((MOE_BLOCK, d), lambda i, be, nu: (i, 0)),
        scratch_shapes=[pltpu.VMEM((2, MOE_BLOCK, d), F32), pltpu.SemaphoreType.DMA((2,))],
    )
    return pl.pallas_call(
        _expert_kernel,
        out_shape=jax.ShapeDtypeStruct((n_blocks * MOE_BLOCK, d), F32),
        grid_spec=grid_spec,
        compiler_params=_params(("arbitrary",), 48),
        name="experts",
    )(block_expert, n_used, tok3, tok3, slot_w.reshape(-1, 1), h2f, wg, wu, wd)


def _combine_kernel(dc_ref, dn_ref, y_hbm, h_ref, x1_ref, wg_ref, wu_ref, wd_ref, gf_ref, pn_ref,
                    o_ref, buf, sem, *, tm):
    i = pl.program_id(0)
    nsteps = pl.num_programs(0)
    slot = i % 2

    def issue(d_ref, s):
        for k in range(TOP_K):
            def body(r, carry, k=k):
                _row_copy(y_hbm, d_ref[0, k * tm + r], buf.at[s, k, pl.ds(r, 1), :], sem.at[s]).start()
                return carry
            lax.fori_loop(0, tm, body, 0)

    @pl.when(i == 0)
    def _():
        issue(dc_ref, 0)

    @pl.when(i + 1 < nsteps)
    def _():
        issue(dn_ref, 1 - slot)

    hb = h_ref[...]
    shared = _dot((_silu(_dot(hb, wg_ref[...])) * _dot(hb, wu_ref[...])).astype(BF16), wd_ref[...])

    def wbody(r, carry):
        _row_copy(y_hbm, 0, buf.at[slot, 0, pl.ds(0, 1), :], sem.at[slot]).wait()
        return carry
    lax.fori_loop(0, TOP_K * tm, wbody, 0)

    acc = buf[slot, 0]
    for k in range(1, TOP_K):
        acc = acc + buf[slot, k]
    y2 = acc + shared
    o_ref[...] = x1_ref[...] + gf_ref[...] * (_rms(y2) * pn_ref[...])


def _combine(y_sorted, dest, h2b, x1, wgs, wus, wds, gate_f, pnf, seq):
    t, d = x1.shape
    ds_ = wgs.shape[-1]
    tm = _pick(seq, (128,))
    per = seq // tm
    nt = t // tm
    dest3 = dest.reshape(TOP_K, nt, tm).transpose(1, 0, 2).reshape(nt, 1, TOP_K * tm)
    smem = functools.partial(pl.BlockSpec, memory_space=pltpu.SMEM)
    row = pl.BlockSpec((tm, d), lambda i: (i, 0))
    return pl.pallas_call(
        functools.partial(_combine_kernel, tm=tm),
        out_shape=jax.ShapeDtypeStruct((t, d), F32),
        grid=(nt,),
        in_specs=[smem((None, 1, TOP_K * tm), lambda i: (i, 0, 0)),
                  smem((None, 1, TOP_K * tm), lambda i: (jnp.minimum(i + 1, nt - 1), 0, 0)),
                  pl.BlockSpec(memory_space=pl.ANY),
                  row, row,
                  pl.BlockSpec((d, ds_), lambda i: (0, 0)),
                  pl.BlockSpec((d, ds_), lambda i: (0, 0)),
                  pl.BlockSpec((ds_, d), lambda i: (0, 0)),
                  pl.BlockSpec((None, 1, d), lambda i: (i // per, 0, 0)),
                  pl.BlockSpec((1, d), lambda i: (0, 0))],
        out_specs=row,
        scratch_shapes=[pltpu.VMEM((2, TOP_K, tm, d), F32), pltpu.SemaphoreType.DMA((2,))],
        compiler_params=_params(("arbitrary",), 48),
        name="combine",
    )(dest3, dest3, y_sorted, h2b, x1, wgs, wus, wds, gate_f, pnf.reshape(1, d))


def _layer(x, c, lb, w_ada, b_ada, pre_norm_mix, post_norm_mix, w_in, hgrn_norm, w_branch_attn, w_branch_hgrn,
           w_out, pre_norm_ffn, post_norm_ffn, w_router, router_bias, w_gate_e, w_up_e, w_down_e,
           w_gate_s, w_up_s, w_down_s):
    nb, s, d = x.shape
    t = nb * s
    x2 = x.reshape(t, d)

    mod = _ada(c, w_ada, b_ada).reshape(nb, 6, 1, d)
    shift_m, scale_m, gate_m, shift_f, scale_f, gate_f = (mod[:, k] for k in range(6))

    proj2 = _inproj(x2, pre_norm_mix, shift_m, scale_m, w_in.astype(BF16), s)
    proj3 = proj2.reshape(nb, s, -1)

    att = [_attention_group(proj3, g, s) for g in range(len(ATT_GROUPS))]
    orec = _hgrn(proj3, lb, hgrn_norm)
    merged = _branch(att, orec, proj2, w_branch_attn.astype(BF16), w_branch_hgrn.astype(BF16), d)
    x1, h2f, h2b = _outproj(merged, w_out.astype(BF16), x2, gate_m, post_norm_mix, pre_norm_ffn,
                            shift_f, scale_f, s)

    eid, gate, pos, cnt = _router(h2b, w_router, router_bias)

    counts = cnt[:, 0]
    padded = (counts + MOE_BLOCK - 1) // MOE_BLOCK * MOE_BLOCK
    pad_end = jnp.cumsum(padded)
    pad_start = pad_end - padded
    n_blocks = -(-(t * TOP_K) // MOE_BLOCK) + N_EXPERTS
    dest = pad_start[eid] + pos
    tok = jnp.broadcast_to(jnp.arange(t, dtype=jnp.int32)[None, :], (TOP_K, t))
    slot_tok = jnp.zeros((n_blocks * MOE_BLOCK,), jnp.int32).at[dest.reshape(-1)].set(tok.reshape(-1))
    slot_w = jnp.zeros((n_blocks * MOE_BLOCK,), F32).at[dest.reshape(-1)].set(gate.reshape(-1))
    block_start = jnp.arange(n_blocks, dtype=jnp.int32) * MOE_BLOCK
    block_expert = jnp.minimum(jnp.searchsorted(pad_end, block_start, side="right"), N_EXPERTS - 1).astype(jnp.int32)
    n_used = (pad_end[-1:] // MOE_BLOCK).astype(jnp.int32)

    y_sorted = _experts(h2f, slot_tok, slot_w, block_expert, n_used,
                        w_gate_e.astype(BF16), w_up_e.astype(BF16), w_down_e.astype(BF16))
    out = _combine(y_sorted, dest, h2b, x1, w_gate_s.astype(BF16), w_up_s.astype(BF16), w_down_s.astype(BF16),
                   gate_f, post_norm_ffn, s)
    return out.reshape(nb, s, d)


def kernel(x, c, w_ada, b_ada, pre_norm_mix, post_norm_mix, w_in, hgrn_lb_logits, hgrn_norm, w_branch_attn,
           w_branch_hgrn, w_out, pre_norm_ffn, post_norm_ffn, w_router, router_bias, w_gate_e, w_up_e, w_down_e,
           w_gate_s, w_up_s, w_down_s):
    lb_table = jnp.cumsum(jax.nn.softmax(hgrn_lb_logits.astype(F32), axis=0), axis=0)
    depth = w_ada.shape[0]
    for l in range(depth):
        x = _layer(x, c, lb_table[l], w_ada[l], b_ada[l], pre_norm_mix[l], post_norm_mix[l], w_in[l],
                   hgrn_norm[l], w_branch_attn[l], w_branch_hgrn[l], w_out[l], pre_norm_ffn[l],
                   post_norm_ffn[l], w_router[l], router_bias[l], w_gate_e[l], w_up_e[l], w_down_e[l],
                   w_gate_s[l], w_up_s[l], w_down_s[l])
    return x
```

```python
import functools

import jax
import jax.numpy as jnp
from jax import lax
from jax.experimental import pallas as pl
from jax.experimental.pallas import tpu as pltpu

F32 = jnp.float32
BF16 = jnp.bfloat16

HEAD_DIM = 128
ATT_GROUPS = ((128, 1), (512, 4), (2048, 16))
ATT_HPG = 4
ATT_HEADS = ATT_HPG * len(ATT_GROUPS)
ATT_WIDTH = ATT_HEADS * HEAD_DIM
ATT_GW = ATT_HPG * HEAD_DIM
ATT_BLOCK = 128
ATT_OUT_COLS = ATT_GW + HEAD_DIM
LSE_LANES = HEAD_DIM // ATT_HPG
HGRN_HEADS = 8
HGRN_D = 128
HGRN_WIDTH = HGRN_HEADS * HGRN_D
HGRN_CHUNK = 64
HGRN_DIRECT = 8
N_EXPERTS = 64
N_GROUPS = 8
TOPK_GROUPS = 4
TOP_K = 8
ROUTED_SCALE = 2.5
MOE_BLOCK = 128
NORM_EPS = 1e-6
V7X_VMEM_LIMIT = 56 * 1024 * 1024

QKV_COLS = 3 * ATT_WIDTH
RQ_OFF = 0
RF_OFF = RQ_OFF + HGRN_WIDTH
RI_OFF = RF_OFF + HGRN_WIDTH
RG_OFF = RI_OFF + HGRN_WIDTH
GATE_OFF = RG_OFF + HGRN_WIDTH


def _pick(n, cands):
    for c in cands:
        if n % c == 0:
            return c
    raise ValueError(f"no tile of {cands} divides {n}")


def _params(sem, vmem_mib=None):
    kw = dict(dimension_semantics=sem)
    if vmem_mib is not None:
        kw["vmem_limit_bytes"] = min(vmem_mib * 1024 * 1024, V7X_VMEM_LIMIT)
    return pltpu.CompilerParams(**kw)


def _dot(a, b):
    return jnp.dot(a, b, preferred_element_type=F32)


def _dot_nt(a, b):
    return lax.dot_general(a, b, (((1,), (1,)), ((), ())), preferred_element_type=F32)


def _dot_tn(a, b):
    return lax.dot_general(a, b, (((0,), (0,)), ((), ())), preferred_element_type=F32)


def _rms(y):
    return y * lax.rsqrt(jnp.mean(y * y, axis=-1, keepdims=True) + NORM_EPS)


def _silu(a):
    return a * jax.nn.sigmoid(a)


ADA_KCHUNK = 128


def _ada_kernel(ct_ref, w_ref, b_ref, o_ref, *, nb, d):
    a = _silu(ct_ref[...])
    tn = w_ref.shape[1]
    accs = [jnp.zeros((8, tn), F32) for _ in range(nb)]
    for kc in range(d // ADA_KCHUNK):
        wc = w_ref[kc * ADA_KCHUNK:(kc + 1) * ADA_KCHUNK, :]
        ac = a[kc * ADA_KCHUNK:(kc + 1) * ADA_KCHUNK, :]
        for b in range(nb):
            p = ac[:, b:b + 1] * wc
            accs[b] = accs[b] + p.reshape(ADA_KCHUNK // 8, 8, tn).sum(axis=0)
    rows = [jnp.sum(acc, axis=0, keepdims=True) for acc in accs]
    o_ref[...] = jnp.concatenate(rows, axis=0) + b_ref[...]


def _ada(c, w_ada, b_ada):
    nb, d = c.shape
    n = w_ada.shape[1]
    tn = _pick(n, (512, 256, 128))
    return pl.pallas_call(
        functools.partial(_ada_kernel, nb=nb, d=d),
        out_shape=jax.ShapeDtypeStruct((nb, n), F32),
        grid=(n // tn,),
        in_specs=[pl.BlockSpec((d, nb), lambda j: (0, 0)),
                  pl.BlockSpec((d, tn), lambda j: (0, j)),
                  pl.BlockSpec((1, tn), lambda j: (0, j))],
        out_specs=pl.BlockSpec((nb, tn), lambda j: (0, j)),
        compiler_params=_params(("arbitrary",), 32),
        name="ada_mod",
    )(c.T, w_ada, b_ada.reshape(1, n))


def _inproj_kernel(x_ref, g_ref, sh_ref, sc_ref, w_ref, o_ref, h_scr, *slab, dil):
    @pl.when(pl.program_id(1) == 0)
    def _():
        y = _rms(x_ref[...]) * g_ref[...]
        h_scr[...] = (y * (1.0 + sc_ref[...]) + sh_ref[...]).astype(BF16)

    o = _dot(h_scr[...], w_ref[...])
    if dil == 1:
        o_ref[...] = o.astype(o_ref.dtype)
    else:
        slab_ref, = slab
        tm, tn = o.shape
        for c in range(tn // 128):
            slab_ref[c] = o[:, c * 128:(c + 1) * 128]
        for r in range(dil):
            for c in range(tn // 128):
                o_ref[r, :, c * 128:(c + 1) * 128] = slab_ref[c, pl.ds(r, tm // dil, stride=dil), :].astype(o_ref.dtype)


def _inproj(x2, gain, shift, scale, w_bf, nb, seq, dil=1):
    t, d = x2.shape
    n = w_bf.shape[1]
    tm = _pick(seq, (512, 256))
    tn = _pick(n, (1024, 512, 256, 128)) if dil == 1 else ATT_GW
    per = seq // tm
    if dil == 1:
        out_shape = jax.ShapeDtypeStruct((t, n), BF16)
        out_spec = pl.BlockSpec((tm, tn), lambda i, j: (i, j))
        scratch = [pltpu.VMEM((tm, d), BF16)]
    else:
        out_shape = jax.ShapeDtypeStruct((nb, dil, seq // dil, n), BF16)
        out_spec = pl.BlockSpec((None, dil, tm // dil, tn), lambda i, j: (i // per, 0, i % per, j))
        scratch = [pltpu.VMEM((tm, d), BF16), pltpu.VMEM((tn // 128, tm, 128), F32)]
    return pl.pallas_call(
        functools.partial(_inproj_kernel, dil=dil),
        out_shape=out_shape,
        grid=(t // tm, n // tn),
        in_specs=[pl.BlockSpec((tm, d), lambda i, j: (i, 0)),
                  pl.BlockSpec((1, d), lambda i, j: (0, 0)),
                  pl.BlockSpec((None, 1, d), lambda i, j: (i // per, 0, 0)),
                  pl.BlockSpec((None, 1, d), lambda i, j: (i // per, 0, 0)),
                  pl.BlockSpec((d, tn), lambda i, j: (0, j))],
        out_specs=out_spec,
        scratch_shapes=scratch,
        compiler_params=_params(("parallel", "arbitrary"), 48),
        name=f"in_proj_d{dil}",
    )(x2, gain.reshape(1, d), shift, scale, w_bf)


def _attn_kernel(q_ref, kp_ref, kc_ref, vp_ref, vc_ref, o_ref, *, dil, slopes, scale):
    n = pl.program_id(2)
    qi = lax.broadcasted_iota(jnp.int32, (ATT_BLOCK, ATT_BLOCK), 0)
    ki = lax.broadcasted_iota(jnp.int32, (ATT_BLOCK, ATT_BLOCK), 1)
    jc = qi - ki
    jp = jc + ATT_BLOCK
    prev_lim = jnp.where(n > 0, ATT_BLOCK, -1)
    valid_c = jc >= 0
    valid_p = jp <= prev_lim
    lane = lax.broadcasted_iota(jnp.int32, (ATT_BLOCK, HEAD_DIM), 1)
    lse_blk = jnp.zeros((ATT_BLOCK, HEAD_DIM), F32)
    for h in range(ATT_HPG):
        hs = slice(h * HEAD_DIM, (h + 1) * HEAD_DIM)
        q = q_ref[:, hs]
        bias = slopes[h] * dil
        s_c = _dot_nt(q, kc_ref[:, hs]) * scale - bias * jc.astype(F32)
        s_p = _dot_nt(q, kp_ref[:, hs]) * scale - bias * jp.astype(F32)
        s_c = jnp.where(valid_c, s_c, -jnp.inf)
        s_p = jnp.where(valid_p, s_p, -jnp.inf)
        m = jnp.maximum(jnp.max(s_c, axis=-1, keepdims=True), jnp.max(s_p, axis=-1, keepdims=True))
        p_c = jnp.exp(s_c - m)
        p_p = jnp.exp(s_p - m)
        l = jnp.sum(p_c, axis=-1, keepdims=True) + jnp.sum(p_p, axis=-1, keepdims=True)
        o = _dot(p_c.astype(BF16), vc_ref[:, hs]) + _dot(p_p.astype(BF16), vp_ref[:, hs])
        o_ref[:, hs] = o / l
        lse = m + jnp.log(l)
        lse_blk = jnp.where((lane >= h * LSE_LANES) & (lane < (h + 1) * LSE_LANES), lse, lse_blk)
    o_ref[:, ATT_GW:] = lse_blk


def _attention_group(qkv, g):
    nb_, dil, l, _ = qkv.shape
    window, dil_ = ATT_GROUPS[g]
    assert dil == dil_ and window // dil == ATT_BLOCK and l % ATT_BLOCK == 0
    nblk = l // ATT_BLOCK
    slopes = tuple(2.0 ** (-8.0 * (g * ATT_HPG + i + 1) / ATT_HEADS) for i in range(ATT_HPG))

    def spec(part, prev):
        if prev:
            return pl.BlockSpec((None, None, ATT_BLOCK, ATT_GW),
                                lambda b, r, n: (b, r, jnp.maximum(n - 1, 0), part))
        return pl.BlockSpec((None, None, ATT_BLOCK, ATT_GW), lambda b, r, n: (b, r, n, part))

    return pl.pallas_call(
        functools.partial(_attn_kernel, dil=float(dil), slopes=slopes, scale=HEAD_DIM ** -0.5),
        out_shape=jax.ShapeDtypeStruct((nb_, dil, l, ATT_OUT_COLS), F32),
        grid=(nb_, dil, nblk),
        in_specs=[spec(0, False), spec(1, True), spec(1, False), spec(2, True), spec(2, False)],
        out_specs=pl.BlockSpec((None, None, ATT_BLOCK, ATT_OUT_COLS), lambda b, r, n: (b, r, n, 0)),
        compiler_params=_params(("parallel", "parallel", "arbitrary")),
        name=f"attn_g{g}",
    )(qkv, qkv, qkv, qkv, qkv)


def _hgrn_kernel(q_ref, f_ref, i_ref, g_ref, lb_ref, gn_ref, tri_ref, o_ref, st_ref, sh_ref, *, tb):
    @pl.when(pl.program_id(2) == 0)
    def _():
        st_ref[...] = jnp.zeros_like(st_ref)
        sh_ref[:, 0:HGRN_DIRECT, :] = jnp.zeros((3, HGRN_DIRECT, HGRN_D), F32)

    c_ = HGRN_CHUNK
    nc = tb // c_
    q = q_ref[...].astype(F32)
    v = i_ref[...].astype(F32)
    lb = lb_ref[...]
    f = lb + (1.0 - lb) * jax.nn.sigmoid(f_ref[...].astype(F32))
    kk = 1.0 - f
    lf = jnp.log(f)

    hi = lf.astype(BF16)
    r1 = lf - hi.astype(F32)
    mid = r1.astype(BF16)
    lo = (r1 - mid.astype(F32)).astype(BF16)
    tri = tri_ref[...]
    b = _dot(tri, hi) + _dot(tri, mid) + _dot(tri, lo)

    def ref_rows(blk, row):
        b3 = b.reshape(tb // blk, blk, HGRN_D)
        return jnp.broadcast_to(b3[:, row:row + 1, :], (tb // blk, blk, HGRN_D)).reshape(tb, HGRN_D)

    ti = lax.broadcasted_iota(jnp.int32, (c_, c_), 0)
    si = lax.broadcasted_iota(jnp.int32, (c_, c_), 1)
    ssum = jnp.zeros((nc, c_, c_), F32)
    half = c_ // 2
    while half >= HGRN_DIRECT:
        bref = ref_rows(2 * half, half)
        ql = (q * jnp.exp(jnp.minimum(b - bref, 0.0))).astype(BF16).reshape(nc, c_, HGRN_D)
        kl = (kk * jnp.exp(jnp.minimum(bref - b, 0.0))).astype(BF16).reshape(nc, c_, HGRN_D)
        s_l = jnp.einsum("ctk,csk->cts", ql, kl, preferred_element_type=F32)
        mask = ((ti // (2 * half)) == (si // (2 * half))) & (((ti // half) % 2) == 1) & (((si // half) % 2) == 0)
        ssum = ssum + jnp.where(mask[None], s_l, 0.0)
        half //= 2
    v3 = v.astype(BF16).reshape(nc, c_, HGRN_D)
    o_acc = jnp.einsum("cts,csv->ctv", ssum.astype(BF16), v3, preferred_element_type=F32).reshape(tb, HGRN_D)

    sh_ref[0, HGRN_DIRECT:, :] = kk
    sh_ref[1, HGRN_DIRECT:, :] = b
    sh_ref[2, HGRN_DIRECT:, :] = v
    tmod = lax.broadcasted_iota(jnp.int32, (tb, HGRN_D), 0) % HGRN_DIRECT
    ones = jnp.ones((HGRN_D, HGRN_D), BF16)
    for lag in range(HGRN_DIRECT):
        st = HGRN_DIRECT - lag
        kd = sh_ref[0, st:st + tb, :]
        bd = sh_ref[1, st:st + tb, :]
        vd = sh_ref[2, st:st + tb, :]
        w = jnp.where(tmod >= lag, q * kd * jnp.exp(jnp.minimum(b - bd, 0.0)), 0.0)
        w_hi = w.astype(BF16)
        w_lo = (w - w_hi.astype(F32)).astype(BF16)
        o_acc = o_acc + (_dot(w_hi, ones) + _dot(w_lo, ones)) * vd

    blast = ref_rows(c_, c_ - 1)
    qe = (q * jnp.exp(b)).astype(BF16)
    kt = (kk * jnp.exp(blast - b)).astype(BF16)
    vb = v.astype(BF16)
    st_t = st_ref[...]
    inter = []
    for c in range(nc):
        rows = slice(c * c_, (c + 1) * c_)
        inter.append(_dot_nt(qe[rows], st_t.astype(BF16)))
        dec = jnp.exp(blast[c * c_:c * c_ + 1, :])
        st_t = st_t * dec + _dot_tn(vb[rows], kt[rows])
    st_ref[...] = st_t
    o = o_acc + jnp.concatenate(inter, axis=0)

    y = _rms(o) * gn_ref[...]
    o_ref[...] = (y * _silu(g_ref[...].astype(F32))).astype(o_ref.dtype)


def _hgrn(proj3, lb, gnorm):
    nb_, s, nc = proj3.shape
    tb = _pick(s, (512, 256, 128, 64))
    idx = jnp.arange(tb)
    tri = ((idx[:, None] // HGRN_CHUNK == idx[None, :] // HGRN_CHUNK) & (idx[None, :] <= idx[:, None])).astype(BF16)

    def spec(off):
        cb = off // HGRN_D
        return pl.BlockSpec((None, tb, HGRN_D), lambda b, h, n: (b, n, cb + h))

    vec = pl.BlockSpec((1, HGRN_D), lambda b, h, n: (0, h))
    out = pl.pallas_call(
        functools.partial(_hgrn_kernel, tb=tb),
        out_shape=jax.ShapeDtypeStruct((nb_, s, HGRN_WIDTH), BF16),
        grid=(nb_, HGRN_HEADS, s // tb),
        in_specs=[spec(RQ_OFF), spec(RF_OFF), spec(RI_OFF), spec(RG_OFF), vec, vec,
                  pl.BlockSpec((tb, tb), lambda b, h, n: (0, 0))],
        out_specs=pl.BlockSpec((None, tb, HGRN_D), lambda b, h, n: (b, n, h)),
        scratch_shapes=[pltpu.VMEM((HGRN_D, HGRN_D), F32),
                        pltpu.VMEM((3, tb + HGRN_DIRECT, HGRN_D), F32)],
        compiler_params=_params(("parallel", "parallel", "arbitrary"), 32),
        name="hgrn2",
    )(proj3, proj3, proj3, proj3, lb.reshape(1, HGRN_WIDTH), gnorm.reshape(1, HGRN_WIDTH), tri)
    return out.reshape(nb_ * s, HGRN_WIDTH)


def _branch_kernel(a0_ref, a1_ref, a2_ref, r_ref, ga_ref, gb_ref, wa_ref, wr_ref, o_ref, att_scr, nat_scr,
                   *, dils):
    @pl.when(pl.program_id(1) == 0)
    def _():
        tm = att_scr.shape[0]
        nslab = ATT_OUT_COLS // 128
        for gi, (a_ref, dil) in enumerate(zip((a1_ref, a2_ref), dils)):
            for r in range(dil):
                for c in range(nslab):
                    nat_scr[gi, c, pl.ds(r, tm // dil, stride=dil), :] = a_ref[r, :, c * 128:(c + 1) * 128]

        def head(g, h):
            return a0_ref[:, h * HEAD_DIM:(h + 1) * HEAD_DIM] if g == 0 else nat_scr[g - 1, h]

        def lse(g, h):
            if g == 0:
                return a0_ref[:, ATT_GW + h * LSE_LANES:ATT_GW + h * LSE_LANES + 1]
            return nat_scr[g - 1, ATT_HPG, :, h * LSE_LANES:h * LSE_LANES + 1]

        for h in range(ATT_HPG):
            ls = [lse(g, h) for g in range(3)]
            m = jnp.maximum(jnp.maximum(ls[0], ls[1]), ls[2])
            es = [jnp.exp(l - m) for l in ls]
            den = es[0] + es[1] + es[2]
            o = ((es[0] / den) * head(0, h) + (es[1] / den) * head(1, h)) + (es[2] / den) * head(2, h)
            att_scr[:, h * HEAD_DIM:(h + 1) * HEAD_DIM] = o.astype(BF16)

    a = _dot(att_scr[...], wa_ref[...])
    r = _dot(r_ref[...], wr_ref[...])
    o_ref[...] = (jax.nn.sigmoid(ga_ref[...].astype(F32)) * a
                  + jax.nn.sigmoid(gb_ref[...].astype(F32)) * r).astype(o_ref.dtype)


def _branch(att, orec, proj2, wa_bf, wr_bf, d, seq):
    t = orec.shape[0]
    tm = _pick(seq, (512, 256))
    per = seq // tm
    gw = _pick(d, (512, 256, 128))
    assert GATE_OFF % gw == 0
    ga0 = GATE_OFF // gw
    gb0 = (GATE_OFF + d) // gw
    dils = tuple(dil for _, dil in ATT_GROUPS[1:])

    def aspec(dil):
        return pl.BlockSpec((None, dil, tm // dil, ATT_OUT_COLS), lambda i, j: (i // per, 0, i % per, 0))

    return pl.pallas_call(
        functools.partial(_branch_kernel, dils=dils),
        out_shape=jax.ShapeDtypeStruct((t, d), BF16),
        grid=(t // tm, d // gw),
        in_specs=[pl.BlockSpec((tm, ATT_OUT_COLS), lambda i, j: (i, 0)), aspec(dils[0]), aspec(dils[1]),
                  pl.BlockSpec((tm, HGRN_WIDTH), lambda i, j: (i, 0)),
                  pl.BlockSpec((tm, gw), lambda i, j: (i, ga0 + j)),
                  pl.BlockSpec((tm, gw), lambda i, j: (i, gb0 + j)),
                  pl.BlockSpec((ATT_GW, gw), lambda i, j: (0, j)),
                  pl.BlockSpec((HGRN_WIDTH, gw), lambda i, j: (0, j))],
        out_specs=pl.BlockSpec((tm, gw), lambda i, j: (i, j)),
        scratch_shapes=[pltpu.VMEM((tm, ATT_GW), BF16),
                        pltpu.VMEM((len(dils), ATT_OUT_COLS // 128, tm, 128), F32)],
        compiler_params=_params(("parallel", "arbitrary"), 32),
        name="branch_merge",
    )(att[0].reshape(t, ATT_OUT_COLS), att[1], att[2], orec, proj2, proj2, wa_bf, wr_bf)


def _outproj_kernel(m_ref, w_ref, x_ref, gm_ref, pnm_ref, pnf_ref, sh_ref, sc_ref, x1_ref, hf_ref, hb_ref):
    y = _dot(m_ref[...], w_ref[...])
    x1 = x_ref[...] + gm_ref[...] * (_rms(y) * pnm_ref[...])
    x1_ref[...] = x1
    h2 = (_rms(x1) * pnf_ref[...]) * (1.0 + sc_ref[...]) + sh_ref[...]
    hf_ref[...] = h2
    hb_ref[...] = h2.astype(BF16)


def _outproj(merged, w_bf, x2, gate_m, pnm, pnf, shift_f, scale_f, seq):
    t, d = x2.shape
    tm = _pick(seq, (256, 128))
    per = seq // tm
    row = pl.BlockSpec((tm, d), lambda i: (i, 0))
    vec = pl.BlockSpec((1, d), lambda i: (0, 0))
    bvec = pl.BlockSpec((None, 1, d), lambda i: (i // per, 0, 0))
    return pl.pallas_call(
        _outproj_kernel,
        out_shape=(jax.ShapeDtypeStruct((t, d), F32), jax.ShapeDtypeStruct((t, d), F32),
                   jax.ShapeDtypeStruct((t, d), BF16)),
        grid=(t // tm,),
        in_specs=[row, pl.BlockSpec((d, d), lambda i: (0, 0)), row, bvec, vec, vec, bvec, bvec],
        out_specs=(row, row, row),
        compiler_params=_params(("parallel",), 48),
        name="out_proj",
    )(merged, w_bf, x2, gate_m, pnm.reshape(1, d), pnf.reshape(1, d), shift_f, scale_f)


def _router_kernel(h_ref, w_ref, bias_ref, up_ref, eid_ref, gate_ref, pos_ref, cnt_ref, carry_ref, *, tr):
    @pl.when(pl.program_id(0) == 0)
    def _():
        carry_ref[...] = jnp.zeros_like(carry_ref)

    per_group = N_EXPERTS // N_GROUPS
    sig = jax.nn.sigmoid(_dot_nt(w_ref[...], h_ref[...]))
    choice = sig + bias_ref[...]
    eidx = lax.broadcasted_iota(jnp.int32, (N_EXPERTS, tr), 0)

    c3 = choice.reshape(N_GROUPS, per_group, tr)
    sub = lax.broadcasted_iota(jnp.int32, (N_GROUPS, per_group, tr), 1)
    m1 = jnp.max(c3, axis=1, keepdims=True)
    first = jnp.min(jnp.where(c3 == m1, sub, per_group), axis=1, keepdims=True)
    m2 = jnp.max(jnp.where(sub == first, -jnp.inf, c3), axis=1, keepdims=True)
    gs = (m1 + m2).reshape(N_GROUPS, tr)

    gidx = lax.broadcasted_iota(jnp.int32, (N_GROUPS, tr), 0)
    grank = jnp.zeros((N_GROUPS, tr), jnp.int32)
    for g in range(N_GROUPS):
        row = gs[g:g + 1, :]
        grank = grank + ((row > gs) | ((row == gs) & (gidx > g))).astype(jnp.int32)
    gsel = jnp.where(grank < TOPK_GROUPS, 1.0, 0.0)
    emask = jnp.broadcast_to(gsel.reshape(N_GROUPS, 1, tr), (N_GROUPS, per_group, tr)).reshape(N_EXPERTS, tr)
    cm = jnp.where(emask > 0.5, choice, -jnp.inf)

    rank = jnp.zeros((N_EXPERTS, tr), jnp.int32)
    for e in range(N_EXPERTS):
        row = cm[e:e + 1, :]
        rank = rank + ((row > cm) | ((row == cm) & (eidx > e))).astype(jnp.int32)
    sel = rank < TOP_K

    denom = jnp.sum(jnp.where(sel, sig, 0.0), axis=0, keepdims=True)
    gate_full = sig / denom * ROUTED_SCALE

    sel_b = jnp.where(sel, 1.0, 0.0).astype(BF16)
    carry = carry_ref[...]
    cum = _dot(sel_b, up_ref[...]) + jnp.concatenate([carry] * (tr // 128), axis=1)
    carry_new = carry + _dot(sel_b, jnp.ones((tr, 128), BF16))
    carry_ref[...] = carry_new
    cnt_ref[...] = carry_new.astype(jnp.int32)
    posi = cum.astype(jnp.int32)

    eids, gates, poss = [], [], []
    for r in range(TOP_K):
        hit = rank == r
        eids.append(jnp.sum(jnp.where(hit, eidx, 0), axis=0, keepdims=True))
        gates.append(jnp.sum(jnp.where(hit, gate_full, 0.0), axis=0, keepdims=True))
        poss.append(jnp.sum(jnp.where(hit, posi, 0), axis=0, keepdims=True))
    eid_ref[...] = jnp.concatenate(eids, axis=0)
    gate_ref[...] = jnp.concatenate(gates, axis=0)
    pos_ref[...] = jnp.concatenate(poss, axis=0)


def _router(h2b, w_router, router_bias):
    t, d = h2b.shape
    tr = _pick(t, (256, 128))
    idx = jnp.arange(tr)
    upper = (idx[:, None] < idx[None, :]).astype(BF16)
    kout = pl.BlockSpec((TOP_K, tr), lambda i: (0, i))
    return pl.pallas_call(
        functools.partial(_router_kernel, tr=tr),
        out_shape=(jax.ShapeDtypeStruct((TOP_K, t), jnp.int32), jax.ShapeDtypeStruct((TOP_K, t), F32),
                   jax.ShapeDtypeStruct((TOP_K, t), jnp.int32), jax.ShapeDtypeStruct((N_EXPERTS, 128), jnp.int32)),
        grid=(t // tr,),
        in_specs=[pl.BlockSpec((tr, d), lambda i: (i, 0)),
                  pl.BlockSpec((N_EXPERTS, d), lambda i: (0, 0)),
                  pl.BlockSpec((N_EXPERTS, 1), lambda i: (0, 0)),
                  pl.BlockSpec((tr, tr), lambda i: (0, 0))],
        out_specs=(kout, kout, kout, pl.BlockSpec((N_EXPERTS, 128), lambda i: (0, 0))),
        scratch_shapes=[pltpu.VMEM((N_EXPERTS, 128), F32)],
        compiler_params=_params(("arbitrary",), 32),
        name="router",
    )(h2b, w_router.T.astype(BF16), router_bias.reshape(N_EXPERTS, 1).astype(F32), upper)


def _row_copy(src_hbm, idx, dst, sem):
    return pltpu.make_async_copy(src_hbm.at[pl.ds(idx, 1), :], dst, sem)


DMA_UNROLL = 8


def _dispatch_kernel(zs_ref, zn_ref, d_ref, h_ref, xs_hbm, zero_scr, sem, zsem, *, tm, n_blocks):
    i = pl.program_id(0)

    def body(it, carry):
        for u in range(DMA_UNROLL):
            r = it * DMA_UNROLL + u
            for k in range(TOP_K):
                pltpu.make_async_copy(h_ref.at[pl.ds(r, 1), :], xs_hbm.at[pl.ds(d_ref[0, k * tm + r], 1), :],
                                      sem).start(priority=(u * TOP_K + k) % 2)
        return carry
    lax.fori_loop(0, tm // DMA_UNROLL, body, 0)

    @pl.when(i == 0)
    def _():
        zero_scr[...] = jnp.zeros_like(zero_scr)
        for e in range(N_EXPERTS):
            def zbody(u, carry, e=e):
                pltpu.make_async_copy(zero_scr.at[pl.ds(0, 1), :], xs_hbm.at[pl.ds(zs_ref[e] + u, 1), :], zsem).start()
                return carry
            lax.fori_loop(0, zn_ref[e], zbody, 0)
        for e in range(N_EXPERTS):
            def zwait(u, carry):
                pltpu.make_async_copy(zero_scr.at[pl.ds(0, 1), :], xs_hbm.at[pl.ds(0, 1), :], zsem).wait()
                return carry
            lax.fori_loop(0, zn_ref[e], zwait, 0)

        def tbody(blk, carry):
            pltpu.make_async_copy(zero_scr, xs_hbm.at[pl.ds(blk * MOE_BLOCK, MOE_BLOCK), :], zsem).start()
            return carry
        lax.fori_loop(zn_ref[N_EXPERTS], n_blocks, tbody, 0)

        def twait(blk, carry):
            pltpu.make_async_copy(zero_scr, xs_hbm.at[pl.ds(0, MOE_BLOCK), :], zsem).wait()
            return carry
        lax.fori_loop(zn_ref[N_EXPERTS], n_blocks, twait, 0)

    for k in range(TOP_K):
        pltpu.make_async_copy(h_ref, xs_hbm.at[pl.ds(0, tm), :], sem).wait()


def _dispatch(h2f, dest3, zstart, znum, n_slots, tm):
    t, d = h2f.shape
    grid_spec = pltpu.PrefetchScalarGridSpec(
        num_scalar_prefetch=2,
        grid=(t // tm,),
        in_specs=[pl.BlockSpec((None, 1, TOP_K * tm), lambda i, zs, zn: (i, 0, 0), memory_space=pltpu.SMEM),
                  pl.BlockSpec((tm, d), lambda i, zs, zn: (i, 0))],
        out_specs=pl.BlockSpec(memory_space=pl.ANY),
        scratch_shapes=[pltpu.VMEM((MOE_BLOCK, d), F32), pltpu.SemaphoreType.DMA(()), pltpu.SemaphoreType.DMA(())],
    )
    return pl.pallas_call(
        functools.partial(_dispatch_kernel, tm=tm, n_blocks=n_slots // MOE_BLOCK),
        out_shape=jax.ShapeDtypeStruct((n_slots, d), F32),
        grid_spec=grid_spec,
        compiler_params=_params(("arbitrary",), 32),
        name="dispatch",
    )(zstart, znum, dest3, h2f)


def _expert_kernel(be_ref, nu_ref, x_ref, wg_ref, wu_ref, wd_ref, o_ref):
    i = pl.program_id(0)

    @pl.when(i < nu_ref[0])
    def _():
        xb = x_ref[...].astype(BF16)
        hid = _silu(_dot(xb, wg_ref[...])) * _dot(xb, wu_ref[...])
        o_ref[...] = _dot(hid.astype(BF16), wd_ref[...])

    @pl.when(i >= nu_ref[0])
    def _():
        o_ref[...] = jnp.zeros_like(o_ref)


def _experts(x_sorted, block_expert, n_used, wg, wu, wd):
    n_slots, d = x_sorted.shape
    n_blocks = block_expert.shape[0]
    de = wg.shape[-1]
    grid_spec = pltpu.PrefetchScalarGridSpec(
        num_scalar_prefetch=2,
        grid=(n_blocks,),
        in_specs=[pl.BlockSpec((MOE_BLOCK, d), lambda i, be, nu: (jnp.minimum(i, nu[0] - 1), 0)),
                  pl.BlockSpec((None, d, de), lambda i, be, nu: (be[i], 0, 0)),
                  pl.BlockSpec((None, d, de), lambda i, be, nu: (be[i], 0, 0)),
                  pl.BlockSpec((None, de, d), lambda i, be, nu: (be[i], 0, 0))],
        out_specs=pl.BlockSpec((MOE_BLOCK, d), lambda i, be, nu: (i, 0)),
    )
    return pl.pallas_call(
        _expert_kernel,
        out_shape=jax.ShapeDtypeStruct((n_slots, d), F32),
        grid_spec=grid_spec,
        compiler_params=_params(("arbitrary",), 48),
        name="experts",
    )(block_expert, n_used, x_sorted, wg, wu, wd)


def _combine_kernel(dc_ref, dn_ref, y_hbm, gt_ref, h_ref, x1_ref, wg_ref, wu_ref, wd_ref, gf_ref, pn_ref,
                    o_ref, buf, sem, *, tm):
    i = pl.program_id(0)
    nsteps = pl.num_programs(0)
    slot = i % 2

    def issue(d_ref, s):
        def body(it, carry):
            for u in range(DMA_UNROLL):
                r = it * DMA_UNROLL + u
                for k in range(TOP_K):
                    _row_copy(y_hbm, d_ref[0, k * tm + r], buf.at[s, k, pl.ds(r, 1), :],
                              sem.at[s]).start(priority=(u * TOP_K + k) % 2)
            return carry
        lax.fori_loop(0, tm // DMA_UNROLL, body, 0)

    @pl.when(i == 0)
    def _():
        issue(dc_ref, 0)

    @pl.when(i + 1 < nsteps)
    def _():
        issue(dn_ref, 1 - slot)

    hb = h_ref[...]
    shared = _dot((_silu(_dot(hb, wg_ref[...])) * _dot(hb, wu_ref[...])).astype(BF16), wd_ref[...])

    for k in range(TOP_K):
        pltpu.make_async_copy(y_hbm.at[pl.ds(0, tm), :], buf.at[slot, k], sem.at[slot]).wait()

    gt = gt_ref[...]
    acc = shared
    for k in range(TOP_K):
        acc = acc + gt[:, k:k + 1] * buf[slot, k]
    o_ref[...] = x1_ref[...] + gf_ref[...] * (_rms(acc) * pn_ref[...])


def _combine(y_sorted, dest3, gate_t, h2b, x1, wgs, wus, wds, gate_f, pnf, seq, tm):
    t, d = x1.shape
    ds_ = wgs.shape[-1]
    per = seq // tm
    nt = t // tm
    smem = functools.partial(pl.BlockSpec, memory_space=pltpu.SMEM)
    row = pl.BlockSpec((tm, d), lambda i: (i, 0))
    return pl.pallas_call(
        functools.partial(_combine_kernel, tm=tm),
        out_shape=jax.ShapeDtypeStruct((t, d), F32),
        grid=(nt,),
        in_specs=[smem((None, 1, TOP_K * tm), lambda i: (i, 0, 0)),
                  smem((None, 1, TOP_K * tm), lambda i: (jnp.minimum(i + 1, nt - 1), 0, 0)),
                  pl.BlockSpec(memory_space=pl.ANY),
                  pl.BlockSpec((tm, TOP_K), lambda i: (i, 0)),
                  row, row,
                  pl.BlockSpec((d, ds_), lambda i: (0, 0)),
                  pl.BlockSpec((d, ds_), lambda i: (0, 0)),
                  pl.BlockSpec((ds_, d), lambda i: (0, 0)),
                  pl.BlockSpec((None, 1, d), lambda i: (i // per, 0, 0)),
                  pl.BlockSpec((1, d), lambda i: (0, 0))],
        out_specs=row,
        scratch_shapes=[pltpu.VMEM((2, TOP_K, tm, d), F32), pltpu.SemaphoreType.DMA((2,))],
        compiler_params=_params(("arbitrary",), 48),
        name="combine",
    )(dest3, dest3, y_sorted, gate_t, h2b, x1, wgs, wus, wds, gate_f, pnf.reshape(1, d))


def _layer(x, c, lb, w_ada, b_ada, pre_norm_mix, post_norm_mix, w_in, hgrn_norm, w_branch_attn, w_branch_hgrn,
           w_out, pre_norm_ffn, post_norm_ffn, w_router, router_bias, w_gate_e, w_up_e, w_down_e,
           w_gate_s, w_up_s, w_down_s):
    nb, s, d = x.shape
    t = nb * s
    x2 = x.reshape(t, d)

    mod = _ada(c, w_ada, b_ada).reshape(nb, 6, 1, d)
    shift_m, scale_m, gate_m, shift_f, scale_f, gate_f = (mod[:, k] for k in range(6))

    w_in_bf = w_in.astype(BF16)
    qkv = []
    for g, (_, dil) in enumerate(ATT_GROUPS):
        cols = jnp.concatenate([w_in_bf[:, p * ATT_WIDTH + g * ATT_GW:p * ATT_WIDTH + (g + 1) * ATT_GW]
                                for p in range(3)], axis=1)
        o = _inproj(x2, pre_norm_mix, shift_m, scale_m, cols, nb, s, dil)
        qkv.append(o.reshape(nb, dil, s // dil, 3 * ATT_GW))
    proj2 = _inproj(x2, pre_norm_mix, shift_m, scale_m, w_in_bf[:, QKV_COLS:], nb, s)
    proj3 = proj2.reshape(nb, s, -1)

    att = [_attention_group(qkv[g], g) for g in range(len(ATT_GROUPS))]
    orec = _hgrn(proj3, lb, hgrn_norm)
    merged = _branch(att, orec, proj2, w_branch_attn.astype(BF16), w_branch_hgrn.astype(BF16), d, s)
    x1, h2f, h2b = _outproj(merged, w_out.astype(BF16), x2, gate_m, post_norm_mix, pre_norm_ffn,
                            shift_f, scale_f, s)

    eid, gate, pos, cnt = _router(h2b, w_router, router_bias)

    counts = cnt[:, 0]
    padded = (counts + MOE_BLOCK - 1) // MOE_BLOCK * MOE_BLOCK
    pad_end = jnp.cumsum(padded)
    pad_start = pad_end - padded
    n_blocks = -(-(t * TOP_K) // MOE_BLOCK) + N_EXPERTS
    onehot = eid[None] == jnp.arange(N_EXPERTS, dtype=jnp.int32)[:, None, None]
    dest = jnp.sum(jnp.where(onehot, pad_start[:, None, None], 0), axis=0) + pos
    block_start = jnp.arange(n_blocks, dtype=jnp.int32) * MOE_BLOCK
    block_expert = jnp.minimum(jnp.sum((block_start[:, None] >= pad_end[None, :]).astype(jnp.int32), axis=1),
                               N_EXPERTS - 1)
    n_used = (pad_end[-1:] // MOE_BLOCK).astype(jnp.int32)

    tm = _pick(s, (128,))
    nt = t // tm
    dest3 = dest.reshape(TOP_K, nt, tm).transpose(1, 0, 2).reshape(nt, 1, TOP_K * tm)
    znum = jnp.concatenate([padded - counts, n_used]).astype(jnp.int32)
    x_sorted = _dispatch(h2f, dest3, (pad_start + counts).astype(jnp.int32), znum, n_blocks * MOE_BLOCK, tm)
    y_sorted = _experts(x_sorted, block_expert, n_used,
                        w_gate_e.astype(BF16), w_up_e.astype(BF16), w_down_e.astype(BF16))
    out = _combine(y_sorted, dest3, gate.T, h2b, x1, w_gate_s.astype(BF16), w_up_s.astype(BF16),
                   w_down_s.astype(BF16), gate_f, post_norm_ffn, s, tm)
    return out.reshape(nb, s, d)


def kernel(x, c, w_ada, b_ada, pre_norm_mix, post_norm_mix, w_in, hgrn_lb_logits, hgrn_norm, w_branch_attn,
           w_branch_hgrn, w_out, pre_norm_ffn, post_norm_ffn, w_router, router_bias, w_gate_e, w_up_e, w_down_e,
           w_gate_s, w_up_s, w_down_s):
    lb_table = jnp.cumsum(jax.nn.softmax(hgrn_lb_logits.astype(F32), axis=0), axis=0)
    depth = w_ada.shape[0]
    for l in range(depth):
        x = _layer(x, c, lb_table[l], w_ada[l], b_ada[l], pre_norm_mix[l], post_norm_mix[l], w_in[l],
                   hgrn_norm[l], w_branch_attn[l], w_branch_hgrn[l], w_out[l], pre_norm_ffn[l],
                   post_norm_ffn[l], w_router[l], router_bias[l], w_gate_e[l], w_up_e[l], w_down_e[l],
                   w_gate_s[l], w_up_s[l], w_down_s[l])
    return x
```

```python
import functools

import jax
import jax.numpy as jnp
from jax import lax
from jax.experimental import pallas as pl
from jax.experimental.pallas import tpu as pltpu

F32 = jnp.float32
BF16 = jnp.bfloat16

HEAD_DIM = 128
ATT_GROUPS = ((128, 1), (512, 4), (2048, 16))
ATT_HPG = 4
ATT_HEADS = ATT_HPG * len(ATT_GROUPS)
ATT_WIDTH = ATT_HEADS * HEAD_DIM
ATT_GW = ATT_HPG * HEAD_DIM
ATT_BLOCK = 128
ATT_OUT_COLS = ATT_GW + HEAD_DIM
LSE_LANES = HEAD_DIM // ATT_HPG
HGRN_HEADS = 8
HGRN_D = 128
HGRN_WIDTH = HGRN_HEADS * HGRN_D
HGRN_CHUNK = 64
HGRN_DIRECT = 8
N_EXPERTS = 64
N_GROUPS = 8
TOPK_GROUPS = 4
TOP_K = 8
ROUTED_SCALE = 2.5
MOE_BLOCK = 256
NORM_EPS = 1e-6
V7X_VMEM_LIMIT = 56 * 1024 * 1024

QKV_COLS = 3 * ATT_WIDTH
RQ_OFF = 0
RF_OFF = RQ_OFF + HGRN_WIDTH
RI_OFF = RF_OFF + HGRN_WIDTH
RG_OFF = RI_OFF + HGRN_WIDTH
GATE_OFF = RG_OFF + HGRN_WIDTH


def _pick(n, cands):
    for c in cands:
        if n % c == 0:
            return c
    raise ValueError(f"no tile of {cands} divides {n}")


def _params(sem, vmem_mib=None):
    kw = dict(dimension_semantics=sem)
    if vmem_mib is not None:
        kw["vmem_limit_bytes"] = min(vmem_mib * 1024 * 1024, V7X_VMEM_LIMIT)
    return pltpu.CompilerParams(**kw)


def _dot(a, b):
    return jnp.dot(a, b, preferred_element_type=F32)


def _dot_nt(a, b):
    return lax.dot_general(a, b, (((1,), (1,)), ((), ())), preferred_element_type=F32)


def _dot_tn(a, b):
    return lax.dot_general(a, b, (((0,), (0,)), ((), ())), preferred_element_type=F32)


def _rms(y):
    return y * lax.rsqrt(jnp.mean(y * y, axis=-1, keepdims=True) + NORM_EPS)


def _silu(a):
    return a * jax.nn.sigmoid(a)


ADA_KCHUNK = 128


def _ada_kernel(ct_ref, w_ref, b_ref, o_ref, *, nb, d):
    a = _silu(ct_ref[...])
    tn = w_ref.shape[1]
    accs = [jnp.zeros((8, tn), F32) for _ in range(nb)]
    for kc in range(d // ADA_KCHUNK):
        wc = w_ref[kc * ADA_KCHUNK:(kc + 1) * ADA_KCHUNK, :]
        ac = a[kc * ADA_KCHUNK:(kc + 1) * ADA_KCHUNK, :]
        for b in range(nb):
            p = ac[:, b:b + 1] * wc
            accs[b] = accs[b] + p.reshape(ADA_KCHUNK // 8, 8, tn).sum(axis=0)
    rows = [jnp.sum(acc, axis=0, keepdims=True) for acc in accs]
    o_ref[...] = jnp.concatenate(rows, axis=0) + b_ref[...]


def _ada(c, w_ada, b_ada):
    nb, d = c.shape
    n = w_ada.shape[1]
    tn = _pick(n, (512, 256, 128))
    return pl.pallas_call(
        functools.partial(_ada_kernel, nb=nb, d=d),
        out_shape=jax.ShapeDtypeStruct((nb, n), F32),
        grid=(n // tn,),
        in_specs=[pl.BlockSpec((d, nb), lambda j: (0, 0)),
                  pl.BlockSpec((d, tn), lambda j: (0, j)),
                  pl.BlockSpec((1, tn), lambda j: (0, j))],
        out_specs=pl.BlockSpec((nb, tn), lambda j: (0, j)),
        compiler_params=_params(("arbitrary",), 32),
        name="ada_mod",
    )(c.T, w_ada, b_ada.reshape(1, n))


def _inproj_kernel(x_ref, g_ref, sh_ref, sc_ref, w_ref, o_ref, h_scr, *slab, dil):
    @pl.when(pl.program_id(1) == 0)
    def _():
        y = _rms(x_ref[...]) * g_ref[...]
        h_scr[...] = (y * (1.0 + sc_ref[...]) + sh_ref[...]).astype(BF16)

    o = _dot(h_scr[...], w_ref[...])
    if dil == 1:
        o_ref[...] = o.astype(o_ref.dtype)
    else:
        slab_ref, = slab
        tm, tn = o.shape
        for c in range(tn // 128):
            slab_ref[c] = o[:, c * 128:(c + 1) * 128]
        for r in range(dil):
            for c in range(tn // 128):
                o_ref[r, :, c * 128:(c + 1) * 128] = slab_ref[c, pl.ds(r, tm // dil, stride=dil), :].astype(o_ref.dtype)


def _inproj(x2, gain, shift, scale, w_bf, nb, seq, dil=1, tn=None):
    t, d = x2.shape
    n = w_bf.shape[1]
    tm = _pick(seq, (512, 256))
    tn = tn or _pick(n, (1024, 512, 256, 128))
    per = seq // tm
    if dil == 1:
        out_shape = jax.ShapeDtypeStruct((t, n), BF16)
        out_spec = pl.BlockSpec((tm, tn), lambda i, j: (i, j))
        scratch = [pltpu.VMEM((tm, d), BF16)]
    else:
        out_shape = jax.ShapeDtypeStruct((nb, dil, seq // dil, n), BF16)
        out_spec = pl.BlockSpec((None, dil, tm // dil, tn), lambda i, j: (i // per, 0, i % per, j))
        scratch = [pltpu.VMEM((tm, d), BF16), pltpu.VMEM((tn // 128, tm, 128), F32)]
    return pl.pallas_call(
        functools.partial(_inproj_kernel, dil=dil),
        out_shape=out_shape,
        grid=(t // tm, n // tn),
        in_specs=[pl.BlockSpec((tm, d), lambda i, j: (i, 0)),
                  pl.BlockSpec((1, d), lambda i, j: (0, 0)),
                  pl.BlockSpec((None, 1, d), lambda i, j: (i // per, 0, 0)),
                  pl.BlockSpec((None, 1, d), lambda i, j: (i // per, 0, 0)),
                  pl.BlockSpec((d, tn), lambda i, j: (0, j))],
        out_specs=out_spec,
        scratch_shapes=scratch,
        compiler_params=_params(("parallel", "arbitrary"), 48),
        name=f"in_proj_d{dil}_n{n}",
    )(x2, gain.reshape(1, d), shift, scale, w_bf)


def _attn_kernel(q_ref, kp_ref, kc_ref, vp_ref, vc_ref, o_ref, *, dil, slopes, scale):
    n = pl.program_id(2)
    qi = lax.broadcasted_iota(jnp.int32, (ATT_BLOCK, ATT_BLOCK), 0)
    ki = lax.broadcasted_iota(jnp.int32, (ATT_BLOCK, ATT_BLOCK), 1)
    jc = qi - ki
    jp = jc + ATT_BLOCK
    prev_lim = jnp.where(n > 0, ATT_BLOCK, -1)
    valid_c = jc >= 0
    valid_p = jp <= prev_lim
    lane = lax.broadcasted_iota(jnp.int32, (ATT_BLOCK, HEAD_DIM), 1)
    lse_blk = jnp.zeros((ATT_BLOCK, HEAD_DIM), F32)
    for h in range(ATT_HPG):
        hs = slice(h * HEAD_DIM, (h + 1) * HEAD_DIM)
        q = q_ref[:, hs]
        bias = slopes[h] * dil
        s_c = _dot_nt(q, kc_ref[:, hs]) * scale - bias * jc.astype(F32)
        s_p = _dot_nt(q, kp_ref[:, hs]) * scale - bias * jp.astype(F32)
        s_c = jnp.where(valid_c, s_c, -jnp.inf)
        s_p = jnp.where(valid_p, s_p, -jnp.inf)
        m = jnp.maximum(jnp.max(s_c, axis=-1, keepdims=True), jnp.max(s_p, axis=-1, keepdims=True))
        p_c = jnp.exp(s_c - m)
        p_p = jnp.exp(s_p - m)
        l = jnp.sum(p_c, axis=-1, keepdims=True) + jnp.sum(p_p, axis=-1, keepdims=True)
        o = _dot(p_c.astype(BF16), vc_ref[:, hs]) + _dot(p_p.astype(BF16), vp_ref[:, hs])
        o_ref[:, hs] = o / l
        lse = m + jnp.log(l)
        lse_blk = jnp.where((lane >= h * LSE_LANES) & (lane < (h + 1) * LSE_LANES), lse, lse_blk)
    o_ref[:, ATT_GW:] = lse_blk


def _attention_group(qkv, g):
    nb_, dil, l, _ = qkv.shape
    window, dil_ = ATT_GROUPS[g]
    assert dil == dil_ and window // dil == ATT_BLOCK and l % ATT_BLOCK == 0
    nblk = l // ATT_BLOCK
    slopes = tuple(2.0 ** (-8.0 * (g * ATT_HPG + i + 1) / ATT_HEADS) for i in range(ATT_HPG))

    def spec(part, prev):
        if prev:
            return pl.BlockSpec((None, None, ATT_BLOCK, ATT_GW),
                                lambda b, r, n: (b, r, jnp.maximum(n - 1, 0), part))
        return pl.BlockSpec((None, None, ATT_BLOCK, ATT_GW), lambda b, r, n: (b, r, n, part))

    return pl.pallas_call(
        functools.partial(_attn_kernel, dil=float(dil), slopes=slopes, scale=HEAD_DIM ** -0.5),
        out_shape=jax.ShapeDtypeStruct((nb_, dil, l, ATT_OUT_COLS), F32),
        grid=(nb_, dil, nblk),
        in_specs=[spec(0, False), spec(1, True), spec(1, False), spec(2, True), spec(2, False)],
        out_specs=pl.BlockSpec((None, None, ATT_BLOCK, ATT_OUT_COLS), lambda b, r, n: (b, r, n, 0)),
        compiler_params=_params(("parallel", "parallel", "arbitrary")),
        name=f"attn_g{g}",
    )(qkv, qkv, qkv, qkv, qkv)


def _hgrn_kernel(q_ref, f_ref, i_ref, g_ref, lb_ref, gn_ref, tri_ref, o_ref, st_ref, sh_ref, *, tb):
    @pl.when(pl.program_id(2) == 0)
    def _():
        st_ref[...] = jnp.zeros_like(st_ref)
        sh_ref[:, 0:HGRN_DIRECT, :] = jnp.zeros((3, HGRN_DIRECT, HGRN_D), F32)

    c_ = HGRN_CHUNK
    nc = tb // c_
    q = q_ref[...].astype(F32)
    v = i_ref[...].astype(F32)
    lb = lb_ref[...]
    f = lb + (1.0 - lb) * jax.nn.sigmoid(f_ref[...].astype(F32))
    kk = 1.0 - f
    lf = jnp.log(f)

    hi = lf.astype(BF16)
    r1 = lf - hi.astype(F32)
    mid = r1.astype(BF16)
    lo = (r1 - mid.astype(F32)).astype(BF16)
    tri = tri_ref[...]
    b = _dot(tri, hi) + _dot(tri, mid) + _dot(tri, lo)

    def ref_rows(blk, row):
        b3 = b.reshape(tb // blk, blk, HGRN_D)
        return jnp.broadcast_to(b3[:, row:row + 1, :], (tb // blk, blk, HGRN_D)).reshape(tb, HGRN_D)

    ti = lax.broadcasted_iota(jnp.int32, (c_, c_), 0)
    si = lax.broadcasted_iota(jnp.int32, (c_, c_), 1)
    ssum = jnp.zeros((nc, c_, c_), F32)
    half = c_ // 2
    while half >= HGRN_DIRECT:
        bref = ref_rows(2 * half, half)
        ql = (q * jnp.exp(jnp.minimum(b - bref, 0.0))).astype(BF16).reshape(nc, c_, HGRN_D)
        kl = (kk * jnp.exp(jnp.minimum(bref - b, 0.0))).astype(BF16).reshape(nc, c_, HGRN_D)
        s_l = jnp.einsum("ctk,csk->cts", ql, kl, preferred_element_type=F32)
        mask = ((ti // (2 * half)) == (si // (2 * half))) & (((ti // half) % 2) == 1) & (((si // half) % 2) == 0)
        ssum = ssum + jnp.where(mask[None], s_l, 0.0)
        half //= 2
    v3 = v.astype(BF16).reshape(nc, c_, HGRN_D)
    o_acc = jnp.einsum("cts,csv->ctv", ssum.astype(BF16), v3, preferred_element_type=F32).reshape(tb, HGRN_D)

    sh_ref[0, HGRN_DIRECT:, :] = kk
    sh_ref[1, HGRN_DIRECT:, :] = b
    sh_ref[2, HGRN_DIRECT:, :] = v
    tmod = lax.broadcasted_iota(jnp.int32, (tb, HGRN_D), 0) % HGRN_DIRECT
    ones = jnp.ones((HGRN_D, HGRN_D), BF16)
    for lag in range(HGRN_DIRECT):
        st = HGRN_DIRECT - lag
        kd = sh_ref[0, st:st + tb, :]
        bd = sh_ref[1, st:st + tb, :]
        vd = sh_ref[2, st:st + tb, :]
        w = jnp.where(tmod >= lag, q * kd * jnp.exp(jnp.minimum(b - bd, 0.0)), 0.0)
        w_hi = w.astype(BF16)
        w_lo = (w - w_hi.astype(F32)).astype(BF16)
        o_acc = o_acc + (_dot(w_hi, ones) + _dot(w_lo, ones)) * vd

    blast = ref_rows(c_, c_ - 1)
    qe = (q * jnp.exp(b)).astype(BF16)
    kt = (kk * jnp.exp(blast - b)).astype(BF16)
    vb = v.astype(BF16)
    st_t = st_ref[...]
    inter = []
    for c in range(nc):
        rows = slice(c * c_, (c + 1) * c_)
        inter.append(_dot_nt(qe[rows], st_t.astype(BF16)))
        dec = jnp.exp(blast[c * c_:c * c_ + 1, :])
        st_t = st_t * dec + _dot_tn(vb[rows], kt[rows])
    st_ref[...] = st_t
    o = o_acc + jnp.concatenate(inter, axis=0)

    y = _rms(o) * gn_ref[...]
    o_ref[...] = (y * _silu(g_ref[...].astype(F32))).astype(o_ref.dtype)


def _hgrn(proj3, lb, gnorm):
    nb_, s, nc = proj3.shape
    tb = _pick(s, (512, 256, 128, 64))
    idx = jnp.arange(tb)
    tri = ((idx[:, None] // HGRN_CHUNK == idx[None, :] // HGRN_CHUNK) & (idx[None, :] <= idx[:, None])).astype(BF16)

    def spec(off):
        cb = off // HGRN_D
        return pl.BlockSpec((None, tb, HGRN_D), lambda b, h, n: (b, n, cb + h))

    vec = pl.BlockSpec((1, HGRN_D), lambda b, h, n: (0, h))
    out = pl.pallas_call(
        functools.partial(_hgrn_kernel, tb=tb),
        out_shape=jax.ShapeDtypeStruct((nb_, s, HGRN_WIDTH), BF16),
        grid=(nb_, HGRN_HEADS, s // tb),
        in_specs=[spec(RQ_OFF), spec(RF_OFF), spec(RI_OFF), spec(RG_OFF), vec, vec,
                  pl.BlockSpec((tb, tb), lambda b, h, n: (0, 0))],
        out_specs=pl.BlockSpec((None, tb, HGRN_D), lambda b, h, n: (b, n, h)),
        scratch_shapes=[pltpu.VMEM((HGRN_D, HGRN_D), F32),
                        pltpu.VMEM((3, tb + HGRN_DIRECT, HGRN_D), F32)],
        compiler_params=_params(("parallel", "parallel", "arbitrary"), 32),
        name="hgrn2",
    )(proj3, proj3, proj3, proj3, lb.reshape(1, HGRN_WIDTH), gnorm.reshape(1, HGRN_WIDTH), tri)
    return out.reshape(nb_ * s, HGRN_WIDTH)


def _branch_kernel(a0_ref, a1_ref, a2_ref, r_ref, ga_ref, gb_ref, wa_ref, wr_ref, o_ref, att_scr, nat_scr,
                   *, dils):
    @pl.when(pl.program_id(1) == 0)
    def _():
        tm = att_scr.shape[0]
        nslab = ATT_OUT_COLS // 128
        for gi, (a_ref, dil) in enumerate(zip((a1_ref, a2_ref), dils)):
            for r in range(dil):
                for c in range(nslab):
                    nat_scr[gi, c, pl.ds(r, tm // dil, stride=dil), :] = a_ref[r, :, c * 128:(c + 1) * 128]

        def head(g, h):
            return a0_ref[:, h * HEAD_DIM:(h + 1) * HEAD_DIM] if g == 0 else nat_scr[g - 1, h]

        def lse(g, h):
            if g == 0:
                return a0_ref[:, ATT_GW + h * LSE_LANES:ATT_GW + h * LSE_LANES + 1]
            return nat_scr[g - 1, ATT_HPG, :, h * LSE_LANES:h * LSE_LANES + 1]

        for h in range(ATT_HPG):
            ls = [lse(g, h) for g in range(3)]
            m = jnp.maximum(jnp.maximum(ls[0], ls[1]), ls[2])
            es = [jnp.exp(l - m) for l in ls]
            den = es[0] + es[1] + es[2]
            o = ((es[0] / den) * head(0, h) + (es[1] / den) * head(1, h)) + (es[2] / den) * head(2, h)
            att_scr[:, h * HEAD_DIM:(h + 1) * HEAD_DIM] = o.astype(BF16)

    a = _dot(att_scr[...], wa_ref[...])
    r = _dot(r_ref[...], wr_ref[...])
    o_ref[...] = (jax.nn.sigmoid(ga_ref[...].astype(F32)) * a
                  + jax.nn.sigmoid(gb_ref[...].astype(F32)) * r).astype(o_ref.dtype)


def _branch(att, orec, proj2, wa_bf, wr_bf, d, seq):
    t = orec.shape[0]
    tm = _pick(seq, (512, 256))
    per = seq // tm
    gw = _pick(d, (512, 256, 128))
    assert GATE_OFF % gw == 0
    ga0 = GATE_OFF // gw
    gb0 = (GATE_OFF + d) // gw
    dils = tuple(dil for _, dil in ATT_GROUPS[1:])

    def aspec(dil):
        return pl.BlockSpec((None, dil, tm // dil, ATT_OUT_COLS), lambda i, j: (i // per, 0, i % per, 0))

    return pl.pallas_call(
        functools.partial(_branch_kernel, dils=dils),
        out_shape=jax.ShapeDtypeStruct((t, d), BF16),
        grid=(t // tm, d // gw),
        in_specs=[pl.BlockSpec((tm, ATT_OUT_COLS), lambda i, j: (i, 0)), aspec(dils[0]), aspec(dils[1]),
                  pl.BlockSpec((tm, HGRN_WIDTH), lambda i, j: (i, 0)),
                  pl.BlockSpec((tm, gw), lambda i, j: (i, ga0 + j)),
                  pl.BlockSpec((tm, gw), lambda i, j: (i, gb0 + j)),
                  pl.BlockSpec((ATT_GW, gw), lambda i, j: (0, j)),
                  pl.BlockSpec((HGRN_WIDTH, gw), lambda i, j: (0, j))],
        out_specs=pl.BlockSpec((tm, gw), lambda i, j: (i, j)),
        scratch_shapes=[pltpu.VMEM((tm, ATT_GW), BF16),
                        pltpu.VMEM((len(dils), ATT_OUT_COLS // 128, tm, 128), F32)],
        compiler_params=_params(("parallel", "arbitrary"), 32),
        name="branch_merge",
    )(att[0].reshape(t, ATT_OUT_COLS), att[1], att[2], orec, proj2, proj2, wa_bf, wr_bf)


def _outproj_kernel(m_ref, w_ref, x_ref, gm_ref, pnm_ref, pnf_ref, sh_ref, sc_ref, x1_ref, hf_ref, hb_ref):
    y = _dot(m_ref[...], w_ref[...])
    x1 = x_ref[...] + gm_ref[...] * (_rms(y) * pnm_ref[...])
    x1_ref[...] = x1
    h2 = (_rms(x1) * pnf_ref[...]) * (1.0 + sc_ref[...]) + sh_ref[...]
    hf_ref[...] = h2
    hb_ref[...] = h2.astype(BF16)


def _outproj(merged, w_bf, x2, gate_m, pnm, pnf, shift_f, scale_f, seq):
    t, d = x2.shape
    tm = _pick(seq, (256, 128))
    per = seq // tm
    row = pl.BlockSpec((tm, d), lambda i: (i, 0))
    vec = pl.BlockSpec((1, d), lambda i: (0, 0))
    bvec = pl.BlockSpec((None, 1, d), lambda i: (i // per, 0, 0))
    return pl.pallas_call(
        _outproj_kernel,
        out_shape=(jax.ShapeDtypeStruct((t, d), F32), jax.ShapeDtypeStruct((t, d), F32),
                   jax.ShapeDtypeStruct((t, d), BF16)),
        grid=(t // tm,),
        in_specs=[row, pl.BlockSpec((d, d), lambda i: (0, 0)), row, bvec, vec, vec, bvec, bvec],
        out_specs=(row, row, row),
        compiler_params=_params(("parallel",), 48),
        name="out_proj",
    )(merged, w_bf, x2, gate_m, pnm.reshape(1, d), pnf.reshape(1, d), shift_f, scale_f)


def _router_kernel(h_ref, w_ref, bias_ref, up_ref, eid_ref, gate_ref, pos_ref, cnt_ref, carry_ref, *, tr):
    @pl.when(pl.program_id(0) == 0)
    def _():
        carry_ref[...] = jnp.zeros_like(carry_ref)

    per_group = N_EXPERTS // N_GROUPS
    sig = jax.nn.sigmoid(_dot_nt(w_ref[...], h_ref[...]))
    choice = sig + bias_ref[...]
    eidx = lax.broadcasted_iota(jnp.int32, (N_EXPERTS, tr), 0)

    c3 = choice.reshape(N_GROUPS, per_group, tr)
    sub = lax.broadcasted_iota(jnp.int32, (N_GROUPS, per_group, tr), 1)
    m1 = jnp.max(c3, axis=1, keepdims=True)
    first = jnp.min(jnp.where(c3 == m1, sub, per_group), axis=1, keepdims=True)
    m2 = jnp.max(jnp.where(sub == first, -jnp.inf, c3), axis=1, keepdims=True)
    gs = (m1 + m2).reshape(N_GROUPS, tr)

    gidx = lax.broadcasted_iota(jnp.int32, (N_GROUPS, tr), 0)
    grank = jnp.zeros((N_GROUPS, tr), jnp.int32)
    for g in range(N_GROUPS):
        row = gs[g:g + 1, :]
        grank = grank + ((row > gs) | ((row == gs) & (gidx > g))).astype(jnp.int32)
    gsel = jnp.where(grank < TOPK_GROUPS, 1.0, 0.0)
    emask = jnp.broadcast_to(gsel.reshape(N_GROUPS, 1, tr), (N_GROUPS, per_group, tr)).reshape(N_EXPERTS, tr)
    cm = jnp.where(emask > 0.5, choice, -jnp.inf)

    rank = jnp.zeros((N_EXPERTS, tr), jnp.int32)
    for e in range(N_EXPERTS):
        row = cm[e:e + 1, :]
        rank = rank + ((row > cm) | ((row == cm) & (eidx > e))).astype(jnp.int32)
    sel = rank < TOP_K

    denom = jnp.sum(jnp.where(sel, sig, 0.0), axis=0, keepdims=True)
    gate_full = sig / denom * ROUTED_SCALE

    sel_b = jnp.where(sel, 1.0, 0.0).astype(BF16)
    carry = carry_ref[...]
    cum = _dot(sel_b, up_ref[...]) + jnp.concatenate([carry] * (tr // 128), axis=1)
    carry_new = carry + _dot(sel_b, jnp.ones((tr, 128), BF16))
    carry_ref[...] = carry_new
    cnt_ref[...] = carry_new.astype(jnp.int32)
    posi = cum.astype(jnp.int32)

    eids, gates, poss = [], [], []
    for r in range(TOP_K):
        hit = rank == r
        eids.append(jnp.sum(jnp.where(hit, eidx, 0), axis=0, keepdims=True))
        gates.append(jnp.sum(jnp.where(hit, gate_full, 0.0), axis=0, keepdims=True))
        poss.append(jnp.sum(jnp.where(hit, posi, 0), axis=0, keepdims=True))
    eid_ref[...] = jnp.concatenate(eids, axis=0)
    gate_ref[...] = jnp.concatenate(gates, axis=0)
    pos_ref[...] = jnp.concatenate(poss, axis=0)


def _router(h2b, w_router, router_bias):
    t, d = h2b.shape
    tr = _pick(t, (256, 128))
    idx = jnp.arange(tr)
    upper = (idx[:, None] < idx[None, :]).astype(BF16)
    kout = pl.BlockSpec((TOP_K, tr), lambda i: (0, i))
    return pl.pallas_call(
        functools.partial(_router_kernel, tr=tr),
        out_shape=(jax.ShapeDtypeStruct((TOP_K, t), jnp.int32), jax.ShapeDtypeStruct((TOP_K, t), F32),
                   jax.ShapeDtypeStruct((TOP_K, t), jnp.int32), jax.ShapeDtypeStruct((N_EXPERTS, 128), jnp.int32)),
        grid=(t // tr,),
        in_specs=[pl.BlockSpec((tr, d), lambda i: (i, 0)),
                  pl.BlockSpec((N_EXPERTS, d), lambda i: (0, 0)),
                  pl.BlockSpec((N_EXPERTS, 1), lambda i: (0, 0)),
                  pl.BlockSpec((tr, tr), lambda i: (0, 0))],
        out_specs=(kout, kout, kout, pl.BlockSpec((N_EXPERTS, 128), lambda i: (0, 0))),
        scratch_shapes=[pltpu.VMEM((N_EXPERTS, 128), F32)],
        compiler_params=_params(("arbitrary",), 32),
        name="router",
    )(h2b, w_router.T.astype(BF16), router_bias.reshape(N_EXPERTS, 1).astype(F32), upper)


def _row_copy(src_hbm, idx, dst, sem):
    return pltpu.make_async_copy(src_hbm.at[pl.ds(idx, 1), :], dst, sem)


DMA_UNROLL = 8


def _dispatch_kernel(zs_ref, zn_ref, d_ref, h_ref, xs_hbm, zero_scr, sem, zsem, *, tm, n_blocks):
    i = pl.program_id(0)

    def body(it, carry):
        for u in range(DMA_UNROLL):
            r = it * DMA_UNROLL + u
            for k in range(TOP_K):
                pltpu.make_async_copy(h_ref.at[pl.ds(r, 1), :], xs_hbm.at[pl.ds(d_ref[0, k * tm + r], 1), :],
                                      sem).start(priority=(u * TOP_K + k) % 2)
        return carry
    lax.fori_loop(0, tm // DMA_UNROLL, body, 0)

    @pl.when(i == 0)
    def _():
        zero_scr[...] = jnp.zeros_like(zero_scr)
        for e in range(N_EXPERTS):
            def zbody(u, carry, e=e):
                pltpu.make_async_copy(zero_scr.at[pl.ds(0, 1), :], xs_hbm.at[pl.ds(zs_ref[e] + u, 1), :], zsem).start()
                return carry
            lax.fori_loop(0, zn_ref[e], zbody, 0)
        for e in range(N_EXPERTS):
            def zwait(u, carry):
                pltpu.make_async_copy(zero_scr.at[pl.ds(0, 1), :], xs_hbm.at[pl.ds(0, 1), :], zsem).wait()
                return carry
            lax.fori_loop(0, zn_ref[e], zwait, 0)

        def tbody(blk, carry):
            pltpu.make_async_copy(zero_scr, xs_hbm.at[pl.ds(blk * MOE_BLOCK, MOE_BLOCK), :], zsem).start()
            return carry
        lax.fori_loop(zn_ref[N_EXPERTS], n_blocks, tbody, 0)

        def twait(blk, carry):
            pltpu.make_async_copy(zero_scr, xs_hbm.at[pl.ds(0, MOE_BLOCK), :], zsem).wait()
            return carry
        lax.fori_loop(zn_ref[N_EXPERTS], n_blocks, twait, 0)

    for k in range(TOP_K):
        pltpu.make_async_copy(h_ref, xs_hbm.at[pl.ds(0, tm), :], sem).wait()


def _dispatch(h2f, dest3, zstart, znum, n_slots, tm):
    t, d = h2f.shape
    grid_spec = pltpu.PrefetchScalarGridSpec(
        num_scalar_prefetch=2,
        grid=(t // tm,),
        in_specs=[pl.BlockSpec((None, 1, TOP_K * tm), lambda i, zs, zn: (i, 0, 0), memory_space=pltpu.SMEM),
                  pl.BlockSpec((tm, d), lambda i, zs, zn: (i, 0))],
        out_specs=pl.BlockSpec(memory_space=pl.ANY),
        scratch_shapes=[pltpu.VMEM((MOE_BLOCK, d), F32), pltpu.SemaphoreType.DMA(()), pltpu.SemaphoreType.DMA(())],
    )
    return pl.pallas_call(
        functools.partial(_dispatch_kernel, tm=tm, n_blocks=n_slots // MOE_BLOCK),
        out_shape=jax.ShapeDtypeStruct((n_slots, d), F32),
        grid_spec=grid_spec,
        compiler_params=_params(("arbitrary",), 32),
        name="dispatch",
    )(zstart, znum, dest3, h2f)


def _expert_kernel(be_ref, nu_ref, x_ref, wg_ref, wu_ref, wd_ref, o_ref, wg_bf, wu_bf, wd_bf):
    i = pl.program_id(0)

    @pl.when((i == 0) | (be_ref[i] != be_ref[jnp.maximum(i - 1, 0)]))
    def _():
        wg_bf[...] = wg_ref[...].astype(BF16)
        wu_bf[...] = wu_ref[...].astype(BF16)
        wd_bf[...] = wd_ref[...].astype(BF16)

    @pl.when(i < nu_ref[0])
    def _():
        xb = x_ref[...].astype(BF16)
        hid = _silu(_dot(xb, wg_bf[...])) * _dot(xb, wu_bf[...])
        o_ref[...] = _dot(hid.astype(BF16), wd_bf[...])

    @pl.when(i >= nu_ref[0])
    def _():
        o_ref[...] = jnp.zeros_like(o_ref)


def _experts(x_sorted, block_expert, n_used, wg, wu, wd):
    n_slots, d = x_sorted.shape
    n_blocks = block_expert.shape[0]
    de = wg.shape[-1]
    grid_spec = pltpu.PrefetchScalarGridSpec(
        num_scalar_prefetch=2,
        grid=(n_blocks,),
        in_specs=[pl.BlockSpec((MOE_BLOCK, d), lambda i, be, nu: (jnp.minimum(i, nu[0] - 1), 0)),
                  pl.BlockSpec((None, d, de), lambda i, be, nu: (be[i], 0, 0)),
                  pl.BlockSpec((None, d, de), lambda i, be, nu: (be[i], 0, 0)),
                  pl.BlockSpec((None, de, d), lambda i, be, nu: (be[i], 0, 0))],
        out_specs=pl.BlockSpec((MOE_BLOCK, d), lambda i, be, nu: (i, 0)),
        scratch_shapes=[pltpu.VMEM((d, de), BF16), pltpu.VMEM((d, de), BF16), pltpu.VMEM((de, d), BF16)],
    )
    return pl.pallas_call(
        _expert_kernel,
        out_shape=jax.ShapeDtypeStruct((n_slots, d), F32),
        grid_spec=grid_spec,
        compiler_params=_params(("arbitrary",), 56),
        name="experts",
    )(block_expert, n_used, x_sorted, wg, wu, wd)


def _combine_kernel(dc_ref, dn_ref, y_hbm, gt_ref, h_ref, x1_ref, wg_ref, wu_ref, wd_ref, gf_ref, pn_ref,
                    o_ref, buf, sem, *, tm):
    i = pl.program_id(0)
    nsteps = pl.num_programs(0)
    slot = i % 2

    def issue(d_ref, s):
        def body(it, carry):
            for u in range(DMA_UNROLL):
                r = it * DMA_UNROLL + u
                for k in range(TOP_K):
                    _row_copy(y_hbm, d_ref[0, k * tm + r], buf.at[s, k, pl.ds(r, 1), :],
                              sem.at[s]).start(priority=(u * TOP_K + k) % 2)
            return carry
        lax.fori_loop(0, tm // DMA_UNROLL, body, 0)

    @pl.when(i == 0)
    def _():
        issue(dc_ref, 0)

    @pl.when(i + 1 < nsteps)
    def _():
        issue(dn_ref, 1 - slot)

    hb = h_ref[...]
    shared = _dot((_silu(_dot(hb, wg_ref[...])) * _dot(hb, wu_ref[...])).astype(BF16), wd_ref[...])

    for k in range(TOP_K):
        pltpu.make_async_copy(y_hbm.at[pl.ds(0, tm), :], buf.at[slot, k], sem.at[slot]).wait()

    gt = gt_ref[...]
    acc = shared
    for k in range(TOP_K):
        acc = acc + gt[:, k:k + 1] * buf[slot, k]
    o_ref[...] = x1_ref[...] + gf_ref[...] * (_rms(acc) * pn_ref[...])


def _combine(y_sorted, dest3, gate_t, h2b, x1, wgs, wus, wds, gate_f, pnf, seq, tm):
    t, d = x1.shape
    ds_ = wgs.shape[-1]
    per = seq // tm
    nt = t // tm
    smem = functools.partial(pl.BlockSpec, memory_space=pltpu.SMEM)
    row = pl.BlockSpec((tm, d), lambda i: (i, 0))
    return pl.pallas_call(
        functools.partial(_combine_kernel, tm=tm),
        out_shape=jax.ShapeDtypeStruct((t, d), F32),
        grid=(nt,),
        in_specs=[smem((None, 1, TOP_K * tm), lambda i: (i, 0, 0)),
                  smem((None, 1, TOP_K * tm), lambda i: (jnp.minimum(i + 1, nt - 1), 0, 0)),
                  pl.BlockSpec(memory_space=pl.ANY),
                  pl.BlockSpec((tm, TOP_K), lambda i: (i, 0)),
                  row, row,
                  pl.BlockSpec((d, ds_), lambda i: (0, 0)),
                  pl.BlockSpec((d, ds_), lambda i: (0, 0)),
                  pl.BlockSpec((ds_, d), lambda i: (0, 0)),
                  pl.BlockSpec((None, 1, d), lambda i: (i // per, 0, 0)),
                  pl.BlockSpec((1, d), lambda i: (0, 0))],
        out_specs=row,
        scratch_shapes=[pltpu.VMEM((2, TOP_K, tm, d), F32), pltpu.SemaphoreType.DMA((2,))],
        compiler_params=_params(("arbitrary",), 48),
        name="combine",
    )(dest3, dest3, y_sorted, gate_t, h2b, x1, wgs, wus, wds, gate_f, pnf.reshape(1, d))


def _layer(x, c, lb, w_ada, b_ada, pre_norm_mix, post_norm_mix, w_in, hgrn_norm, w_branch_attn, w_branch_hgrn,
           w_out, pre_norm_ffn, post_norm_ffn, w_router, router_bias, w_gate_e, w_up_e, w_down_e,
           w_gate_s, w_up_s, w_down_s):
    nb, s, d = x.shape
    t = nb * s
    x2 = x.reshape(t, d)

    mod = _ada(c, w_ada, b_ada).reshape(nb, 6, 1, d)
    shift_m, scale_m, gate_m, shift_f, scale_f, gate_f = (mod[:, k] for k in range(6))

    w_in_bf = w_in.astype(BF16)
    qkv = []
    for g, (_, dil) in enumerate(ATT_GROUPS):
        cols = jnp.concatenate([w_in_bf[:, p * ATT_WIDTH + g * ATT_GW:p * ATT_WIDTH + (g + 1) * ATT_GW]
                                for p in range(3)], axis=1)
        o = _inproj(x2, pre_norm_mix, shift_m, scale_m, cols, nb, s, dil, tn=3 * ATT_GW)
        qkv.append(o.reshape(nb, dil, s // dil, 3 * ATT_GW))
    proj2 = _inproj(x2, pre_norm_mix, shift_m, scale_m, w_in_bf[:, QKV_COLS:], nb, s)
    proj3 = proj2.reshape(nb, s, -1)

    att = [_attention_group(qkv[g], g) for g in range(len(ATT_GROUPS))]
    orec = _hgrn(proj3, lb, hgrn_norm)
    merged = _branch(att, orec, proj2, w_branch_attn.astype(BF16), w_branch_hgrn.astype(BF16), d, s)
    x1, h2f, h2b = _outproj(merged, w_out.astype(BF16), x2, gate_m, post_norm_mix, pre_norm_ffn,
                            shift_f, scale_f, s)

    eid, gate, pos, cnt = _router(h2b, w_router, router_bias)

    counts = cnt[:, 0]
    padded = (counts + MOE_BLOCK - 1) // MOE_BLOCK * MOE_BLOCK
    pad_end = jnp.cumsum(padded)
    pad_start = pad_end - padded
    n_blocks = -(-(t * TOP_K) // MOE_BLOCK) + N_EXPERTS
    onehot = eid[None] == jnp.arange(N_EXPERTS, dtype=jnp.int32)[:, None, None]
    dest = jnp.sum(jnp.where(onehot, pad_start[:, None, None], 0), axis=0) + pos
    block_start = jnp.arange(n_blocks, dtype=jnp.int32) * MOE_BLOCK
    block_expert = jnp.minimum(jnp.sum((block_start[:, None] >= pad_end[None, :]).astype(jnp.int32), axis=1),
                               N_EXPERTS - 1)
    n_used = (pad_end[-1:] // MOE_BLOCK).astype(jnp.int32)

    tm = _pick(s, (128,))
    nt = t // tm
    dest3 = dest.reshape(TOP_K, nt, tm).transpose(1, 0, 2).reshape(nt, 1, TOP_K * tm)
    znum = jnp.concatenate([padded - counts, n_used]).astype(jnp.int32)
    x_sorted = _dispatch(h2f, dest3, (pad_start + counts).astype(jnp.int32), znum, n_blocks * MOE_BLOCK, tm)
    y_sorted = _experts(x_sorted, block_expert, n_used, w_gate_e, w_up_e, w_down_e)
    out = _combine(y_sorted, dest3, gate.T, h2b, x1, w_gate_s.astype(BF16), w_up_s.astype(BF16),
                   w_down_s.astype(BF16), gate_f, post_norm_ffn, s, tm)
    return out.reshape(nb, s, d)


def kernel(x, c, w_ada, b_ada, pre_norm_mix, post_norm_mix, w_in, hgrn_lb_logits, hgrn_norm, w_branch_attn,
           w_branch_hgrn, w_out, pre_norm_ffn, post_norm_ffn, w_router, router_bias, w_gate_e, w_up_e, w_down_e,
           w_gate_s, w_up_s, w_down_s):
    lb_table = jnp.cumsum(jax.nn.softmax(hgrn_lb_logits.astype(F32), axis=0), axis=0)
    depth = w_ada.shape[0]
    for l in range(depth):
        x = _layer(x, c, lb_table[l], w_ada[l], b_ada[l], pre_norm_mix[l], post_norm_mix[l], w_in[l],
                   hgrn_norm[l], w_branch_attn[l], w_branch_hgrn[l], w_out[l], pre_norm_ffn[l],
                   post_norm_ffn[l], w_router[l], router_bias[l], w_gate_e[l], w_up_e[l], w_down_e[l],
                   w_gate_s[l], w_up_s[l], w_down_s[l])
    return x
```

```python
import functools

import jax
import jax.numpy as jnp
from jax import lax
from jax.experimental import pallas as pl
from jax.experimental.pallas import tpu as pltpu

F32 = jnp.float32
BF16 = jnp.bfloat16

HEAD_DIM = 128
ATT_GROUPS = ((128, 1), (512, 4), (2048, 16))
ATT_HPG = 4
ATT_HEADS = ATT_HPG * len(ATT_GROUPS)
ATT_WIDTH = ATT_HEADS * HEAD_DIM
ATT_GW = ATT_HPG * HEAD_DIM
ATT_BLOCK = 128
ATT_OUT_COLS = ATT_GW + HEAD_DIM
LSE_LANES = HEAD_DIM // ATT_HPG
HGRN_HEADS = 8
HGRN_D = 128
HGRN_WIDTH = HGRN_HEADS * HGRN_D
HGRN_CHUNK = 64
HGRN_DIRECT = 8
N_EXPERTS = 64
N_GROUPS = 8
TOPK_GROUPS = 4
TOP_K = 8
ROUTED_SCALE = 2.5
MOE_BLOCK = 256
NORM_EPS = 1e-6
V7X_VMEM_LIMIT = 56 * 1024 * 1024

QKV_COLS = 3 * ATT_WIDTH
RQ_OFF = 0
RF_OFF = RQ_OFF + HGRN_WIDTH
RI_OFF = RF_OFF + HGRN_WIDTH
RG_OFF = RI_OFF + HGRN_WIDTH
GATE_OFF = RG_OFF + HGRN_WIDTH


def _pick(n, cands):
    for c in cands:
        if n % c == 0:
            return c
    raise ValueError(f"no tile of {cands} divides {n}")


def _params(sem, vmem_mib=None):
    kw = dict(dimension_semantics=sem)
    if vmem_mib is not None:
        kw["vmem_limit_bytes"] = min(vmem_mib * 1024 * 1024, V7X_VMEM_LIMIT)
    return pltpu.CompilerParams(**kw)


def _dot(a, b):
    return jnp.dot(a, b, preferred_element_type=F32)


def _dot_nt(a, b):
    return lax.dot_general(a, b, (((1,), (1,)), ((), ())), preferred_element_type=F32)


def _dot_tn(a, b):
    return lax.dot_general(a, b, (((0,), (0,)), ((), ())), preferred_element_type=F32)


def _rms(y):
    return y * lax.rsqrt(jnp.mean(y * y, axis=-1, keepdims=True) + NORM_EPS)


def _silu(a):
    return a * jax.nn.sigmoid(a)


ADA_KCHUNK = 128


def _ada_kernel(ct_ref, w_ref, b_ref, o_ref, *, nb, d):
    a = _silu(ct_ref[...])
    tn = w_ref.shape[1]
    accs = [jnp.zeros((8, tn), F32) for _ in range(nb)]
    for kc in range(d // ADA_KCHUNK):
        wc = w_ref[kc * ADA_KCHUNK:(kc + 1) * ADA_KCHUNK, :]
        ac = a[kc * ADA_KCHUNK:(kc + 1) * ADA_KCHUNK, :]
        for b in range(nb):
            p = ac[:, b:b + 1] * wc
            accs[b] = accs[b] + p.reshape(ADA_KCHUNK // 8, 8, tn).sum(axis=0)
    rows = [jnp.sum(acc, axis=0, keepdims=True) for acc in accs]
    o_ref[...] = jnp.concatenate(rows, axis=0) + b_ref[...]


def _ada(c, w_ada, b_ada):
    nb, d = c.shape
    n = w_ada.shape[1]
    tn = _pick(n, (512, 256, 128))
    return pl.pallas_call(
        functools.partial(_ada_kernel, nb=nb, d=d),
        out_shape=jax.ShapeDtypeStruct((nb, n), F32),
        grid=(n // tn,),
        in_specs=[pl.BlockSpec((d, nb), lambda j: (0, 0)),
                  pl.BlockSpec((d, tn), lambda j: (0, j)),
                  pl.BlockSpec((1, tn), lambda j: (0, j))],
        out_specs=pl.BlockSpec((nb, tn), lambda j: (0, j)),
        compiler_params=_params(("arbitrary",), 32),
        name="ada_mod",
    )(c.T, w_ada, b_ada.reshape(1, n))


def _inproj_kernel(x_ref, g_ref, sh_ref, sc_ref, w_ref, o_ref, h_scr, *slab, dil):
    @pl.when(pl.program_id(1) == 0)
    def _():
        y = _rms(x_ref[...]) * g_ref[...]
        h_scr[...] = (y * (1.0 + sc_ref[...]) + sh_ref[...]).astype(BF16)

    o = _dot(h_scr[...], w_ref[...])
    if dil == 1:
        o_ref[...] = o.astype(o_ref.dtype)
    else:
        slab_ref, = slab
        tm, tn = o.shape
        for c in range(tn // 128):
            slab_ref[c] = o[:, c * 128:(c + 1) * 128]
        for r in range(dil):
            for c in range(tn // 128):
                o_ref[r, :, c * 128:(c + 1) * 128] = slab_ref[c, pl.ds(r, tm // dil, stride=dil), :].astype(o_ref.dtype)


def _inproj(x2, gain, shift, scale, w_bf, nb, seq, dil=1, tn=None):
    t, d = x2.shape
    n = w_bf.shape[1]
    tm = _pick(seq, (512, 256))
    tn = tn or _pick(n, (1024, 512, 256, 128))
    per = seq // tm
    if dil == 1:
        out_shape = jax.ShapeDtypeStruct((t, n), BF16)
        out_spec = pl.BlockSpec((tm, tn), lambda i, j: (i, j))
        scratch = [pltpu.VMEM((tm, d), BF16)]
    else:
        out_shape = jax.ShapeDtypeStruct((nb, dil, seq // dil, n), BF16)
        out_spec = pl.BlockSpec((None, dil, tm // dil, tn), lambda i, j: (i // per, 0, i % per, j))
        scratch = [pltpu.VMEM((tm, d), BF16), pltpu.VMEM((tn // 128, tm, 128), F32)]
    return pl.pallas_call(
        functools.partial(_inproj_kernel, dil=dil),
        out_shape=out_shape,
        grid=(t // tm, n // tn),
        in_specs=[pl.BlockSpec((tm, d), lambda i, j: (i, 0)),
                  pl.BlockSpec((1, d), lambda i, j: (0, 0)),
                  pl.BlockSpec((None, 1, d), lambda i, j: (i // per, 0, 0)),
                  pl.BlockSpec((None, 1, d), lambda i, j: (i // per, 0, 0)),
                  pl.BlockSpec((d, tn), lambda i, j: (0, j))],
        out_specs=out_spec,
        scratch_shapes=scratch,
        compiler_params=_params(("parallel", "arbitrary"), 48),
        name=f"in_proj_d{dil}_n{n}",
    )(x2, gain.reshape(1, d), shift, scale, w_bf)


def _attn_kernel(q_ref, kp_ref, kc_ref, vp_ref, vc_ref, o_ref, *, dil, slopes, scale):
    n = pl.program_id(2)
    qi = lax.broadcasted_iota(jnp.int32, (ATT_BLOCK, ATT_BLOCK), 0)
    ki = lax.broadcasted_iota(jnp.int32, (ATT_BLOCK, ATT_BLOCK), 1)
    jc = qi - ki
    jp = jc + ATT_BLOCK
    prev_lim = jnp.where(n > 0, ATT_BLOCK, -1)
    valid_c = jc >= 0
    valid_p = jp <= prev_lim
    lane = lax.broadcasted_iota(jnp.int32, (ATT_BLOCK, HEAD_DIM), 1)
    lse_blk = jnp.zeros((ATT_BLOCK, HEAD_DIM), F32)
    for h in range(ATT_HPG):
        hs = slice(h * HEAD_DIM, (h + 1) * HEAD_DIM)
        q = q_ref[:, hs]
        bias = slopes[h] * dil
        s_c = _dot_nt(q, kc_ref[:, hs]) * scale - bias * jc.astype(F32)
        s_p = _dot_nt(q, kp_ref[:, hs]) * scale - bias * jp.astype(F32)
        s_c = jnp.where(valid_c, s_c, -jnp.inf)
        s_p = jnp.where(valid_p, s_p, -jnp.inf)
        m = jnp.maximum(jnp.max(s_c, axis=-1, keepdims=True), jnp.max(s_p, axis=-1, keepdims=True))
        p_c = jnp.exp(s_c - m)
        p_p = jnp.exp(s_p - m)
        l = jnp.sum(p_c, axis=-1, keepdims=True) + jnp.sum(p_p, axis=-1, keepdims=True)
        o = _dot(p_c.astype(BF16), vc_ref[:, hs]) + _dot(p_p.astype(BF16), vp_ref[:, hs])
        o_ref[:, hs] = o / l
        lse = m + jnp.log(l)
        lse_blk = jnp.where((lane >= h * LSE_LANES) & (lane < (h + 1) * LSE_LANES), lse, lse_blk)
    o_ref[:, ATT_GW:] = lse_blk


def _attention_group(qkv, g):
    nb_, dil, l, _ = qkv.shape
    window, dil_ = ATT_GROUPS[g]
    assert dil == dil_ and window // dil == ATT_BLOCK and l % ATT_BLOCK == 0
    nblk = l // ATT_BLOCK
    slopes = tuple(2.0 ** (-8.0 * (g * ATT_HPG + i + 1) / ATT_HEADS) for i in range(ATT_HPG))

    def spec(part, prev):
        if prev:
            return pl.BlockSpec((None, None, ATT_BLOCK, ATT_GW),
                                lambda b, r, n: (b, r, jnp.maximum(n - 1, 0), part))
        return pl.BlockSpec((None, None, ATT_BLOCK, ATT_GW), lambda b, r, n: (b, r, n, part))

    return pl.pallas_call(
        functools.partial(_attn_kernel, dil=float(dil), slopes=slopes, scale=HEAD_DIM ** -0.5),
        out_shape=jax.ShapeDtypeStruct((nb_, dil, l, ATT_OUT_COLS), F32),
        grid=(nb_, dil, nblk),
        in_specs=[spec(0, False), spec(1, True), spec(1, False), spec(2, True), spec(2, False)],
        out_specs=pl.BlockSpec((None, None, ATT_BLOCK, ATT_OUT_COLS), lambda b, r, n: (b, r, n, 0)),
        compiler_params=_params(("parallel", "parallel", "arbitrary")),
        name=f"attn_g{g}",
    )(qkv, qkv, qkv, qkv, qkv)


def _hgrn_kernel(q_ref, f_ref, i_ref, g_ref, lb_ref, gn_ref, tri_ref, o_ref, st_ref, sh_ref, *, tb):
    @pl.when(pl.program_id(2) == 0)
    def _():
        st_ref[...] = jnp.zeros_like(st_ref)
        sh_ref[:, :, 0:HGRN_DIRECT, :] = jnp.zeros((3, tb // HGRN_DIRECT, HGRN_DIRECT, HGRN_D), F32)

    c_ = HGRN_CHUNK
    nc = tb // c_
    q = q_ref[...].astype(F32)
    v = i_ref[...].astype(F32)
    lb = lb_ref[...]
    f = lb + (1.0 - lb) * jax.nn.sigmoid(f_ref[...].astype(F32))
    kk = 1.0 - f
    lf = jnp.log(f)

    hi = lf.astype(BF16)
    r1 = lf - hi.astype(F32)
    mid = r1.astype(BF16)
    lo = (r1 - mid.astype(F32)).astype(BF16)
    tri = tri_ref[...]
    b = _dot(tri, hi) + _dot(tri, mid) + _dot(tri, lo)

    def ref_rows(blk, row):
        b3 = b.reshape(tb // blk, blk, HGRN_D)
        return jnp.broadcast_to(b3[:, row:row + 1, :], (tb // blk, blk, HGRN_D)).reshape(tb, HGRN_D)

    ti = lax.broadcasted_iota(jnp.int32, (c_, c_), 0)
    si = lax.broadcasted_iota(jnp.int32, (c_, c_), 1)
    ssum = jnp.zeros((nc, c_, c_), F32)
    half = c_ // 2
    while half >= HGRN_DIRECT:
        bref = ref_rows(2 * half, half)
        ql = (q * jnp.exp(jnp.minimum(b - bref, 0.0))).astype(BF16).reshape(nc, c_, HGRN_D)
        kl = (kk * jnp.exp(jnp.minimum(bref - b, 0.0))).astype(BF16).reshape(nc, c_, HGRN_D)
        s_l = jnp.einsum("ctk,csk->cts", ql, kl, preferred_element_type=F32)
        mask = ((ti // (2 * half)) == (si // (2 * half))) & (((ti // half) % 2) == 1) & (((si // half) % 2) == 0)
        ssum = ssum + jnp.where(mask[None], s_l, 0.0)
        half //= 2
    v3 = v.astype(BF16).reshape(nc, c_, HGRN_D)
    o_acc = jnp.einsum("cts,csv->ctv", ssum.astype(BF16), v3, preferred_element_type=F32).reshape(tb, HGRN_D)

    nt8 = tb // HGRN_DIRECT
    for plane, val in enumerate((kk, f, v)):
        sh_ref[plane, :, HGRN_DIRECT:, :] = val.reshape(nt8, HGRN_DIRECT, HGRN_D)

    def shifted(plane, lag):
        st = HGRN_DIRECT - lag
        return sh_ref[plane, :, st:st + HGRN_DIRECT, :].reshape(tb, HGRN_D)

    ones = jnp.ones((HGRN_D, HGRN_D), BF16)
    decay = None
    for lag in range(HGRN_DIRECT):
        if lag == 0:
            w = q * kk
        else:
            decay = f if lag == 1 else decay * shifted(1, lag - 1)
            w = q * shifted(0, lag) * decay
        o_acc = o_acc + _dot(w.astype(BF16), ones) * (v if lag == 0 else shifted(2, lag))

    blast = ref_rows(c_, c_ - 1)
    qe = (q * jnp.exp(b)).astype(BF16)
    kt = (kk * jnp.exp(blast - b)).astype(BF16)
    vb = v.astype(BF16)
    st_t = st_ref[...]
    inter = []
    for c in range(nc):
        rows = slice(c * c_, (c + 1) * c_)
        inter.append(_dot_nt(qe[rows], st_t.astype(BF16)))
        dec = jnp.exp(blast[c * c_:c * c_ + 1, :])
        st_t = st_t * dec + _dot_tn(vb[rows], kt[rows])
    st_ref[...] = st_t
    o = o_acc + jnp.concatenate(inter, axis=0)

    y = _rms(o) * gn_ref[...]
    o_ref[...] = (y * _silu(g_ref[...].astype(F32))).astype(o_ref.dtype)


def _hgrn(proj3, lb, gnorm):
    nb_, s, nc = proj3.shape
    tb = _pick(s, (512, 256, 128, 64))
    idx = jnp.arange(tb)
    tri = ((idx[:, None] // HGRN_CHUNK == idx[None, :] // HGRN_CHUNK) & (idx[None, :] <= idx[:, None])).astype(BF16)

    def spec(off):
        cb = off // HGRN_D
        return pl.BlockSpec((None, tb, HGRN_D), lambda b, h, n: (b, n, cb + h))

    vec = pl.BlockSpec((1, HGRN_D), lambda b, h, n: (0, h))
    out = pl.pallas_call(
        functools.partial(_hgrn_kernel, tb=tb),
        out_shape=jax.ShapeDtypeStruct((nb_, s, HGRN_WIDTH), BF16),
        grid=(nb_, HGRN_HEADS, s // tb),
        in_specs=[spec(RQ_OFF), spec(RF_OFF), spec(RI_OFF), spec(RG_OFF), vec, vec,
                  pl.BlockSpec((tb, tb), lambda b, h, n: (0, 0))],
        out_specs=pl.BlockSpec((None, tb, HGRN_D), lambda b, h, n: (b, n, h)),
        scratch_shapes=[pltpu.VMEM((HGRN_D, HGRN_D), F32),
                        pltpu.VMEM((3, tb // HGRN_DIRECT, 2 * HGRN_DIRECT, HGRN_D), F32)],
        compiler_params=_params(("parallel", "parallel", "arbitrary"), 32),
        name="hgrn2",
    )(proj3, proj3, proj3, proj3, lb.reshape(1, HGRN_WIDTH), gnorm.reshape(1, HGRN_WIDTH), tri)
    return out.reshape(nb_ * s, HGRN_WIDTH)


def _branch_kernel(a0_ref, a1_ref, a2_ref, r_ref, ga_ref, gb_ref, wa_ref, wr_ref, o_ref, att_scr, nat_scr,
                   *, dils):
    @pl.when(pl.program_id(1) == 0)
    def _():
        tm = att_scr.shape[0]
        nslab = ATT_OUT_COLS // 128
        for gi, (a_ref, dil) in enumerate(zip((a1_ref, a2_ref), dils)):
            for r in range(dil):
                for c in range(nslab):
                    nat_scr[gi, c, pl.ds(r, tm // dil, stride=dil), :] = a_ref[r, :, c * 128:(c + 1) * 128]

        def head(g, h):
            return a0_ref[:, h * HEAD_DIM:(h + 1) * HEAD_DIM] if g == 0 else nat_scr[g - 1, h]

        def lse(g, h):
            if g == 0:
                return a0_ref[:, ATT_GW + h * LSE_LANES:ATT_GW + h * LSE_LANES + 1]
            return nat_scr[g - 1, ATT_HPG, :, h * LSE_LANES:h * LSE_LANES + 1]

        for h in range(ATT_HPG):
            ls = [lse(g, h) for g in range(3)]
            m = jnp.maximum(jnp.maximum(ls[0], ls[1]), ls[2])
            es = [jnp.exp(l - m) for l in ls]
            den = es[0] + es[1] + es[2]
            o = ((es[0] / den) * head(0, h) + (es[1] / den) * head(1, h)) + (es[2] / den) * head(2, h)
            att_scr[:, h * HEAD_DIM:(h + 1) * HEAD_DIM] = o.astype(BF16)

    a = _dot(att_scr[...], wa_ref[...])
    r = _dot(r_ref[...], wr_ref[...])
    o_ref[...] = (jax.nn.sigmoid(ga_ref[...].astype(F32)) * a
                  + jax.nn.sigmoid(gb_ref[...].astype(F32)) * r).astype(o_ref.dtype)


def _branch(att, orec, proj2, wa_bf, wr_bf, d, seq):
    t = orec.shape[0]
    tm = _pick(seq, (512, 256))
    per = seq // tm
    gw = _pick(d, (512, 256, 128))
    assert GATE_OFF % gw == 0
    ga0 = GATE_OFF // gw
    gb0 = (GATE_OFF + d) // gw
    dils = tuple(dil for _, dil in ATT_GROUPS[1:])

    def aspec(dil):
        return pl.BlockSpec((None, dil, tm // dil, ATT_OUT_COLS), lambda i, j: (i // per, 0, i % per, 0))

    return pl.pallas_call(
        functools.partial(_branch_kernel, dils=dils),
        out_shape=jax.ShapeDtypeStruct((t, d), BF16),
        grid=(t // tm, d // gw),
        in_specs=[pl.BlockSpec((tm, ATT_OUT_COLS), lambda i, j: (i, 0)), aspec(dils[0]), aspec(dils[1]),
                  pl.BlockSpec((tm, HGRN_WIDTH), lambda i, j: (i, 0)),
                  pl.BlockSpec((tm, gw), lambda i, j: (i, ga0 + j)),
                  pl.BlockSpec((tm, gw), lambda i, j: (i, gb0 + j)),
                  pl.BlockSpec((ATT_GW, gw), lambda i, j: (0, j)),
                  pl.BlockSpec((HGRN_WIDTH, gw), lambda i, j: (0, j))],
        out_specs=pl.BlockSpec((tm, gw), lambda i, j: (i, j)),
        scratch_shapes=[pltpu.VMEM((tm, ATT_GW), BF16),
                        pltpu.VMEM((len(dils), ATT_OUT_COLS // 128, tm, 128), F32)],
        compiler_params=_params(("parallel", "arbitrary"), 32),
        name="branch_merge",
    )(att[0].reshape(t, ATT_OUT_COLS), att[1], att[2], orec, proj2, proj2, wa_bf, wr_bf)


def _outproj_kernel(m_ref, w_ref, x_ref, gm_ref, pnm_ref, pnf_ref, sh_ref, sc_ref, x1_ref, hf_ref, hb_ref):
    y = _dot(m_ref[...], w_ref[...])
    x1 = x_ref[...] + gm_ref[...] * (_rms(y) * pnm_ref[...])
    x1_ref[...] = x1
    h2 = (_rms(x1) * pnf_ref[...]) * (1.0 + sc_ref[...]) + sh_ref[...]
    hf_ref[...] = h2
    hb_ref[...] = h2.astype(BF16)


def _outproj(merged, w_bf, x2, gate_m, pnm, pnf, shift_f, scale_f, seq):
    t, d = x2.shape
    tm = _pick(seq, (256, 128))
    per = seq // tm
    row = pl.BlockSpec((tm, d), lambda i: (i, 0))
    vec = pl.BlockSpec((1, d), lambda i: (0, 0))
    bvec = pl.BlockSpec((None, 1, d), lambda i: (i // per, 0, 0))
    return pl.pallas_call(
        _outproj_kernel,
        out_shape=(jax.ShapeDtypeStruct((t, d), F32), jax.ShapeDtypeStruct((t, d), F32),
                   jax.ShapeDtypeStruct((t, d), BF16)),
        grid=(t // tm,),
        in_specs=[row, pl.BlockSpec((d, d), lambda i: (0, 0)), row, bvec, vec, vec, bvec, bvec],
        out_specs=(row, row, row),
        compiler_params=_params(("parallel",), 48),
        name="out_proj",
    )(merged, w_bf, x2, gate_m, pnm.reshape(1, d), pnf.reshape(1, d), shift_f, scale_f)


def _router_kernel(h_ref, w_ref, bias_ref, up_ref, eid_ref, gate_ref, pos_ref, cnt_ref, carry_ref, *, tr):
    @pl.when(pl.program_id(0) == 0)
    def _():
        carry_ref[...] = jnp.zeros_like(carry_ref)

    per_group = N_EXPERTS // N_GROUPS
    sig = jax.nn.sigmoid(_dot_nt(w_ref[...], h_ref[...]))
    choice = sig + bias_ref[...]
    eidx = lax.broadcasted_iota(jnp.int32, (N_EXPERTS, tr), 0)

    c3 = choice.reshape(N_GROUPS, per_group, tr)
    sub = lax.broadcasted_iota(jnp.int32, (N_GROUPS, per_group, tr), 1)
    m1 = jnp.max(c3, axis=1, keepdims=True)
    first = jnp.min(jnp.where(c3 == m1, sub, per_group), axis=1, keepdims=True)
    m2 = jnp.max(jnp.where(sub == first, -jnp.inf, c3), axis=1, keepdims=True)
    gs = (m1 + m2).reshape(N_GROUPS, tr)

    gidx = lax.broadcasted_iota(jnp.int32, (N_GROUPS, tr), 0)
    grank = jnp.zeros((N_GROUPS, tr), jnp.int32)
    for g in range(N_GROUPS):
        row = gs[g:g + 1, :]
        grank = grank + ((row > gs) | ((row == gs) & (gidx > g))).astype(jnp.int32)
    gsel = jnp.where(grank < TOPK_GROUPS, 1.0, 0.0)
    emask = jnp.broadcast_to(gsel.reshape(N_GROUPS, 1, tr), (N_GROUPS, per_group, tr)).reshape(N_EXPERTS, tr)
    cm = jnp.where(emask > 0.5, choice, -jnp.inf)

    rank = jnp.zeros((N_EXPERTS, tr), jnp.int32)
    for e in range(N_EXPERTS):
        row = cm[e:e + 1, :]
        rank = rank + ((row > cm) | ((row == cm) & (eidx > e))).astype(jnp.int32)
    sel = rank < TOP_K

    denom = jnp.sum(jnp.where(sel, sig, 0.0), axis=0, keepdims=True)
    gate_full = sig / denom * ROUTED_SCALE

    sel_b = jnp.where(sel, 1.0, 0.0).astype(BF16)
    carry = carry_ref[...]
    cum = _dot(sel_b, up_ref[...]) + jnp.concatenate([carry] * (tr // 128), axis=1)
    carry_new = carry + _dot(sel_b, jnp.ones((tr, 128), BF16))
    carry_ref[...] = carry_new
    cnt_ref[...] = carry_new.astype(jnp.int32)
    posi = cum.astype(jnp.int32)

    eids, gates, poss = [], [], []
    for r in range(TOP_K):
        hit = rank == r
        eids.append(jnp.sum(jnp.where(hit, eidx, 0), axis=0, keepdims=True))
        gates.append(jnp.sum(jnp.where(hit, gate_full, 0.0), axis=0, keepdims=True))
        poss.append(jnp.sum(jnp.where(hit, posi, 0), axis=0, keepdims=True))
    eid_ref[...] = jnp.concatenate(eids, axis=0)
    gate_ref[...] = jnp.concatenate(gates, axis=0)
    pos_ref[...] = jnp.concatenate(poss, axis=0)


def _router(h2b, w_router, router_bias):
    t, d = h2b.shape
    tr = _pick(t, (256, 128))
    idx = jnp.arange(tr)
    upper = (idx[:, None] < idx[None, :]).astype(BF16)
    kout = pl.BlockSpec((TOP_K, tr), lambda i: (0, i))
    return pl.pallas_call(
        functools.partial(_router_kernel, tr=tr),
        out_shape=(jax.ShapeDtypeStruct((TOP_K, t), jnp.int32), jax.ShapeDtypeStruct((TOP_K, t), F32),
                   jax.ShapeDtypeStruct((TOP_K, t), jnp.int32), jax.ShapeDtypeStruct((N_EXPERTS, 128), jnp.int32)),
        grid=(t // tr,),
        in_specs=[pl.BlockSpec((tr, d), lambda i: (i, 0)),
                  pl.BlockSpec((N_EXPERTS, d), lambda i: (0, 0)),
                  pl.BlockSpec((N_EXPERTS, 1), lambda i: (0, 0)),
                  pl.BlockSpec((tr, tr), lambda i: (0, 0))],
        out_specs=(kout, kout, kout, pl.BlockSpec((N_EXPERTS, 128), lambda i: (0, 0))),
        scratch_shapes=[pltpu.VMEM((N_EXPERTS, 128), F32)],
        compiler_params=_params(("arbitrary",), 32),
        name="router",
    )(h2b, w_router.T.astype(BF16), router_bias.reshape(N_EXPERTS, 1).astype(F32), upper)


def _row_copy(src_hbm, idx, dst, sem):
    return pltpu.make_async_copy(src_hbm.at[pl.ds(idx, 1), :], dst, sem)


DMA_UNROLL = 8


def _dispatch_kernel(zs_ref, zn_ref, d_ref, h_ref, xs_hbm, zero_scr, sem, zsem, *, tm, n_blocks):
    i = pl.program_id(0)

    def body(it, carry):
        for u in range(DMA_UNROLL):
            r = it * DMA_UNROLL + u
            for k in range(TOP_K):
                pltpu.make_async_copy(h_ref.at[pl.ds(r, 1), :], xs_hbm.at[pl.ds(d_ref[0, k * tm + r], 1), :],
                                      sem).start(priority=(u * TOP_K + k) % 2)
        return carry
    lax.fori_loop(0, tm // DMA_UNROLL, body, 0)

    @pl.when(i == 0)
    def _():
        zero_scr[...] = jnp.zeros_like(zero_scr)
        for e in range(N_EXPERTS):
            def zbody(u, carry, e=e):
                pltpu.make_async_copy(zero_scr.at[pl.ds(0, 1), :], xs_hbm.at[pl.ds(zs_ref[e] + u, 1), :], zsem).start()
                return carry
            lax.fori_loop(0, zn_ref[e], zbody, 0)
        for e in range(N_EXPERTS):
            def zwait(u, carry):
                pltpu.make_async_copy(zero_scr.at[pl.ds(0, 1), :], xs_hbm.at[pl.ds(0, 1), :], zsem).wait()
                return carry
            lax.fori_loop(0, zn_ref[e], zwait, 0)

        def tbody(blk, carry):
            pltpu.make_async_copy(zero_scr, xs_hbm.at[pl.ds(blk * MOE_BLOCK, MOE_BLOCK), :], zsem).start()
            return carry
        lax.fori_loop(zn_ref[N_EXPERTS], n_blocks, tbody, 0)

        def twait(blk, carry):
            pltpu.make_async_copy(zero_scr, xs_hbm.at[pl.ds(0, MOE_BLOCK), :], zsem).wait()
            return carry
        lax.fori_loop(zn_ref[N_EXPERTS], n_blocks, twait, 0)

    for k in range(TOP_K):
        pltpu.make_async_copy(h_ref, xs_hbm.at[pl.ds(0, tm), :], sem).wait()


def _dispatch(h2f, dest3, zstart, znum, n_slots, tm):
    t, d = h2f.shape
    grid_spec = pltpu.PrefetchScalarGridSpec(
        num_scalar_prefetch=2,
        grid=(t // tm,),
        in_specs=[pl.BlockSpec((None, 1, TOP_K * tm), lambda i, zs, zn: (i, 0, 0), memory_space=pltpu.SMEM),
                  pl.BlockSpec((tm, d), lambda i, zs, zn: (i, 0))],
        out_specs=pl.BlockSpec(memory_space=pl.ANY),
        scratch_shapes=[pltpu.VMEM((MOE_BLOCK, d), F32), pltpu.SemaphoreType.DMA(()), pltpu.SemaphoreType.DMA(())],
    )
    return pl.pallas_call(
        functools.partial(_dispatch_kernel, tm=tm, n_blocks=n_slots // MOE_BLOCK),
        out_shape=jax.ShapeDtypeStruct((n_slots, d), F32),
        grid_spec=grid_spec,
        compiler_params=_params(("arbitrary",), 32),
        name="dispatch",
    )(zstart, znum, dest3, h2f)


def _expert_kernel(be_ref, nu_ref, first_ref, par_ref, nxt_ref, x_ref, wg_hbm, wu_hbm, wd_hbm, o_ref,
                   wg_f, wu_f, wd_f, wg_bf, wu_bf, wd_bf, sem):
    i = pl.program_id(0)

    def weight_copies(e, s):
        return [pltpu.make_async_copy(src.at[e], dst.at[s], sem.at[s])
                for src, dst in ((wg_hbm, wg_f), (wu_hbm, wu_f), (wd_hbm, wd_f))]

    @pl.when(i == 0)
    def _():
        for cp in weight_copies(be_ref[0], 0):
            cp.start()

    @pl.when(first_ref[i] == 1)
    def _():
        p = par_ref[i]
        for cp in weight_copies(be_ref[i], p):
            cp.wait()
        wg_bf[...] = wg_f[p].astype(BF16)
        wu_bf[...] = wu_f[p].astype(BF16)
        wd_bf[...] = wd_f[p].astype(BF16)

        @pl.when(nxt_ref[i] >= 0)
        def _():
            for cp in weight_copies(nxt_ref[i], 1 - p):
                cp.start()

    @pl.when(i < nu_ref[0])
    def _():
        xb = x_ref[...].astype(BF16)
        hid = _silu(_dot(xb, wg_bf[...])) * _dot(xb, wu_bf[...])
        o_ref[...] = _dot(hid.astype(BF16), wd_bf[...])

    @pl.when(i >= nu_ref[0])
    def _():
        o_ref[...] = jnp.zeros_like(o_ref)


def _experts(x_sorted, block_expert, n_used, first, par, nxt, wg, wu, wd):
    n_slots, d = x_sorted.shape
    n_blocks = block_expert.shape[0]
    de = wg.shape[-1]
    hbm = pl.BlockSpec(memory_space=pl.ANY)
    grid_spec = pltpu.PrefetchScalarGridSpec(
        num_scalar_prefetch=5,
        grid=(n_blocks,),
        in_specs=[pl.BlockSpec((MOE_BLOCK, d), lambda i, be, nu, *_: (jnp.minimum(i, nu[0] - 1), 0)),
                  hbm, hbm, hbm],
        out_specs=pl.BlockSpec((MOE_BLOCK, d), lambda i, *_: (i, 0)),
        scratch_shapes=[pltpu.VMEM((2, d, de), F32), pltpu.VMEM((2, d, de), F32), pltpu.VMEM((2, de, d), F32),
                        pltpu.VMEM((d, de), BF16), pltpu.VMEM((d, de), BF16), pltpu.VMEM((de, d), BF16),
                        pltpu.SemaphoreType.DMA((2,))],
    )
    return pl.pallas_call(
        _expert_kernel,
        out_shape=jax.ShapeDtypeStruct((n_slots, d), F32),
        grid_spec=grid_spec,
        compiler_params=_params(("arbitrary",), 56),
        name="experts",
    )(block_expert, n_used, first, par, nxt, x_sorted, wg, wu, wd)


def _combine_kernel(dc_ref, dn_ref, y_hbm, gt_ref, h_ref, x1_ref, wg_ref, wu_ref, wd_ref, gf_ref, pn_ref,
                    o_ref, buf, sem, *, tm):
    i = pl.program_id(0)
    nsteps = pl.num_programs(0)
    slot = i % 2

    def issue(d_ref, s):
        def body(it, carry):
            for u in range(DMA_UNROLL):
                r = it * DMA_UNROLL + u
                for k in range(TOP_K):
                    _row_copy(y_hbm, d_ref[0, k * tm + r], buf.at[s, k, pl.ds(r, 1), :],
                              sem.at[s]).start(priority=(u * TOP_K + k) % 2)
            return carry
        lax.fori_loop(0, tm // DMA_UNROLL, body, 0)

    @pl.when(i == 0)
    def _():
        issue(dc_ref, 0)

    @pl.when(i + 1 < nsteps)
    def _():
        issue(dn_ref, 1 - slot)

    hb = h_ref[...]
    shared = _dot((_silu(_dot(hb, wg_ref[...])) * _dot(hb, wu_ref[...])).astype(BF16), wd_ref[...])

    for k in range(TOP_K):
        pltpu.make_async_copy(y_hbm.at[pl.ds(0, tm), :], buf.at[slot, k], sem.at[slot]).wait()

    gt = gt_ref[...]
    acc = shared
    for k in range(TOP_K):
        acc = acc + gt[:, k:k + 1] * buf[slot, k]
    o_ref[...] = x1_ref[...] + gf_ref[...] * (_rms(acc) * pn_ref[...])


def _combine(y_sorted, dest3, gate_t, h2b, x1, wgs, wus, wds, gate_f, pnf, seq, tm):
    t, d = x1.shape
    ds_ = wgs.shape[-1]
    per = seq // tm
    nt = t // tm
    smem = functools.partial(pl.BlockSpec, memory_space=pltpu.SMEM)
    row = pl.BlockSpec((tm, d), lambda i: (i, 0))
    return pl.pallas_call(
        functools.partial(_combine_kernel, tm=tm),
        out_shape=jax.ShapeDtypeStruct((t, d), F32),
        grid=(nt,),
        in_specs=[smem((None, 1, TOP_K * tm), lambda i: (i, 0, 0)),
                  smem((None, 1, TOP_K * tm), lambda i: (jnp.minimum(i + 1, nt - 1), 0, 0)),
                  pl.BlockSpec(memory_space=pl.ANY),
                  pl.BlockSpec((tm, TOP_K), lambda i: (i, 0)),
                  row, row,
                  pl.BlockSpec((d, ds_), lambda i: (0, 0)),
                  pl.BlockSpec((d, ds_), lambda i: (0, 0)),
                  pl.BlockSpec((ds_, d), lambda i: (0, 0)),
                  pl.BlockSpec((None, 1, d), lambda i: (i // per, 0, 0)),
                  pl.BlockSpec((1, d), lambda i: (0, 0))],
        out_specs=row,
        scratch_shapes=[pltpu.VMEM((2, TOP_K, tm, d), F32), pltpu.SemaphoreType.DMA((2,))],
        compiler_params=_params(("arbitrary",), 48),
        name="combine",
    )(dest3, dest3, y_sorted, gate_t, h2b, x1, wgs, wus, wds, gate_f, pnf.reshape(1, d))


def _layer(x, c, lb, w_ada, b_ada, pre_norm_mix, post_norm_mix, w_in, hgrn_norm, w_branch_attn, w_branch_hgrn,
           w_out, pre_norm_ffn, post_norm_ffn, w_router, router_bias, w_gate_e, w_up_e, w_down_e,
           w_gate_s, w_up_s, w_down_s):
    nb, s, d = x.shape
    t = nb * s
    x2 = x.reshape(t, d)

    mod = _ada(c, w_ada, b_ada).reshape(nb, 6, 1, d)
    shift_m, scale_m, gate_m, shift_f, scale_f, gate_f = (mod[:, k] for k in range(6))

    w_in_bf = w_in.astype(BF16)
    qkv = []
    for g, (_, dil) in enumerate(ATT_GROUPS):
        cols = jnp.concatenate([w_in_bf[:, p * ATT_WIDTH + g * ATT_GW:p * ATT_WIDTH + (g + 1) * ATT_GW]
                                for p in range(3)], axis=1)
        o = _inproj(x2, pre_norm_mix, shift_m, scale_m, cols, nb, s, dil, tn=3 * ATT_GW)
        qkv.append(o.reshape(nb, dil, s // dil, 3 * ATT_GW))
    proj2 = _inproj(x2, pre_norm_mix, shift_m, scale_m, w_in_bf[:, QKV_COLS:], nb, s)
    proj3 = proj2.reshape(nb, s, -1)

    att = [_attention_group(qkv[g], g) for g in range(len(ATT_GROUPS))]
    orec = _hgrn(proj3, lb, hgrn_norm)
    merged = _branch(att, orec, proj2, w_branch_attn.astype(BF16), w_branch_hgrn.astype(BF16), d, s)
    x1, h2f, h2b = _outproj(merged, w_out.astype(BF16), x2, gate_m, post_norm_mix, pre_norm_ffn,
                            shift_f, scale_f, s)

    eid, gate, pos, cnt = _router(h2b, w_router, router_bias)

    counts = cnt[:, 0]
    padded = (counts + MOE_BLOCK - 1) // MOE_BLOCK * MOE_BLOCK
    pad_end = jnp.cumsum(padded)
    pad_start = pad_end - padded
    n_blocks = -(-(t * TOP_K) // MOE_BLOCK) + N_EXPERTS
    onehot = eid[None] == jnp.arange(N_EXPERTS, dtype=jnp.int32)[:, None, None]
    dest = jnp.sum(jnp.where(onehot, pad_start[:, None, None], 0), axis=0) + pos
    block_start = jnp.arange(n_blocks, dtype=jnp.int32) * MOE_BLOCK
    block_expert = jnp.minimum(jnp.sum((block_start[:, None] >= pad_end[None, :]).astype(jnp.int32), axis=1),
                               N_EXPERTS - 1)
    n_used = (pad_end[-1:] // MOE_BLOCK).astype(jnp.int32)

    tm = _pick(s, (128,))
    nt = t // tm
    dest3 = dest.reshape(TOP_K, nt, tm).transpose(1, 0, 2).reshape(nt, 1, TOP_K * tm)
    znum = jnp.concatenate([padded - counts, n_used]).astype(jnp.int32)
    x_sorted = _dispatch(h2f, dest3, (pad_start + counts).astype(jnp.int32), znum, n_blocks * MOE_BLOCK, tm)
    eidx = jnp.arange(N_EXPERTS, dtype=jnp.int32)
    has = counts > 0
    rank = jnp.cumsum(has.astype(jnp.int32)) - 1
    later = (eidx[None, :] > eidx[:, None]) & has[None, :]
    nxt_e = jnp.min(jnp.where(later, eidx[None, :], N_EXPERTS), axis=1)
    nxt_e = jnp.where(nxt_e == N_EXPERTS, -1, nxt_e).astype(jnp.int32)
    oh_b = block_expert[:, None] == eidx[None, :]
    pick = lambda v: jnp.sum(jnp.where(oh_b, v[None, :], 0), axis=1).astype(jnp.int32)
    first = ((block_start == pick(pad_start)) & (block_start < pad_end[-1])).astype(jnp.int32)
    y_sorted = _experts(x_sorted, block_expert, n_used, first, pick(rank % 2), pick(nxt_e),
                        w_gate_e, w_up_e, w_down_e)
    out = _combine(y_sorted, dest3, gate.T, h2b, x1, w_gate_s.astype(BF16), w_up_s.astype(BF16),
                   w_down_s.astype(BF16), gate_f, post_norm_ffn, s, tm)
    return out.reshape(nb, s, d)


def kernel(x, c, w_ada, b_ada, pre_norm_mix, post_norm_mix, w_in, hgrn_lb_logits, hgrn_norm, w_branch_attn,
           w_branch_hgrn, w_out, pre_norm_ffn, post_norm_ffn, w_router, router_bias, w_gate_e, w_up_e, w_down_e,
           w_gate_s, w_up_s, w_down_s):
    lb_table = jnp.cumsum(jax.nn.softmax(hgrn_lb_logits.astype(F32), axis=0), axis=0)
    depth = w_ada.shape[0]
    for l in range(depth):
        x = _layer(x, c, lb_table[l], w_ada[l], b_ada[l], pre_norm_mix[l], post_norm_mix[l], w_in[l],
                   hgrn_norm[l], w_branch_attn[l], w_branch_hgrn[l], w_out[l], pre_norm_ffn[l],
                   post_norm_ffn[l], w_router[l], router_bias[l], w_gate_e[l], w_up_e[l], w_down_e[l],
                   w_gate_s[l], w_up_s[l], w_down_s[l])
    return x
```

```python
import functools

import jax
import jax.numpy as jnp
from jax import lax
from jax.experimental import pallas as pl
from jax.experimental.pallas import tpu as pltpu

F32 = jnp.float32
BF16 = jnp.bfloat16

HEAD_DIM = 128
ATT_GROUPS = ((128, 1), (512, 4), (2048, 16))
ATT_HPG = 4
ATT_HEADS = ATT_HPG * len(ATT_GROUPS)
ATT_WIDTH = ATT_HEADS * HEAD_DIM
ATT_GW = ATT_HPG * HEAD_DIM
ATT_BLOCK = 128
ATT_OUT_COLS = ATT_GW + HEAD_DIM
LSE_LANES = HEAD_DIM // ATT_HPG
HGRN_HEADS = 8
HGRN_D = 128
HGRN_WIDTH = HGRN_HEADS * HGRN_D
HGRN_CHUNK = 64
HGRN_DIRECT = 8
N_EXPERTS = 64
N_GROUPS = 8
TOPK_GROUPS = 4
TOP_K = 8
ROUTED_SCALE = 2.5
MOE_BLOCK = 256
NORM_EPS = 1e-6
V7X_VMEM_LIMIT = 56 * 1024 * 1024

QKV_COLS = 3 * ATT_WIDTH
RQ_OFF = 0
RF_OFF = RQ_OFF + HGRN_WIDTH
RI_OFF = RF_OFF + HGRN_WIDTH
RG_OFF = RI_OFF + HGRN_WIDTH
GATE_OFF = RG_OFF + HGRN_WIDTH


def _pick(n, cands):
    for c in cands:
        if n % c == 0:
            return c
    raise ValueError(f"no tile of {cands} divides {n}")


def _params(sem, vmem_mib=None):
    kw = dict(dimension_semantics=sem)
    if vmem_mib is not None:
        kw["vmem_limit_bytes"] = min(vmem_mib * 1024 * 1024, V7X_VMEM_LIMIT)
    return pltpu.CompilerParams(**kw)


def _dot(a, b):
    return jnp.dot(a, b, preferred_element_type=F32)


def _dot_nt(a, b):
    return lax.dot_general(a, b, (((1,), (1,)), ((), ())), preferred_element_type=F32)


def _dot_tn(a, b):
    return lax.dot_general(a, b, (((0,), (0,)), ((), ())), preferred_element_type=F32)


def _rms(y):
    return y * lax.rsqrt(jnp.mean(y * y, axis=-1, keepdims=True) + NORM_EPS)


def _silu(a):
    return a * jax.nn.sigmoid(a)


ADA_KCHUNK = 128


def _ada_kernel(ct_ref, w_ref, b_ref, o_ref, *, nb, d):
    a = _silu(ct_ref[...])
    tn = w_ref.shape[1]
    accs = [jnp.zeros((8, tn), F32) for _ in range(nb)]
    for kc in range(d // ADA_KCHUNK):
        wc = w_ref[kc * ADA_KCHUNK:(kc + 1) * ADA_KCHUNK, :]
        ac = a[kc * ADA_KCHUNK:(kc + 1) * ADA_KCHUNK, :]
        for b in range(nb):
            p = ac[:, b:b + 1] * wc
            accs[b] = accs[b] + p.reshape(ADA_KCHUNK // 8, 8, tn).sum(axis=0)
    rows = [jnp.sum(acc, axis=0, keepdims=True) for acc in accs]
    o_ref[...] = jnp.concatenate(rows, axis=0) + b_ref[...]


def _ada(c, w_ada, b_ada):
    nb, d = c.shape
    n = w_ada.shape[1]
    tn = _pick(n, (512, 256, 128))
    return pl.pallas_call(
        functools.partial(_ada_kernel, nb=nb, d=d),
        out_shape=jax.ShapeDtypeStruct((nb, n), F32),
        grid=(n // tn,),
        in_specs=[pl.BlockSpec((d, nb), lambda j: (0, 0)),
                  pl.BlockSpec((d, tn), lambda j: (0, j)),
                  pl.BlockSpec((1, tn), lambda j: (0, j))],
        out_specs=pl.BlockSpec((nb, tn), lambda j: (0, j)),
        compiler_params=_params(("arbitrary",), 32),
        name="ada_mod",
    )(c.T, w_ada, b_ada.reshape(1, n))


def _inproj_kernel(x_ref, g_ref, sh_ref, sc_ref, w_ref, o_ref, h_scr, *slab, dil):
    @pl.when(pl.program_id(1) == 0)
    def _():
        y = _rms(x_ref[...]) * g_ref[...]
        h_scr[...] = (y * (1.0 + sc_ref[...]) + sh_ref[...]).astype(BF16)

    o = _dot(h_scr[...], w_ref[...])
    if dil == 1:
        o_ref[...] = o.astype(o_ref.dtype)
    else:
        slab_ref, = slab
        tm, tn = o.shape
        for c in range(tn // 128):
            slab_ref[c] = o[:, c * 128:(c + 1) * 128]
        for r in range(dil):
            for c in range(tn // 128):
                o_ref[r, :, c * 128:(c + 1) * 128] = slab_ref[c, pl.ds(r, tm // dil, stride=dil), :].astype(o_ref.dtype)


def _inproj(x2, gain, shift, scale, w_bf, nb, seq, dil=1, tn=None):
    t, d = x2.shape
    n = w_bf.shape[1]
    tm = _pick(seq, (512, 256))
    tn = tn or _pick(n, (1024, 512, 256, 128))
    per = seq // tm
    if dil == 1:
        out_shape = jax.ShapeDtypeStruct((t, n), BF16)
        out_spec = pl.BlockSpec((tm, tn), lambda i, j: (i, j))
        scratch = [pltpu.VMEM((tm, d), BF16)]
    else:
        out_shape = jax.ShapeDtypeStruct((nb, dil, seq // dil, n), BF16)
        out_spec = pl.BlockSpec((None, dil, tm // dil, tn), lambda i, j: (i // per, 0, i % per, j))
        scratch = [pltpu.VMEM((tm, d), BF16), pltpu.VMEM((tn // 128, tm, 128), F32)]
    return pl.pallas_call(
        functools.partial(_inproj_kernel, dil=dil),
        out_shape=out_shape,
        grid=(t // tm, n // tn),
        in_specs=[pl.BlockSpec((tm, d), lambda i, j: (i, 0)),
                  pl.BlockSpec((1, d), lambda i, j: (0, 0)),
                  pl.BlockSpec((None, 1, d), lambda i, j: (i // per, 0, 0)),
                  pl.BlockSpec((None, 1, d), lambda i, j: (i // per, 0, 0)),
                  pl.BlockSpec((d, tn), lambda i, j: (0, j))],
        out_specs=out_spec,
        scratch_shapes=scratch,
        compiler_params=_params(("parallel", "arbitrary"), 48),
        name=f"in_proj_d{dil}_n{n}",
    )(x2, gain.reshape(1, d), shift, scale, w_bf)


def _attn_kernel(q_ref, kp_ref, kc_ref, vp_ref, vc_ref, o_ref, *, dil, slopes, scale, qb):
    n = pl.program_id(2)
    qi = lax.broadcasted_iota(jnp.int32, (ATT_BLOCK, ATT_BLOCK), 0)
    ki = lax.broadcasted_iota(jnp.int32, (ATT_BLOCK, ATT_BLOCK), 1)
    jc = qi - ki
    jp = jc + ATT_BLOCK
    first_lim = jnp.where(n > 0, ATT_BLOCK, -1)
    valid_c = jc >= 0
    lane = lax.broadcasted_iota(jnp.int32, (ATT_BLOCK, HEAD_DIM), 1)
    ones = jnp.ones((ATT_BLOCK, HEAD_DIM), BF16)
    for j in range(qb):
        rows = slice(j * ATT_BLOCK, (j + 1) * ATT_BLOCK)
        prow = slice((j - 1) * ATT_BLOCK, j * ATT_BLOCK)
        valid_p = jp <= (first_lim if j == 0 else ATT_BLOCK)
        lse_blk = jnp.zeros((ATT_BLOCK, HEAD_DIM), F32)
        for h in range(ATT_HPG):
            hs = slice(h * HEAD_DIM, (h + 1) * HEAD_DIM)
            q = q_ref[rows, hs]
            k_p = kp_ref[:, hs] if j == 0 else kc_ref[prow, hs]
            v_p = vp_ref[:, hs] if j == 0 else vc_ref[prow, hs]
            bias = slopes[h] * dil
            s_c = _dot_nt(q, kc_ref[rows, hs]) * scale - bias * jc.astype(F32)
            s_p = _dot_nt(q, k_p) * scale - bias * jp.astype(F32)
            s_c = jnp.where(valid_c, s_c, -jnp.inf)
            s_p = jnp.where(valid_p, s_p, -jnp.inf)
            m = jnp.maximum(jnp.max(s_c, axis=-1, keepdims=True), jnp.max(s_p, axis=-1, keepdims=True))
            p_c = jnp.exp(s_c - m).astype(BF16)
            p_p = jnp.exp(s_p - m).astype(BF16)
            oa = (_dot(p_c, jnp.concatenate([vc_ref[rows, hs], ones], axis=1))
                  + _dot(p_p, jnp.concatenate([v_p, ones], axis=1)))
            l = oa[:, HEAD_DIM:HEAD_DIM + 1]
            o_ref[rows, hs] = oa[:, :HEAD_DIM] / l
            lse = m + jnp.log(l)
            lse_blk = jnp.where((lane >= h * LSE_LANES) & (lane < (h + 1) * LSE_LANES), lse, lse_blk)
        o_ref[rows, ATT_GW:] = lse_blk


def _attention_group(qkv, g):
    nb_, dil, l, _ = qkv.shape
    window, dil_ = ATT_GROUPS[g]
    assert dil == dil_ and window // dil == ATT_BLOCK and l % ATT_BLOCK == 0
    qb = _pick(l // ATT_BLOCK, (4, 2, 1))
    slopes = tuple(2.0 ** (-8.0 * (g * ATT_HPG + i + 1) / ATT_HEADS) for i in range(ATT_HPG))

    def spec(part, prev):
        if prev:
            return pl.BlockSpec((None, None, ATT_BLOCK, ATT_GW),
                                lambda b, r, n: (b, r, jnp.maximum(n * qb - 1, 0), part))
        return pl.BlockSpec((None, None, qb * ATT_BLOCK, ATT_GW), lambda b, r, n: (b, r, n, part))

    return pl.pallas_call(
        functools.partial(_attn_kernel, dil=float(dil), slopes=slopes, scale=HEAD_DIM ** -0.5, qb=qb),
        out_shape=jax.ShapeDtypeStruct((nb_, dil, l, ATT_OUT_COLS), F32),
        grid=(nb_, dil, l // (qb * ATT_BLOCK)),
        in_specs=[spec(0, False), spec(1, True), spec(1, False), spec(2, True), spec(2, False)],
        out_specs=pl.BlockSpec((None, None, qb * ATT_BLOCK, ATT_OUT_COLS), lambda b, r, n: (b, r, n, 0)),
        compiler_params=_params(("parallel", "parallel", "arbitrary")),
        name=f"attn_g{g}",
    )(qkv, qkv, qkv, qkv, qkv)


def _hgrn_kernel(q_ref, f_ref, i_ref, g_ref, lb_ref, gn_ref, tri_ref, o_ref, st_ref, sh_ref, *, tb):
    @pl.when(pl.program_id(2) == 0)
    def _():
        st_ref[...] = jnp.zeros_like(st_ref)
        sh_ref[:, :, 0:HGRN_DIRECT, :] = jnp.zeros((3, tb // HGRN_DIRECT, HGRN_DIRECT, HGRN_D), F32)

    c_ = HGRN_CHUNK
    nc = tb // c_
    q = q_ref[...].astype(F32)
    v = i_ref[...].astype(F32)
    lb = lb_ref[...]
    f = lb + (1.0 - lb) * jax.nn.sigmoid(f_ref[...].astype(F32))
    kk = 1.0 - f
    lf = jnp.log(f)

    hi = lf.astype(BF16)
    r1 = lf - hi.astype(F32)
    mid = r1.astype(BF16)
    lo = (r1 - mid.astype(F32)).astype(BF16)
    tri = tri_ref[...]
    b = _dot(tri, hi) + _dot(tri, mid) + _dot(tri, lo)

    def ref_rows(blk, row):
        b3 = b.reshape(tb // blk, blk, HGRN_D)
        return jnp.broadcast_to(b3[:, row:row + 1, :], (tb // blk, blk, HGRN_D)).reshape(tb, HGRN_D)

    ti = lax.broadcasted_iota(jnp.int32, (c_, c_), 0)
    si = lax.broadcasted_iota(jnp.int32, (c_, c_), 1)
    ssum = jnp.zeros((nc, c_, c_), F32)
    half = c_ // 2
    while half >= HGRN_DIRECT:
        bref = ref_rows(2 * half, half)
        ql = (q * jnp.exp(jnp.minimum(b - bref, 0.0))).astype(BF16).reshape(nc, c_, HGRN_D)
        kl = (kk * jnp.exp(jnp.minimum(bref - b, 0.0))).astype(BF16).reshape(nc, c_, HGRN_D)
        s_l = jnp.einsum("ctk,csk->cts", ql, kl, preferred_element_type=F32)
        mask = ((ti // (2 * half)) == (si // (2 * half))) & (((ti // half) % 2) == 1) & (((si // half) % 2) == 0)
        ssum = ssum + jnp.where(mask[None], s_l, 0.0)
        half //= 2
    v3 = v.astype(BF16).reshape(nc, c_, HGRN_D)
    o_acc = jnp.einsum("cts,csv->ctv", ssum.astype(BF16), v3, preferred_element_type=F32).reshape(tb, HGRN_D)

    nt8 = tb // HGRN_DIRECT
    for plane, val in enumerate((kk, f, v)):
        sh_ref[plane, :, HGRN_DIRECT:, :] = val.reshape(nt8, HGRN_DIRECT, HGRN_D)

    def shifted(plane, lag):
        st = HGRN_DIRECT - lag
        return sh_ref[plane, :, st:st + HGRN_DIRECT, :].reshape(tb, HGRN_D)

    ones = jnp.ones((HGRN_D, HGRN_D), BF16)
    decay = None
    for lag in range(HGRN_DIRECT):
        if lag == 0:
            w = q * kk
        else:
            decay = f if lag == 1 else decay * shifted(1, lag - 1)
            w = q * shifted(0, lag) * decay
        o_acc = o_acc + _dot(w.astype(BF16), ones) * (v if lag == 0 else shifted(2, lag))

    blast = ref_rows(c_, c_ - 1)
    qe = (q * jnp.exp(b)).astype(BF16)
    kt = (kk * jnp.exp(blast - b)).astype(BF16)
    vb = v.astype(BF16)
    st_t = st_ref[...]
    inter = []
    for c in range(nc):
        rows = slice(c * c_, (c + 1) * c_)
        inter.append(_dot_nt(qe[rows], st_t.astype(BF16)))
        dec = jnp.exp(blast[c * c_:c * c_ + 1, :])
        st_t = st_t * dec + _dot_tn(vb[rows], kt[rows])
    st_ref[...] = st_t
    o = o_acc + jnp.concatenate(inter, axis=0)

    y = _rms(o) * gn_ref[...]
    o_ref[...] = (y * _silu(g_ref[...].astype(F32))).astype(o_ref.dtype)


def _hgrn(proj3, lb, gnorm):
    nb_, s, nc = proj3.shape
    tb = _pick(s, (512, 256, 128, 64))
    idx = jnp.arange(tb)
    tri = ((idx[:, None] // HGRN_CHUNK == idx[None, :] // HGRN_CHUNK) & (idx[None, :] <= idx[:, None])).astype(BF16)

    def spec(off):
        cb = off // HGRN_D
        return pl.BlockSpec((None, tb, HGRN_D), lambda b, h, n: (b, n, cb + h))

    vec = pl.BlockSpec((1, HGRN_D), lambda b, h, n: (0, h))
    out = pl.pallas_call(
        functools.partial(_hgrn_kernel, tb=tb),
        out_shape=jax.ShapeDtypeStruct((nb_, s, HGRN_WIDTH), BF16),
        grid=(nb_, HGRN_HEADS, s // tb),
        in_specs=[spec(RQ_OFF), spec(RF_OFF), spec(RI_OFF), spec(RG_OFF), vec, vec,
                  pl.BlockSpec((tb, tb), lambda b, h, n: (0, 0))],
        out_specs=pl.BlockSpec((None, tb, HGRN_D), lambda b, h, n: (b, n, h)),
        scratch_shapes=[pltpu.VMEM((HGRN_D, HGRN_D), F32),
                        pltpu.VMEM((3, tb // HGRN_DIRECT, 2 * HGRN_DIRECT, HGRN_D), F32)],
        compiler_params=_params(("parallel", "parallel", "arbitrary"), 32),
        name="hgrn2",
    )(proj3, proj3, proj3, proj3, lb.reshape(1, HGRN_WIDTH), gnorm.reshape(1, HGRN_WIDTH), tri)
    return out.reshape(nb_ * s, HGRN_WIDTH)


def _branch_kernel(a0_ref, a1_ref, a2_ref, r_ref, ga_ref, gb_ref, wa_ref, wr_ref, o_ref, att_scr, nat_scr,
                   *, dils):
    @pl.when(pl.program_id(1) == 0)
    def _():
        tm = att_scr.shape[0]
        nslab = ATT_OUT_COLS // 128
        for gi, (a_ref, dil) in enumerate(zip((a1_ref, a2_ref), dils)):
            for r in range(dil):
                for c in range(nslab):
                    nat_scr[gi, c, pl.ds(r, tm // dil, stride=dil), :] = a_ref[r, :, c * 128:(c + 1) * 128]

        def head(g, h):
            return a0_ref[:, h * HEAD_DIM:(h + 1) * HEAD_DIM] if g == 0 else nat_scr[g - 1, h]

        def lse(g, h):
            if g == 0:
                return a0_ref[:, ATT_GW + h * LSE_LANES:ATT_GW + h * LSE_LANES + 1]
            return nat_scr[g - 1, ATT_HPG, :, h * LSE_LANES:h * LSE_LANES + 1]

        for h in range(ATT_HPG):
            ls = [lse(g, h) for g in range(3)]
            m = jnp.maximum(jnp.maximum(ls[0], ls[1]), ls[2])
            es = [jnp.exp(l - m) for l in ls]
            den = es[0] + es[1] + es[2]
            o = ((es[0] / den) * head(0, h) + (es[1] / den) * head(1, h)) + (es[2] / den) * head(2, h)
            att_scr[:, h * HEAD_DIM:(h + 1) * HEAD_DIM] = o.astype(BF16)

    a = _dot(att_scr[...], wa_ref[...])
    r = _dot(r_ref[...], wr_ref[...])
    o_ref[...] = (jax.nn.sigmoid(ga_ref[...].astype(F32)) * a
                  + jax.nn.sigmoid(gb_ref[...].astype(F32)) * r).astype(o_ref.dtype)


def _branch(att, orec, proj2, wa_bf, wr_bf, d, seq):
    t = orec.shape[0]
    tm = _pick(seq, (512, 256))
    per = seq // tm
    gw = _pick(d, (512, 256, 128))
    assert GATE_OFF % gw == 0
    ga0 = GATE_OFF // gw
    gb0 = (GATE_OFF + d) // gw
    dils = tuple(dil for _, dil in ATT_GROUPS[1:])

    def aspec(dil):
        return pl.BlockSpec((None, dil, tm // dil, ATT_OUT_COLS), lambda i, j: (i // per, 0, i % per, 0))

    return pl.pallas_call(
        functools.partial(_branch_kernel, dils=dils),
        out_shape=jax.ShapeDtypeStruct((t, d), BF16),
        grid=(t // tm, d // gw),
        in_specs=[pl.BlockSpec((tm, ATT_OUT_COLS), lambda i, j: (i, 0)), aspec(dils[0]), aspec(dils[1]),
                  pl.BlockSpec((tm, HGRN_WIDTH), lambda i, j: (i, 0)),
                  pl.BlockSpec((tm, gw), lambda i, j: (i, ga0 + j)),
                  pl.BlockSpec((tm, gw), lambda i, j: (i, gb0 + j)),
                  pl.BlockSpec((ATT_GW, gw), lambda i, j: (0, j)),
                  pl.BlockSpec((HGRN_WIDTH, gw), lambda i, j: (0, j))],
        out_specs=pl.BlockSpec((tm, gw), lambda i, j: (i, j)),
        scratch_shapes=[pltpu.VMEM((tm, ATT_GW), BF16),
                        pltpu.VMEM((len(dils), ATT_OUT_COLS // 128, tm, 128), F32)],
        compiler_params=_params(("parallel", "arbitrary"), 32),
        name="branch_merge",
    )(att[0].reshape(t, ATT_OUT_COLS), att[1], att[2], orec, proj2, proj2, wa_bf, wr_bf)


def _rows_to_tiles(a):
    m, n = a.shape
    return pltpu.einshape("ctl->tcl", jnp.stack([a[:, c * 128:(c + 1) * 128] for c in range(n // 128)], axis=0))


def _tiles_to_rows(a):
    at = pltpu.einshape("tcl->ctl", a)
    return jnp.concatenate([at[c] for c in range(a.shape[1])], axis=1)


def _outproj_kernel(m_ref, w_ref, x_ref, gm_ref, pnm_ref, pnf_ref, sh_ref, sc_ref, x1_ref, ht_ref, hb_ref):
    y = _dot(m_ref[...], w_ref[...])
    x1 = x_ref[...] + gm_ref[...] * (_rms(y) * pnm_ref[...])
    x1_ref[...] = x1
    h2 = ((_rms(x1) * pnf_ref[...]) * (1.0 + sc_ref[...]) + sh_ref[...]).astype(BF16)
    hb_ref[...] = h2
    ht_ref[...] = _rows_to_tiles(h2)


def _outproj(merged, w_bf, x2, gate_m, pnm, pnf, shift_f, scale_f, seq):
    t, d = x2.shape
    tm = _pick(seq, (256, 128))
    per = seq // tm
    row = pl.BlockSpec((tm, d), lambda i: (i, 0))
    vec = pl.BlockSpec((1, d), lambda i: (0, 0))
    bvec = pl.BlockSpec((None, 1, d), lambda i: (i // per, 0, 0))
    return pl.pallas_call(
        _outproj_kernel,
        out_shape=(jax.ShapeDtypeStruct((t, d), F32), jax.ShapeDtypeStruct((t, d // 128, 128), BF16),
                   jax.ShapeDtypeStruct((t, d), BF16)),
        grid=(t // tm,),
        in_specs=[row, pl.BlockSpec((d, d), lambda i: (0, 0)), row, bvec, vec, vec, bvec, bvec],
        out_specs=(row, pl.BlockSpec((tm, d // 128, 128), lambda i: (i, 0, 0)), row),
        compiler_params=_params(("parallel",), 48),
        name="out_proj",
    )(merged, w_bf, x2, gate_m, pnm.reshape(1, d), pnf.reshape(1, d), shift_f, scale_f)


def _router_kernel(h_ref, w_ref, bias_ref, up_ref, eid_ref, gate_ref, pos_ref, cnt_ref, carry_ref, *, tr):
    @pl.when(pl.program_id(0) == 0)
    def _():
        carry_ref[...] = jnp.zeros_like(carry_ref)

    per_group = N_EXPERTS // N_GROUPS
    sig = jax.nn.sigmoid(_dot_nt(w_ref[...], h_ref[...]))
    choice = sig + bias_ref[...]
    eidx = lax.broadcasted_iota(jnp.int32, (N_EXPERTS, tr), 0)

    c3 = choice.reshape(N_GROUPS, per_group, tr)
    sub = lax.broadcasted_iota(jnp.int32, (N_GROUPS, per_group, tr), 1)
    m1 = jnp.max(c3, axis=1, keepdims=True)
    first = jnp.min(jnp.where(c3 == m1, sub, per_group), axis=1, keepdims=True)
    m2 = jnp.max(jnp.where(sub == first, -jnp.inf, c3), axis=1, keepdims=True)
    gs = (m1 + m2).reshape(N_GROUPS, tr)

    gidx = lax.broadcasted_iota(jnp.int32, (N_GROUPS, tr), 0)
    grank = jnp.zeros((N_GROUPS, tr), jnp.int32)
    for g in range(N_GROUPS):
        row = gs[g:g + 1, :]
        grank = grank + ((row > gs) | ((row == gs) & (gidx > g))).astype(jnp.int32)
    gsel = jnp.where(grank < TOPK_GROUPS, 1.0, 0.0)
    emask = jnp.broadcast_to(gsel.reshape(N_GROUPS, 1, tr), (N_GROUPS, per_group, tr)).reshape(N_EXPERTS, tr)
    cm = jnp.where(emask > 0.5, choice, -jnp.inf)

    rank = jnp.zeros((N_EXPERTS, tr), jnp.int32)
    for e in range(N_EXPERTS):
        row = cm[e:e + 1, :]
        rank = rank + ((row > cm) | ((row == cm) & (eidx > e))).astype(jnp.int32)
    sel = rank < TOP_K

    denom = jnp.sum(jnp.where(sel, sig, 0.0), axis=0, keepdims=True)
    gate_full = sig / denom * ROUTED_SCALE

    sel_b = jnp.where(sel, 1.0, 0.0).astype(BF16)
    carry = carry_ref[...]
    cum = _dot(sel_b, up_ref[...]) + jnp.concatenate([carry] * (tr // 128), axis=1)
    carry_new = carry + _dot(sel_b, jnp.ones((tr, 128), BF16))
    carry_ref[...] = carry_new
    cnt_ref[...] = carry_new.astype(jnp.int32)
    posi = cum.astype(jnp.int32)

    eids, gates, poss = [], [], []
    for r in range(TOP_K):
        hit = rank == r
        eids.append(jnp.sum(jnp.where(hit, eidx, 0), axis=0, keepdims=True))
        gates.append(jnp.sum(jnp.where(hit, gate_full, 0.0), axis=0, keepdims=True))
        poss.append(jnp.sum(jnp.where(hit, posi, 0), axis=0, keepdims=True))
    eid_ref[...] = jnp.concatenate(eids, axis=0)
    gate_ref[...] = jnp.concatenate(gates, axis=0)
    pos_ref[...] = jnp.concatenate(poss, axis=0)


def _router(h2b, w_router, router_bias):
    t, d = h2b.shape
    tr = _pick(t, (256, 128))
    idx = jnp.arange(tr)
    upper = (idx[:, None] < idx[None, :]).astype(BF16)
    kout = pl.BlockSpec((TOP_K, tr), lambda i: (0, i))
    return pl.pallas_call(
        functools.partial(_router_kernel, tr=tr),
        out_shape=(jax.ShapeDtypeStruct((TOP_K, t), jnp.int32), jax.ShapeDtypeStruct((TOP_K, t), F32),
                   jax.ShapeDtypeStruct((TOP_K, t), jnp.int32), jax.ShapeDtypeStruct((N_EXPERTS, 128), jnp.int32)),
        grid=(t // tr,),
        in_specs=[pl.BlockSpec((tr, d), lambda i: (i, 0)),
                  pl.BlockSpec((N_EXPERTS, d), lambda i: (0, 0)),
                  pl.BlockSpec((N_EXPERTS, 1), lambda i: (0, 0)),
                  pl.BlockSpec((tr, tr), lambda i: (0, 0))],
        out_specs=(kout, kout, kout, pl.BlockSpec((N_EXPERTS, 128), lambda i: (0, 0))),
        scratch_shapes=[pltpu.VMEM((N_EXPERTS, 128), F32)],
        compiler_params=_params(("arbitrary",), 32),
        name="router",
    )(h2b, w_router.T.astype(BF16), router_bias.reshape(N_EXPERTS, 1).astype(F32), upper)


def _row_copy(src_hbm, idx, dst, sem):
    return pltpu.make_async_copy(src_hbm.at[pl.ds(idx, 1), :], dst, sem)


DMA_UNROLL = 8


def _dispatch_kernel(zs_ref, zn_ref, d_ref, h_ref, xs_hbm, zero_scr, sem, zsem, *, tm, n_blocks):
    i = pl.program_id(0)

    def body(it, carry):
        for u in range(DMA_UNROLL):
            r = it * DMA_UNROLL + u
            for k in range(TOP_K):
                pltpu.make_async_copy(h_ref.at[pl.ds(r, 1)], xs_hbm.at[pl.ds(d_ref[0, k * tm + r], 1)],
                                      sem).start(priority=(u * TOP_K + k) % 2)
        return carry
    lax.fori_loop(0, tm // DMA_UNROLL, body, 0)

    @pl.when(i == 0)
    def _():
        zero_scr[...] = jnp.zeros_like(zero_scr)
        for e in range(N_EXPERTS):
            def zbody(u, carry, e=e):
                pltpu.make_async_copy(zero_scr.at[pl.ds(0, 1)], xs_hbm.at[pl.ds(zs_ref[e] + u, 1)], zsem).start()
                return carry
            lax.fori_loop(0, zn_ref[e], zbody, 0)
        for e in range(N_EXPERTS):
            def zwait(u, carry):
                pltpu.make_async_copy(zero_scr.at[pl.ds(0, 1)], xs_hbm.at[pl.ds(0, 1)], zsem).wait()
                return carry
            lax.fori_loop(0, zn_ref[e], zwait, 0)

        def tbody(blk, carry):
            pltpu.make_async_copy(zero_scr, xs_hbm.at[pl.ds(blk * MOE_BLOCK, MOE_BLOCK)], zsem).start()
            return carry
        lax.fori_loop(zn_ref[N_EXPERTS], n_blocks, tbody, 0)

        def twait(blk, carry):
            pltpu.make_async_copy(zero_scr, xs_hbm.at[pl.ds(0, MOE_BLOCK)], zsem).wait()
            return carry
        lax.fori_loop(zn_ref[N_EXPERTS], n_blocks, twait, 0)

    for k in range(TOP_K):
        pltpu.make_async_copy(h_ref, xs_hbm.at[pl.ds(0, tm)], sem).wait()


def _dispatch(h2t, dest3, zstart, znum, n_slots, tm):
    t, nch, _ = h2t.shape
    grid_spec = pltpu.PrefetchScalarGridSpec(
        num_scalar_prefetch=2,
        grid=(t // tm,),
        in_specs=[pl.BlockSpec((None, 1, TOP_K * tm), lambda i, zs, zn: (i, 0, 0), memory_space=pltpu.SMEM),
                  pl.BlockSpec((tm, nch, 128), lambda i, zs, zn: (i, 0, 0))],
        out_specs=pl.BlockSpec(memory_space=pl.ANY),
        scratch_shapes=[pltpu.VMEM((MOE_BLOCK, nch, 128), h2t.dtype), pltpu.SemaphoreType.DMA(()),
                        pltpu.SemaphoreType.DMA(())],
    )
    return pl.pallas_call(
        functools.partial(_dispatch_kernel, tm=tm, n_blocks=n_slots // MOE_BLOCK),
        out_shape=jax.ShapeDtypeStruct((n_slots, nch, 128), h2t.dtype),
        grid_spec=grid_spec,
        compiler_params=_params(("arbitrary",), 32),
        name="dispatch",
    )(zstart, znum, dest3, h2t)


def _expert_kernel(be_ref, nu_ref, first_ref, par_ref, nxt_ref, x_ref, wg_hbm, wu_hbm, wd_hbm, o_ref,
                   wg_f, wu_f, wd_f, wg_bf, wu_bf, wd_bf, sem):
    i = pl.program_id(0)

    def weight_copies(e, s):
        return [pltpu.make_async_copy(src.at[e], dst.at[s], sem.at[s])
                for src, dst in ((wg_hbm, wg_f), (wu_hbm, wu_f), (wd_hbm, wd_f))]

    @pl.when(i == 0)
    def _():
        for cp in weight_copies(be_ref[0], 0):
            cp.start()

    @pl.when(first_ref[i] == 1)
    def _():
        p = par_ref[i]
        for cp in weight_copies(be_ref[i], p):
            cp.wait()
        wg_bf[...] = wg_f[p].astype(BF16)
        wu_bf[...] = wu_f[p].astype(BF16)
        wd_bf[...] = wd_f[p].astype(BF16)

        @pl.when(nxt_ref[i] >= 0)
        def _():
            for cp in weight_copies(nxt_ref[i], 1 - p):
                cp.start()

    @pl.when(i < nu_ref[0])
    def _():
        xb = _tiles_to_rows(x_ref[...])
        hid = _silu(_dot(xb, wg_bf[...])) * _dot(xb, wu_bf[...])
        o_ref[...] = _dot(hid.astype(BF16), wd_bf[...])

    @pl.when(i >= nu_ref[0])
    def _():
        o_ref[...] = jnp.zeros_like(o_ref)


def _experts(x_sorted, block_expert, n_used, first, par, nxt, wg, wu, wd):
    n_slots, nch, _ = x_sorted.shape
    d = nch * 128
    n_blocks = block_expert.shape[0]
    de = wg.shape[-1]
    hbm = pl.BlockSpec(memory_space=pl.ANY)
    grid_spec = pltpu.PrefetchScalarGridSpec(
        num_scalar_prefetch=5,
        grid=(n_blocks,),
        in_specs=[pl.BlockSpec((MOE_BLOCK, nch, 128), lambda i, be, nu, *_: (jnp.minimum(i, nu[0] - 1), 0, 0)),
                  hbm, hbm, hbm],
        out_specs=pl.BlockSpec((MOE_BLOCK, d), lambda i, *_: (i, 0)),
        scratch_shapes=[pltpu.VMEM((2, d, de), F32), pltpu.VMEM((2, d, de), F32), pltpu.VMEM((2, de, d), F32),
                        pltpu.VMEM((d, de), BF16), pltpu.VMEM((d, de), BF16), pltpu.VMEM((de, d), BF16),
                        pltpu.SemaphoreType.DMA((2,))],
    )
    return pl.pallas_call(
        _expert_kernel,
        out_shape=jax.ShapeDtypeStruct((n_slots, d), F32),
        grid_spec=grid_spec,
        compiler_params=_params(("arbitrary",), 56),
        name="experts",
    )(block_expert, n_used, first, par, nxt, x_sorted, wg, wu, wd)


def _combine_kernel(dc_ref, dn_ref, y_hbm, gt_ref, h_ref, x1_ref, wg_ref, wu_ref, wd_ref, gf_ref, pn_ref,
                    o_ref, buf, sem, *, tm):
    i = pl.program_id(0)
    nsteps = pl.num_programs(0)
    slot = i % 2

    def issue(d_ref, s):
        def body(it, carry):
            for u in range(DMA_UNROLL):
                r = it * DMA_UNROLL + u
                for k in range(TOP_K):
                    _row_copy(y_hbm, d_ref[0, k * tm + r], buf.at[s, k, pl.ds(r, 1), :],
                              sem.at[s]).start(priority=(u * TOP_K + k) % 2)
            return carry
        lax.fori_loop(0, tm // DMA_UNROLL, body, 0)

    @pl.when(i == 0)
    def _():
        issue(dc_ref, 0)

    @pl.when(i + 1 < nsteps)
    def _():
        issue(dn_ref, 1 - slot)

    hb = h_ref[...]
    shared = _dot((_silu(_dot(hb, wg_ref[...])) * _dot(hb, wu_ref[...])).astype(BF16), wd_ref[...])

    for k in range(TOP_K):
        pltpu.make_async_copy(y_hbm.at[pl.ds(0, tm), :], buf.at[slot, k], sem.at[slot]).wait()

    gt = gt_ref[...]
    acc = shared
    for k in range(TOP_K):
        acc = acc + gt[:, k:k + 1] * buf[slot, k]
    o_ref[...] = x1_ref[...] + gf_ref[...] * (_rms(acc) * pn_ref[...])


def _combine(y_sorted, dest3, gate_t, h2b, x1, wgs, wus, wds, gate_f, pnf, seq, tm):
    t, d = x1.shape
    ds_ = wgs.shape[-1]
    per = seq // tm
    nt = t // tm
    smem = functools.partial(pl.BlockSpec, memory_space=pltpu.SMEM)
    row = pl.BlockSpec((tm, d), lambda i: (i, 0))
    return pl.pallas_call(
        functools.partial(_combine_kernel, tm=tm),
        out_shape=jax.ShapeDtypeStruct((t, d), F32),
        grid=(nt,),
        in_specs=[smem((None, 1, TOP_K * tm), lambda i: (i, 0, 0)),
                  smem((None, 1, TOP_K * tm), lambda i: (jnp.minimum(i + 1, nt - 1), 0, 0)),
                  pl.BlockSpec(memory_space=pl.ANY),
                  pl.BlockSpec((tm, TOP_K), lambda i: (i, 0)),
                  row, row,
                  pl.BlockSpec((d, ds_), lambda i: (0, 0)),
                  pl.BlockSpec((d, ds_), lambda i: (0, 0)),
                  pl.BlockSpec((ds_, d), lambda i: (0, 0)),
                  pl.BlockSpec((None, 1, d), lambda i: (i // per, 0, 0)),
                  pl.BlockSpec((1, d), lambda i: (0, 0))],
        out_specs=row,
        scratch_shapes=[pltpu.VMEM((2, TOP_K, tm, d), F32), pltpu.SemaphoreType.DMA((2,))],
        compiler_params=_params(("arbitrary",), 48),
        name="combine",
    )(dest3, dest3, y_sorted, gate_t, h2b, x1, wgs, wus, wds, gate_f, pnf.reshape(1, d))


def _layer(x, c, lb, w_ada, b_ada, pre_norm_mix, post_norm_mix, w_in, hgrn_norm, w_branch_attn, w_branch_hgrn,
           w_out, pre_norm_ffn, post_norm_ffn, w_router, router_bias, w_gate_e, w_up_e, w_down_e,
           w_gate_s, w_up_s, w_down_s):
    nb, s, d = x.shape
    t = nb * s
    x2 = x.reshape(t, d)

    mod = _ada(c, w_ada, b_ada).reshape(nb, 6, 1, d)
    shift_m, scale_m, gate_m, shift_f, scale_f, gate_f = (mod[:, k] for k in range(6))

    w_in_bf = w_in.astype(BF16)
    qkv = []
    for g, (_, dil) in enumerate(ATT_GROUPS):
        cols = jnp.concatenate([w_in_bf[:, p * ATT_WIDTH + g * ATT_GW:p * ATT_WIDTH + (g + 1) * ATT_GW]
                                for p in range(3)], axis=1)
        o = _inproj(x2, pre_norm_mix, shift_m, scale_m, cols, nb, s, dil, tn=3 * ATT_GW)
        qkv.append(o.reshape(nb, dil, s // dil, 3 * ATT_GW))
    proj2 = _inproj(x2, pre_norm_mix, shift_m, scale_m, w_in_bf[:, QKV_COLS:], nb, s)
    proj3 = proj2.reshape(nb, s, -1)

    att = [_attention_group(qkv[g], g) for g in range(len(ATT_GROUPS))]
    orec = _hgrn(proj3, lb, hgrn_norm)
    merged = _branch(att, orec, proj2, w_branch_attn.astype(BF16), w_branch_hgrn.astype(BF16), d, s)
    x1, h2t, h2b = _outproj(merged, w_out.astype(BF16), x2, gate_m, post_norm_mix, pre_norm_ffn,
                            shift_f, scale_f, s)

    eid, gate, pos, cnt = _router(h2b, w_router, router_bias)

    counts = cnt[:, 0]
    padded = (counts + MOE_BLOCK - 1) // MOE_BLOCK * MOE_BLOCK
    pad_end = jnp.cumsum(padded)
    pad_start = pad_end - padded
    n_blocks = -(-(t * TOP_K) // MOE_BLOCK) + N_EXPERTS
    onehot = eid[None] == jnp.arange(N_EXPERTS, dtype=jnp.int32)[:, None, None]
    dest = jnp.sum(jnp.where(onehot, pad_start[:, None, None], 0), axis=0) + pos
    block_start = jnp.arange(n_blocks, dtype=jnp.int32) * MOE_BLOCK
    block_expert = jnp.minimum(jnp.sum((block_start[:, None] >= pad_end[None, :]).astype(jnp.int32), axis=1),
                               N_EXPERTS - 1)
    n_used = (pad_end[-1:] // MOE_BLOCK).astype(jnp.int32)

    tm = _pick(s, (128,))
    nt = t // tm
    dest3 = dest.reshape(TOP_K, nt, tm).transpose(1, 0, 2).reshape(nt, 1, TOP_K * tm)
    znum = jnp.concatenate([padded - counts, n_used]).astype(jnp.int32)
    x_sorted = _dispatch(h2t, dest3, (pad_start + counts).astype(jnp.int32), znum, n_blocks * MOE_BLOCK, tm)
    eidx = jnp.arange(N_EXPERTS, dtype=jnp.int32)
    has = counts > 0
    rank = jnp.cumsum(has.astype(jnp.int32)) - 1
    later = (eidx[None, :] > eidx[:, None]) & has[None, :]
    nxt_e = jnp.min(jnp.where(later, eidx[None, :], N_EXPERTS), axis=1)
    nxt_e = jnp.where(nxt_e == N_EXPERTS, -1, nxt_e).astype(jnp.int32)
    oh_b = block_expert[:, None] == eidx[None, :]
    pick = lambda v: jnp.sum(jnp.where(oh_b, v[None, :], 0), axis=1).astype(jnp.int32)
    first = ((block_start == pick(pad_start)) & (block_start < pad_end[-1])).astype(jnp.int32)
    y_sorted = _experts(x_sorted, block_expert, n_used, first, pick(rank % 2), pick(nxt_e),
                        w_gate_e, w_up_e, w_down_e)
    out = _combine(y_sorted, dest3, gate.T, h2b, x1, w_gate_s.astype(BF16), w_up_s.astype(BF16),
                   w_down_s.astype(BF16), gate_f, post_norm_ffn, s, tm)
    return out.reshape(nb, s, d)


def kernel(x, c, w_ada, b_ada, pre_norm_mix, post_norm_mix, w_in, hgrn_lb_logits, hgrn_norm, w_branch_attn,
           w_branch_hgrn, w_out, pre_norm_ffn, post_norm_ffn, w_router, router_bias, w_gate_e, w_up_e, w_down_e,
           w_gate_s, w_up_s, w_down_s):
    lb_table = jnp.cumsum(jax.nn.softmax(hgrn_lb_logits.astype(F32), axis=0), axis=0)
    depth = w_ada.shape[0]
    for l in range(depth):
        x = _layer(x, c, lb_table[l], w_ada[l], b_ada[l], pre_norm_mix[l], post_norm_mix[l], w_in[l],
                   hgrn_norm[l], w_branch_attn[l], w_branch_hgrn[l], w_out[l], pre_norm_ffn[l],
                   post_norm_ffn[l], w_router[l], router_bias[l], w_gate_e[l], w_up_e[l], w_down_e[l],
                   w_gate_s[l], w_up_s[l], w_down_s[l])
    return x
```

```python
import functools

import jax
import jax.numpy as jnp
from jax import lax
from jax.experimental import pallas as pl
from jax.experimental.pallas import tpu as pltpu

F32 = jnp.float32
BF16 = jnp.bfloat16

HEAD_DIM = 128
ATT_GROUPS = ((128, 1), (512, 4), (2048, 16))
ATT_HPG = 4
ATT_HEADS = ATT_HPG * len(ATT_GROUPS)
ATT_WIDTH = ATT_HEADS * HEAD_DIM
ATT_GW = ATT_HPG * HEAD_DIM
ATT_BLOCK = 128
ATT_OUT_COLS = ATT_GW + HEAD_DIM
LSE_LANES = HEAD_DIM // ATT_HPG
HGRN_HEADS = 8
HGRN_D = 128
HGRN_WIDTH = HGRN_HEADS * HGRN_D
HGRN_CHUNK = 64
HGRN_DIRECT = 8
N_EXPERTS = 64
N_GROUPS = 8
TOPK_GROUPS = 4
TOP_K = 8
ROUTED_SCALE = 2.5
MOE_BLOCK = 256
NORM_EPS = 1e-6
V7X_VMEM_LIMIT = 56 * 1024 * 1024

QKV_COLS = 3 * ATT_WIDTH
RQ_OFF = 0
RF_OFF = RQ_OFF + HGRN_WIDTH
RI_OFF = RF_OFF + HGRN_WIDTH
RG_OFF = RI_OFF + HGRN_WIDTH
GATE_OFF = RG_OFF + HGRN_WIDTH


def _pick(n, cands):
    for c in cands:
        if n % c == 0:
            return c
    raise ValueError(f"no tile of {cands} divides {n}")


def _params(sem, vmem_mib=None):
    kw = dict(dimension_semantics=sem)
    if vmem_mib is not None:
        kw["vmem_limit_bytes"] = min(vmem_mib * 1024 * 1024, V7X_VMEM_LIMIT)
    return pltpu.CompilerParams(**kw)


def _dot(a, b):
    return jnp.dot(a, b, preferred_element_type=F32)


def _dot_nt(a, b):
    return lax.dot_general(a, b, (((1,), (1,)), ((), ())), preferred_element_type=F32)


def _dot_tn(a, b):
    return lax.dot_general(a, b, (((0,), (0,)), ((), ())), preferred_element_type=F32)


def _rms(y):
    return y * lax.rsqrt(jnp.mean(y * y, axis=-1, keepdims=True) + NORM_EPS)


def _silu(a):
    return a * jax.nn.sigmoid(a)


ADA_KCHUNK = 128


def _ada_kernel(ct_ref, w_ref, b_ref, o_ref, *, nb, d):
    a = _silu(ct_ref[...])
    tn = w_ref.shape[1]
    accs = [jnp.zeros((8, tn), F32) for _ in range(nb)]
    for kc in range(d // ADA_KCHUNK):
        wc = w_ref[kc * ADA_KCHUNK:(kc + 1) * ADA_KCHUNK, :]
        ac = a[kc * ADA_KCHUNK:(kc + 1) * ADA_KCHUNK, :]
        for b in range(nb):
            p = ac[:, b:b + 1] * wc
            accs[b] = accs[b] + p.reshape(ADA_KCHUNK // 8, 8, tn).sum(axis=0)
    rows = [jnp.sum(acc, axis=0, keepdims=True) for acc in accs]
    o_ref[...] = jnp.concatenate(rows, axis=0) + b_ref[...]


def _ada(c, w_ada, b_ada):
    nb, d = c.shape
    n = w_ada.shape[1]
    tn = _pick(n, (512, 256, 128))
    return pl.pallas_call(
        functools.partial(_ada_kernel, nb=nb, d=d),
        out_shape=jax.ShapeDtypeStruct((nb, n), F32),
        grid=(n // tn,),
        in_specs=[pl.BlockSpec((d, nb), lambda j: (0, 0)),
                  pl.BlockSpec((d, tn), lambda j: (0, j)),
                  pl.BlockSpec((1, tn), lambda j: (0, j))],
        out_specs=pl.BlockSpec((nb, tn), lambda j: (0, j)),
        compiler_params=_params(("arbitrary",), 32),
        name="ada_mod",
    )(c.T, w_ada, b_ada.reshape(1, n))


def _inproj_kernel(x_ref, g_ref, sh_ref, sc_ref, w_ref, o_ref, h_scr, *slab, dil):
    @pl.when(pl.program_id(1) == 0)
    def _():
        y = _rms(x_ref[...]) * g_ref[...]
        h_scr[...] = (y * (1.0 + sc_ref[...]) + sh_ref[...]).astype(BF16)

    o = _dot(h_scr[...], w_ref[...])
    if dil == 1:
        o_ref[...] = o.astype(o_ref.dtype)
    else:
        slab_ref, = slab
        tm, tn = o.shape
        for c in range(tn // 128):
            slab_ref[c] = o[:, c * 128:(c + 1) * 128]
        for r in range(dil):
            for c in range(tn // 128):
                o_ref[r, :, c * 128:(c + 1) * 128] = slab_ref[c, pl.ds(r, tm // dil, stride=dil), :].astype(o_ref.dtype)


def _inproj(x2, gain, shift, scale, w_bf, nb, seq, dil=1, tn=None):
    t, d = x2.shape
    n = w_bf.shape[1]
    tm = _pick(seq, (1024, 512, 256) if tn is None else (512, 256))
    tn = tn or _pick(n, (1024, 512, 256, 128))
    per = seq // tm
    if dil == 1:
        out_shape = jax.ShapeDtypeStruct((t, n), BF16)
        out_spec = pl.BlockSpec((tm, tn), lambda i, j: (i, j))
        scratch = [pltpu.VMEM((tm, d), BF16)]
    else:
        out_shape = jax.ShapeDtypeStruct((nb, dil, seq // dil, n), BF16)
        out_spec = pl.BlockSpec((None, dil, tm // dil, tn), lambda i, j: (i // per, 0, i % per, j))
        scratch = [pltpu.VMEM((tm, d), BF16), pltpu.VMEM((tn // 128, tm, 128), F32)]
    return pl.pallas_call(
        functools.partial(_inproj_kernel, dil=dil),
        out_shape=out_shape,
        grid=(t // tm, n // tn),
        in_specs=[pl.BlockSpec((tm, d), lambda i, j: (i, 0)),
                  pl.BlockSpec((1, d), lambda i, j: (0, 0)),
                  pl.BlockSpec((None, 1, d), lambda i, j: (i // per, 0, 0)),
                  pl.BlockSpec((None, 1, d), lambda i, j: (i // per, 0, 0)),
                  pl.BlockSpec((d, tn), lambda i, j: (0, j))],
        out_specs=out_spec,
        scratch_shapes=scratch,
        compiler_params=_params(("parallel", "arbitrary"), 48),
        name=f"in_proj_d{dil}_n{n}",
    )(x2, gain.reshape(1, d), shift, scale, w_bf)


def _attn_kernel(q_ref, kp_ref, kc_ref, vp_ref, vc_ref, o_ref, *, dil, slopes, scale, qb):
    n = pl.program_id(2)
    qi = lax.broadcasted_iota(jnp.int32, (ATT_BLOCK, ATT_BLOCK), 0)
    ki = lax.broadcasted_iota(jnp.int32, (ATT_BLOCK, ATT_BLOCK), 1)
    jc = qi - ki
    jp = jc + ATT_BLOCK
    first_lim = jnp.where(n > 0, ATT_BLOCK, -1)
    valid_c = jc >= 0
    lane = lax.broadcasted_iota(jnp.int32, (ATT_BLOCK, HEAD_DIM), 1)
    ones = jnp.ones((ATT_BLOCK, HEAD_DIM), BF16)
    for j in range(qb):
        rows = slice(j * ATT_BLOCK, (j + 1) * ATT_BLOCK)
        prow = slice((j - 1) * ATT_BLOCK, j * ATT_BLOCK)
        valid_p = jp <= (first_lim if j == 0 else ATT_BLOCK)
        lse_blk = jnp.zeros((ATT_BLOCK, HEAD_DIM), F32)
        for h in range(ATT_HPG):
            hs = slice(h * HEAD_DIM, (h + 1) * HEAD_DIM)
            q = q_ref[rows, hs]
            k_p = kp_ref[:, hs] if j == 0 else kc_ref[prow, hs]
            v_p = vp_ref[:, hs] if j == 0 else vc_ref[prow, hs]
            bias = slopes[h] * dil
            s_c = _dot_nt(q, kc_ref[rows, hs]) * scale - bias * jc.astype(F32)
            s_p = _dot_nt(q, k_p) * scale - bias * jp.astype(F32)
            s_c = jnp.where(valid_c, s_c, -jnp.inf)
            s_p = jnp.where(valid_p, s_p, -jnp.inf)
            m = jnp.maximum(jnp.max(s_c, axis=-1, keepdims=True), jnp.max(s_p, axis=-1, keepdims=True))
            p_c = jnp.exp(s_c - m).astype(BF16)
            p_p = jnp.exp(s_p - m).astype(BF16)
            oa = (_dot(p_c, jnp.concatenate([vc_ref[rows, hs], ones], axis=1))
                  + _dot(p_p, jnp.concatenate([v_p, ones], axis=1)))
            l = oa[:, HEAD_DIM:HEAD_DIM + 1]
            o_ref[rows, hs] = oa[:, :HEAD_DIM] / l
            lse = m + jnp.log(l)
            lse_blk = jnp.where((lane >= h * LSE_LANES) & (lane < (h + 1) * LSE_LANES), lse, lse_blk)
        o_ref[rows, ATT_GW:] = lse_blk


def _attention_group(qkv, g):
    nb_, dil, l, _ = qkv.shape
    window, dil_ = ATT_GROUPS[g]
    assert dil == dil_ and window // dil == ATT_BLOCK and l % ATT_BLOCK == 0
    qb = _pick(l // ATT_BLOCK, (4, 2, 1))
    slopes = tuple(2.0 ** (-8.0 * (g * ATT_HPG + i + 1) / ATT_HEADS) for i in range(ATT_HPG))

    def spec(part, prev):
        if prev:
            return pl.BlockSpec((None, None, ATT_BLOCK, ATT_GW),
                                lambda b, r, n: (b, r, jnp.maximum(n * qb - 1, 0), part))
        return pl.BlockSpec((None, None, qb * ATT_BLOCK, ATT_GW), lambda b, r, n: (b, r, n, part))

    return pl.pallas_call(
        functools.partial(_attn_kernel, dil=float(dil), slopes=slopes, scale=HEAD_DIM ** -0.5, qb=qb),
        out_shape=jax.ShapeDtypeStruct((nb_, dil, l, ATT_OUT_COLS), F32),
        grid=(nb_, dil, l // (qb * ATT_BLOCK)),
        in_specs=[spec(0, False), spec(1, True), spec(1, False), spec(2, True), spec(2, False)],
        out_specs=pl.BlockSpec((None, None, qb * ATT_BLOCK, ATT_OUT_COLS), lambda b, r, n: (b, r, n, 0)),
        compiler_params=_params(("parallel", "parallel", "arbitrary")),
        name=f"attn_g{g}",
    )(qkv, qkv, qkv, qkv, qkv)


def _hgrn_kernel(q_ref, f_ref, i_ref, g_ref, lb_ref, gn_ref, tri_ref, o_ref, st_ref, sh_ref, *, tb):
    @pl.when(pl.program_id(2) == 0)
    def _():
        st_ref[...] = jnp.zeros_like(st_ref)
        sh_ref[:, :, 0:HGRN_DIRECT, :] = jnp.zeros((3, tb // HGRN_DIRECT, HGRN_DIRECT, HGRN_D), F32)

    c_ = HGRN_CHUNK
    nc = tb // c_
    q = q_ref[...].astype(F32)
    v = i_ref[...].astype(F32)
    lb = lb_ref[...]
    f = lb + (1.0 - lb) * jax.nn.sigmoid(f_ref[...].astype(F32))
    kk = 1.0 - f
    lf = jnp.log(f)

    hi = lf.astype(BF16)
    r1 = lf - hi.astype(F32)
    mid = r1.astype(BF16)
    lo = (r1 - mid.astype(F32)).astype(BF16)
    tri = tri_ref[...]
    b = _dot(tri, hi) + _dot(tri, mid) + _dot(tri, lo)

    def ref_rows(blk, row):
        b3 = b.reshape(tb // blk, blk, HGRN_D)
        return jnp.broadcast_to(b3[:, row:row + 1, :], (tb // blk, blk, HGRN_D)).reshape(tb, HGRN_D)

    ti = lax.broadcasted_iota(jnp.int32, (c_, c_), 0)
    si = lax.broadcasted_iota(jnp.int32, (c_, c_), 1)
    ssum = jnp.zeros((nc, c_, c_), F32)
    half = c_ // 2
    while half >= HGRN_DIRECT:
        bref = ref_rows(2 * half, half)
        ql = (q * jnp.exp(jnp.minimum(b - bref, 0.0))).astype(BF16).reshape(nc, c_, HGRN_D)
        kl = (kk * jnp.exp(jnp.minimum(bref - b, 0.0))).astype(BF16).reshape(nc, c_, HGRN_D)
        s_l = jnp.einsum("ctk,csk->cts", ql, kl, preferred_element_type=F32)
        mask = ((ti // (2 * half)) == (si // (2 * half))) & (((ti // half) % 2) == 1) & (((si // half) % 2) == 0)
        ssum = ssum + jnp.where(mask[None], s_l, 0.0)
        half //= 2
    v3 = v.astype(BF16).reshape(nc, c_, HGRN_D)
    o_acc = jnp.einsum("cts,csv->ctv", ssum.astype(BF16), v3, preferred_element_type=F32).reshape(tb, HGRN_D)

    nt8 = tb // HGRN_DIRECT
    for plane, val in enumerate((kk, f, v)):
        sh_ref[plane, :, HGRN_DIRECT:, :] = val.reshape(nt8, HGRN_DIRECT, HGRN_D)

    def shifted(plane, lag):
        st = HGRN_DIRECT - lag
        return sh_ref[plane, :, st:st + HGRN_DIRECT, :].reshape(tb, HGRN_D)

    ones = jnp.ones((HGRN_D, HGRN_D), BF16)
    decay = None
    for lag in range(HGRN_DIRECT):
        if lag == 0:
            w = q * kk
        else:
            decay = f if lag == 1 else decay * shifted(1, lag - 1)
            w = q * shifted(0, lag) * decay
        o_acc = o_acc + _dot(w.astype(BF16), ones) * (v if lag == 0 else shifted(2, lag))

    blast = ref_rows(c_, c_ - 1)
    qe = (q * jnp.exp(b)).astype(BF16)
    kt = (kk * jnp.exp(blast - b)).astype(BF16)
    vb = v.astype(BF16)
    st_t = st_ref[...]
    inter = []
    for c in range(nc):
        rows = slice(c * c_, (c + 1) * c_)
        inter.append(_dot_nt(qe[rows], st_t.astype(BF16)))
        dec = jnp.exp(blast[c * c_:c * c_ + 1, :])
        st_t = st_t * dec + _dot_tn(vb[rows], kt[rows])
    st_ref[...] = st_t
    o = o_acc + jnp.concatenate(inter, axis=0)

    y = _rms(o) * gn_ref[...]
    o_ref[...] = (y * _silu(g_ref[...].astype(F32))).astype(o_ref.dtype)


def _hgrn(proj3, lb, gnorm):
    nb_, s, nc = proj3.shape
    tb = _pick(s, (512, 256, 128, 64))
    idx = jnp.arange(tb)
    tri = ((idx[:, None] // HGRN_CHUNK == idx[None, :] // HGRN_CHUNK) & (idx[None, :] <= idx[:, None])).astype(BF16)

    def spec(off):
        cb = off // HGRN_D
        return pl.BlockSpec((None, tb, HGRN_D), lambda b, h, n: (b, n, cb + h))

    vec = pl.BlockSpec((1, HGRN_D), lambda b, h, n: (0, h))
    out = pl.pallas_call(
        functools.partial(_hgrn_kernel, tb=tb),
        out_shape=jax.ShapeDtypeStruct((nb_, s, HGRN_WIDTH), BF16),
        grid=(nb_, HGRN_HEADS, s // tb),
        in_specs=[spec(RQ_OFF), spec(RF_OFF), spec(RI_OFF), spec(RG_OFF), vec, vec,
                  pl.BlockSpec((tb, tb), lambda b, h, n: (0, 0))],
        out_specs=pl.BlockSpec((None, tb, HGRN_D), lambda b, h, n: (b, n, h)),
        scratch_shapes=[pltpu.VMEM((HGRN_D, HGRN_D), F32),
                        pltpu.VMEM((3, tb // HGRN_DIRECT, 2 * HGRN_DIRECT, HGRN_D), F32)],
        compiler_params=_params(("parallel", "parallel", "arbitrary"), 32),
        name="hgrn2",
    )(proj3, proj3, proj3, proj3, lb.reshape(1, HGRN_WIDTH), gnorm.reshape(1, HGRN_WIDTH), tri)
    return out.reshape(nb_ * s, HGRN_WIDTH)


def _branch_kernel(a0_ref, a1_ref, a2_ref, r_ref, ga_ref, gb_ref, wa_ref, wr_ref, o_ref, att_scr, nat_scr,
                   *, dils):
    @pl.when(pl.program_id(1) == 0)
    def _():
        tm = att_scr.shape[0]
        nslab = ATT_OUT_COLS // 128
        for gi, (a_ref, dil) in enumerate(zip((a1_ref, a2_ref), dils)):
            for r in range(dil):
                for c in range(nslab):
                    nat_scr[gi, c, pl.ds(r, tm // dil, stride=dil), :] = a_ref[r, :, c * 128:(c + 1) * 128]

        def head(g, h):
            return a0_ref[:, h * HEAD_DIM:(h + 1) * HEAD_DIM] if g == 0 else nat_scr[g - 1, h]

        def lse(g, h):
            if g == 0:
                return a0_ref[:, ATT_GW + h * LSE_LANES:ATT_GW + h * LSE_LANES + 1]
            return nat_scr[g - 1, ATT_HPG, :, h * LSE_LANES:h * LSE_LANES + 1]

        for h in range(ATT_HPG):
            ls = [lse(g, h) for g in range(3)]
            m = jnp.maximum(jnp.maximum(ls[0], ls[1]), ls[2])
            es = [jnp.exp(l - m) for l in ls]
            den = es[0] + es[1] + es[2]
            o = ((es[0] / den) * head(0, h) + (es[1] / den) * head(1, h)) + (es[2] / den) * head(2, h)
            att_scr[:, h * HEAD_DIM:(h + 1) * HEAD_DIM] = o.astype(BF16)

    a = _dot(att_scr[...], wa_ref[...])
    r = _dot(r_ref[...], wr_ref[...])
    o_ref[...] = (jax.nn.sigmoid(ga_ref[...].astype(F32)) * a
                  + jax.nn.sigmoid(gb_ref[...].astype(F32)) * r).astype(o_ref.dtype)


def _branch(att, orec, proj2, wa_bf, wr_bf, d, seq):
    t = orec.shape[0]
    tm = _pick(seq, (512, 256))
    per = seq // tm
    gw = _pick(d, (512, 256, 128))
    assert GATE_OFF % gw == 0
    ga0 = GATE_OFF // gw
    gb0 = (GATE_OFF + d) // gw
    dils = tuple(dil for _, dil in ATT_GROUPS[1:])

    def aspec(dil):
        return pl.BlockSpec((None, dil, tm // dil, ATT_OUT_COLS), lambda i, j: (i // per, 0, i % per, 0))

    return pl.pallas_call(
        functools.partial(_branch_kernel, dils=dils),
        out_shape=jax.ShapeDtypeStruct((t, d), BF16),
        grid=(t // tm, d // gw),
        in_specs=[pl.BlockSpec((tm, ATT_OUT_COLS), lambda i, j: (i, 0)), aspec(dils[0]), aspec(dils[1]),
                  pl.BlockSpec((tm, HGRN_WIDTH), lambda i, j: (i, 0)),
                  pl.BlockSpec((tm, gw), lambda i, j: (i, ga0 + j)),
                  pl.BlockSpec((tm, gw), lambda i, j: (i, gb0 + j)),
                  pl.BlockSpec((ATT_GW, gw), lambda i, j: (0, j)),
                  pl.BlockSpec((HGRN_WIDTH, gw), lambda i, j: (0, j))],
        out_specs=pl.BlockSpec((tm, gw), lambda i, j: (i, j)),
        scratch_shapes=[pltpu.VMEM((tm, ATT_GW), BF16),
                        pltpu.VMEM((len(dils), ATT_OUT_COLS // 128, tm, 128), F32)],
        compiler_params=_params(("parallel", "arbitrary"), 32),
        name="branch_merge",
    )(att[0].reshape(t, ATT_OUT_COLS), att[1], att[2], orec, proj2, proj2, wa_bf, wr_bf)


def _rows_to_tiles(a):
    m, n = a.shape
    return jnp.swapaxes(jnp.stack([a[:, c * 128:(c + 1) * 128] for c in range(n // 128)], axis=0), 0, 1)


def _tiles_to_rows(a):
    at = jnp.swapaxes(a, 0, 1)
    return jnp.concatenate([at[c] for c in range(a.shape[1])], axis=1)


def _outproj_kernel(m_ref, w_ref, x_ref, gm_ref, pnm_ref, pnf_ref, sh_ref, sc_ref, x1_ref, ht_ref, hb_ref):
    y = _dot(m_ref[...], w_ref[...])
    x1 = x_ref[...] + gm_ref[...] * (_rms(y) * pnm_ref[...])
    x1_ref[...] = x1
    h2 = ((_rms(x1) * pnf_ref[...]) * (1.0 + sc_ref[...]) + sh_ref[...]).astype(BF16)
    hb_ref[...] = h2
    ht_ref[...] = _rows_to_tiles(h2)


def _outproj(merged, w_bf, x2, gate_m, pnm, pnf, shift_f, scale_f, seq):
    t, d = x2.shape
    tm = _pick(seq, (256, 128))
    per = seq // tm
    row = pl.BlockSpec((tm, d), lambda i: (i, 0))
    vec = pl.BlockSpec((1, d), lambda i: (0, 0))
    bvec = pl.BlockSpec((None, 1, d), lambda i: (i // per, 0, 0))
    return pl.pallas_call(
        _outproj_kernel,
        out_shape=(jax.ShapeDtypeStruct((t, d), F32), jax.ShapeDtypeStruct((t, d // 128, 128), BF16),
                   jax.ShapeDtypeStruct((t, d), BF16)),
        grid=(t // tm,),
        in_specs=[row, pl.BlockSpec((d, d), lambda i: (0, 0)), row, bvec, vec, vec, bvec, bvec],
        out_specs=(row, pl.BlockSpec((tm, d // 128, 128), lambda i: (i, 0, 0)), row),
        compiler_params=_params(("parallel",), 48),
        name="out_proj",
    )(merged, w_bf, x2, gate_m, pnm.reshape(1, d), pnf.reshape(1, d), shift_f, scale_f)


def _router_kernel(h_ref, w_ref, bias_ref, up_ref, eid_ref, gate_ref, pos_ref, cnt_ref, carry_ref, *, tr):
    @pl.when(pl.program_id(0) == 0)
    def _():
        carry_ref[...] = jnp.zeros_like(carry_ref)

    per_group = N_EXPERTS // N_GROUPS
    sig = jax.nn.sigmoid(_dot_nt(w_ref[...], h_ref[...]))
    choice = sig + bias_ref[...]
    eidx = lax.broadcasted_iota(jnp.int32, (N_EXPERTS, tr), 0)

    c3 = choice.reshape(N_GROUPS, per_group, tr)
    sub = lax.broadcasted_iota(jnp.int32, (N_GROUPS, per_group, tr), 1)
    m1 = jnp.max(c3, axis=1, keepdims=True)
    first = jnp.min(jnp.where(c3 == m1, sub, per_group), axis=1, keepdims=True)
    m2 = jnp.max(jnp.where(sub == first, -jnp.inf, c3), axis=1, keepdims=True)
    gs = (m1 + m2).reshape(N_GROUPS, tr)

    gidx = lax.broadcasted_iota(jnp.int32, (N_GROUPS, tr), 0)
    grank = jnp.zeros((N_GROUPS, tr), jnp.int32)
    for g in range(N_GROUPS):
        row = gs[g:g + 1, :]
        grank = grank + ((row > gs) | ((row == gs) & (gidx > g))).astype(jnp.int32)
    gsel = jnp.where(grank < TOPK_GROUPS, 1.0, 0.0)
    emask = jnp.broadcast_to(gsel.reshape(N_GROUPS, 1, tr), (N_GROUPS, per_group, tr)).reshape(N_EXPERTS, tr)
    cm = jnp.where(emask > 0.5, choice, -jnp.inf)

    rank = jnp.zeros((N_EXPERTS, tr), jnp.int32)
    for e in range(N_EXPERTS):
        row = cm[e:e + 1, :]
        rank = rank + ((row > cm) | ((row == cm) & (eidx > e))).astype(jnp.int32)
    sel = rank < TOP_K

    denom = jnp.sum(jnp.where(sel, sig, 0.0), axis=0, keepdims=True)
    gate_full = sig / denom * ROUTED_SCALE

    sel_b = jnp.where(sel, 1.0, 0.0).astype(BF16)
    carry = carry_ref[...]
    cum = _dot(sel_b, up_ref[...]) + jnp.concatenate([carry] * (tr // 128), axis=1)
    carry_new = carry + _dot(sel_b, jnp.ones((tr, 128), BF16))
    carry_ref[...] = carry_new
    cnt_ref[...] = carry_new.astype(jnp.int32)
    posi = cum.astype(jnp.int32)

    eids, gates, poss = [], [], []
    for r in range(TOP_K):
        hit = rank == r
        eids.append(jnp.sum(jnp.where(hit, eidx, 0), axis=0, keepdims=True))
        gates.append(jnp.sum(jnp.where(hit, gate_full, 0.0), axis=0, keepdims=True))
        poss.append(jnp.sum(jnp.where(hit, posi, 0), axis=0, keepdims=True))
    eid_ref[...] = jnp.concatenate(eids, axis=0)
    gate_ref[...] = jnp.concatenate(gates, axis=0)
    pos_ref[...] = jnp.concatenate(poss, axis=0)


def _router(h2b, w_router, router_bias):
    t, d = h2b.shape
    tr = _pick(t, (256, 128))
    idx = jnp.arange(tr)
    upper = (idx[:, None] < idx[None, :]).astype(BF16)
    kout = pl.BlockSpec((TOP_K, tr), lambda i: (0, i))
    return pl.pallas_call(
        functools.partial(_router_kernel, tr=tr),
        out_shape=(jax.ShapeDtypeStruct((TOP_K, t), jnp.int32), jax.ShapeDtypeStruct((TOP_K, t), F32),
                   jax.ShapeDtypeStruct((TOP_K, t), jnp.int32), jax.ShapeDtypeStruct((N_EXPERTS, 128), jnp.int32)),
        grid=(t // tr,),
        in_specs=[pl.BlockSpec((tr, d), lambda i: (i, 0)),
                  pl.BlockSpec((N_EXPERTS, d), lambda i: (0, 0)),
                  pl.BlockSpec((N_EXPERTS, 1), lambda i: (0, 0)),
                  pl.BlockSpec((tr, tr), lambda i: (0, 0))],
        out_specs=(kout, kout, kout, pl.BlockSpec((N_EXPERTS, 128), lambda i: (0, 0))),
        scratch_shapes=[pltpu.VMEM((N_EXPERTS, 128), F32)],
        compiler_params=_params(("arbitrary",), 32),
        name="router",
    )(h2b, w_router.T.astype(BF16), router_bias.reshape(N_EXPERTS, 1).astype(F32), upper)


def _row_copy(src_hbm, idx, dst, sem):
    return pltpu.make_async_copy(src_hbm.at[pl.ds(idx, 1), :], dst, sem)


def _dispatch_kernel(zs_ref, zn_ref, d_ref, h_ref, xs_hbm, zero_scr, sem, zsem, *, tm, n_blocks):
    i = pl.program_id(0)

    for r in range(tm):
        for k in range(TOP_K):
            pltpu.make_async_copy(h_ref.at[pl.ds(r, 1)], xs_hbm.at[pl.ds(d_ref[0, k * tm + r], 1)],
                                  sem).start(priority=(r + k) % 2)

    @pl.when(i == 0)
    def _():
        zero_scr[...] = jnp.zeros_like(zero_scr)
        for e in range(N_EXPERTS):
            def zbody(u, carry, e=e):
                pltpu.make_async_copy(zero_scr.at[pl.ds(0, 1)], xs_hbm.at[pl.ds(zs_ref[e] + u, 1)], zsem).start()
                return carry
            lax.fori_loop(0, zn_ref[e], zbody, 0)
        for e in range(N_EXPERTS):
            def zwait(u, carry):
                pltpu.make_async_copy(zero_scr.at[pl.ds(0, 1)], xs_hbm.at[pl.ds(0, 1)], zsem).wait()
                return carry
            lax.fori_loop(0, zn_ref[e], zwait, 0)

        def tbody(blk, carry):
            pltpu.make_async_copy(zero_scr, xs_hbm.at[pl.ds(blk * MOE_BLOCK, MOE_BLOCK)], zsem).start()
            return carry
        lax.fori_loop(zn_ref[N_EXPERTS], n_blocks, tbody, 0)

        def twait(blk, carry):
            pltpu.make_async_copy(zero_scr, xs_hbm.at[pl.ds(0, MOE_BLOCK)], zsem).wait()
            return carry
        lax.fori_loop(zn_ref[N_EXPERTS], n_blocks, twait, 0)

    for k in range(TOP_K):
        pltpu.make_async_copy(h_ref, xs_hbm.at[pl.ds(0, tm)], sem).wait()


def _dispatch(h2t, dest3, zstart, znum, n_slots, tm):
    t, nch, _ = h2t.shape
    grid_spec = pltpu.PrefetchScalarGridSpec(
        num_scalar_prefetch=2,
        grid=(t // tm,),
        in_specs=[pl.BlockSpec((None, 1, TOP_K * tm), lambda i, zs, zn: (i, 0, 0), memory_space=pltpu.SMEM),
                  pl.BlockSpec((tm, nch, 128), lambda i, zs, zn: (i, 0, 0))],
        out_specs=pl.BlockSpec(memory_space=pl.ANY),
        scratch_shapes=[pltpu.VMEM((MOE_BLOCK, nch, 128), h2t.dtype), pltpu.SemaphoreType.DMA(()),
                        pltpu.SemaphoreType.DMA(())],
    )
    return pl.pallas_call(
        functools.partial(_dispatch_kernel, tm=tm, n_blocks=n_slots // MOE_BLOCK),
        out_shape=jax.ShapeDtypeStruct((n_slots, nch, 128), h2t.dtype),
        grid_spec=grid_spec,
        compiler_params=_params(("arbitrary",), 32),
        name="dispatch",
    )(zstart, znum, dest3, h2t)


def _expert_kernel(be_ref, nu_ref, first_ref, par_ref, nxt_ref, x_ref, wg_hbm, wu_hbm, wd_hbm, o_ref,
                   wg_f, wu_f, wd_f, wg_bf, wu_bf, wd_bf, sem):
    i = pl.program_id(0)

    def weight_copies(e, s):
        return [pltpu.make_async_copy(src.at[e], dst.at[s], sem.at[s])
                for src, dst in ((wg_hbm, wg_f), (wu_hbm, wu_f), (wd_hbm, wd_f))]

    @pl.when(i == 0)
    def _():
        for cp in weight_copies(be_ref[0], 0):
            cp.start()

    @pl.when(first_ref[i] == 1)
    def _():
        p = par_ref[i]
        for cp in weight_copies(be_ref[i], p):
            cp.wait()
        wg_bf[...] = wg_f[p].astype(BF16)
        wu_bf[...] = wu_f[p].astype(BF16)
        wd_bf[...] = wd_f[p].astype(BF16)

        @pl.when(nxt_ref[i] >= 0)
        def _():
            for cp in weight_copies(nxt_ref[i], 1 - p):
                cp.start()

    @pl.when(i < nu_ref[0])
    def _():
        xb = _tiles_to_rows(x_ref[...])
        hid = _silu(_dot(xb, wg_bf[...])) * _dot(xb, wu_bf[...])
        o_ref[...] = _dot(hid.astype(BF16), wd_bf[...])

    @pl.when(i >= nu_ref[0])
    def _():
        o_ref[...] = jnp.zeros_like(o_ref)


def _experts(x_sorted, block_expert, n_used, first, par, nxt, wg, wu, wd):
    n_slots, nch, _ = x_sorted.shape
    d = nch * 128
    n_blocks = block_expert.shape[0]
    de = wg.shape[-1]
    hbm = pl.BlockSpec(memory_space=pl.ANY)
    grid_spec = pltpu.PrefetchScalarGridSpec(
        num_scalar_prefetch=5,
        grid=(n_blocks,),
        in_specs=[pl.BlockSpec((MOE_BLOCK, nch, 128), lambda i, be, nu, *_: (jnp.minimum(i, nu[0] - 1), 0, 0)),
                  hbm, hbm, hbm],
        out_specs=pl.BlockSpec((MOE_BLOCK, d), lambda i, *_: (i, 0)),
        scratch_shapes=[pltpu.VMEM((2, d, de), F32), pltpu.VMEM((2, d, de), F32), pltpu.VMEM((2, de, d), F32),
                        pltpu.VMEM((d, de), BF16), pltpu.VMEM((d, de), BF16), pltpu.VMEM((de, d), BF16),
                        pltpu.SemaphoreType.DMA((2,))],
    )
    return pl.pallas_call(
        _expert_kernel,
        out_shape=jax.ShapeDtypeStruct((n_slots, d), F32),
        grid_spec=grid_spec,
        compiler_params=_params(("arbitrary",), 56),
        name="experts",
    )(block_expert, n_used, first, par, nxt, x_sorted, wg, wu, wd)


def _combine_kernel(dc_ref, dn_ref, y_hbm, gt_ref, h_ref, x1_ref, wg_ref, wu_ref, wd_ref, gf_ref, pn_ref,
                    o_ref, buf, sem, *, tm):
    i = pl.program_id(0)
    nsteps = pl.num_programs(0)
    slot = i % 2

    def issue_tile(d_ref, s):
        for r in range(tm):
            for k in range(TOP_K):
                _row_copy(y_hbm, d_ref[0, k * tm + r], buf.at[s, k, pl.ds(r, 1), :],
                          sem.at[s]).start(priority=(r + k) % 2)

    @pl.when(i == 0)
    def _():
        issue_tile(dc_ref, 0)

    @pl.when(i + 1 < nsteps)
    def _():
        issue_tile(dn_ref, 1 - slot)

    hb = h_ref[...]
    shared = _dot((_silu(_dot(hb, wg_ref[...])) * _dot(hb, wu_ref[...])).astype(BF16), wd_ref[...])

    for k in range(TOP_K):
        pltpu.make_async_copy(y_hbm.at[pl.ds(0, tm), :], buf.at[slot, k], sem.at[slot]).wait()

    gt = gt_ref[...]
    acc = shared
    for k in range(TOP_K):
        acc = acc + gt[:, k:k + 1] * buf[slot, k]
    o_ref[...] = x1_ref[...] + gf_ref[...] * (_rms(acc) * pn_ref[...])


def _combine(y_sorted, dest3, gate_t, h2b, x1, wgs, wus, wds, gate_f, pnf, seq, tm):
    t, d = x1.shape
    ds_ = wgs.shape[-1]
    per = seq // tm
    nt = t // tm
    smem = functools.partial(pl.BlockSpec, memory_space=pltpu.SMEM)
    row = pl.BlockSpec((tm, d), lambda i: (i, 0))
    return pl.pallas_call(
        functools.partial(_combine_kernel, tm=tm),
        out_shape=jax.ShapeDtypeStruct((t, d), F32),
        grid=(nt,),
        in_specs=[smem((None, 1, TOP_K * tm), lambda i: (i, 0, 0)),
                  smem((None, 1, TOP_K * tm), lambda i: (jnp.minimum(i + 1, nt - 1), 0, 0)),
                  pl.BlockSpec(memory_space=pl.ANY),
                  pl.BlockSpec((tm, TOP_K), lambda i: (i, 0)),
                  row, row,
                  pl.BlockSpec((d, ds_), lambda i: (0, 0)),
                  pl.BlockSpec((d, ds_), lambda i: (0, 0)),
                  pl.BlockSpec((ds_, d), lambda i: (0, 0)),
                  pl.BlockSpec((None, 1, d), lambda i: (i // per, 0, 0)),
                  pl.BlockSpec((1, d), lambda i: (0, 0))],
        out_specs=row,
        scratch_shapes=[pltpu.VMEM((2, TOP_K, tm, d), F32), pltpu.SemaphoreType.DMA((2,))],
        compiler_params=_params(("arbitrary",), 48),
        name="combine",
    )(dest3, dest3, y_sorted, gate_t, h2b, x1, wgs, wus, wds, gate_f, pnf.reshape(1, d))


def _layer(x, c, lb, w_ada, b_ada, pre_norm_mix, post_norm_mix, w_in, hgrn_norm, w_branch_attn, w_branch_hgrn,
           w_out, pre_norm_ffn, post_norm_ffn, w_router, router_bias, w_gate_e, w_up_e, w_down_e,
           w_gate_s, w_up_s, w_down_s):
    nb, s, d = x.shape
    t = nb * s
    x2 = x.reshape(t, d)

    mod = _ada(c, w_ada, b_ada).reshape(nb, 6, 1, d)
    shift_m, scale_m, gate_m, shift_f, scale_f, gate_f = (mod[:, k] for k in range(6))

    w_in_bf = w_in.astype(BF16)
    qkv = []
    for g, (_, dil) in enumerate(ATT_GROUPS):
        cols = jnp.concatenate([w_in_bf[:, p * ATT_WIDTH + g * ATT_GW:p * ATT_WIDTH + (g + 1) * ATT_GW]
                                for p in range(3)], axis=1)
        o = _inproj(x2, pre_norm_mix, shift_m, scale_m, cols, nb, s, dil, tn=3 * ATT_GW)
        qkv.append(o.reshape(nb, dil, s // dil, 3 * ATT_GW))
    proj2 = _inproj(x2, pre_norm_mix, shift_m, scale_m, w_in_bf[:, QKV_COLS:], nb, s)
    proj3 = proj2.reshape(nb, s, -1)

    att = [_attention_group(qkv[g], g) for g in range(len(ATT_GROUPS))]
    orec = _hgrn(proj3, lb, hgrn_norm)
    merged = _branch(att, orec, proj2, w_branch_attn.astype(BF16), w_branch_hgrn.astype(BF16), d, s)
    x1, h2t, h2b = _outproj(merged, w_out.astype(BF16), x2, gate_m, post_norm_mix, pre_norm_ffn,
                            shift_f, scale_f, s)

    eid, gate, pos, cnt = _router(h2b, w_router, router_bias)

    counts = cnt[:, 0]
    padded = (counts + MOE_BLOCK - 1) // MOE_BLOCK * MOE_BLOCK
    pad_end = jnp.cumsum(padded)
    pad_start = pad_end - padded
    n_blocks = -(-(t * TOP_K) // MOE_BLOCK) + N_EXPERTS
    onehot = eid[None] == jnp.arange(N_EXPERTS, dtype=jnp.int32)[:, None, None]
    dest = jnp.sum(jnp.where(onehot, pad_start[:, None, None], 0), axis=0) + pos
    block_start = jnp.arange(n_blocks, dtype=jnp.int32) * MOE_BLOCK
    block_expert = jnp.minimum(jnp.sum((block_start[:, None] >= pad_end[None, :]).astype(jnp.int32), axis=1),
                               N_EXPERTS - 1)
    n_used = (pad_end[-1:] // MOE_BLOCK).astype(jnp.int32)

    tm = _pick(s, (128,))
    nt = t // tm
    dest3 = dest.reshape(TOP_K, nt, tm).transpose(1, 0, 2).reshape(nt, 1, TOP_K * tm)
    znum = jnp.concatenate([padded - counts, n_used]).astype(jnp.int32)
    x_sorted = _dispatch(h2t, dest3, (pad_start + counts).astype(jnp.int32), znum, n_blocks * MOE_BLOCK, tm)
    eidx = jnp.arange(N_EXPERTS, dtype=jnp.int32)
    has = counts > 0
    rank = jnp.cumsum(has.astype(jnp.int32)) - 1
    later = (eidx[None, :] > eidx[:, None]) & has[None, :]
    nxt_e = jnp.min(jnp.where(later, eidx[None, :], N_EXPERTS), axis=1)
    nxt_e = jnp.where(nxt_e == N_EXPERTS, -1, nxt_e).astype(jnp.int32)
    oh_b = block_expert[:, None] == eidx[None, :]
    pick = lambda v: jnp.sum(jnp.where(oh_b, v[None, :], 0), axis=1).astype(jnp.int32)
    first = ((block_start == pick(pad_start)) & (block_start < pad_end[-1])).astype(jnp.int32)
    y_sorted = _experts(x_sorted, block_expert, n_used, first, pick(rank % 2), pick(nxt_e),
                        w_gate_e, w_up_e, w_down_e)
    out = _combine(y_sorted, dest3, gate.T, h2b, x1, w_gate_s.astype(BF16), w_up_s.astype(BF16),
                   w_down_s.astype(BF16), gate_f, post_norm_ffn, s, tm)
    return out.reshape(nb, s, d)


def kernel(x, c, w_ada, b_ada, pre_norm_mix, post_norm_mix, w_in, hgrn_lb_logits, hgrn_norm, w_branch_attn,
           w_branch_hgrn, w_out, pre_norm_ffn, post_norm_ffn, w_router, router_bias, w_gate_e, w_up_e, w_down_e,
           w_gate_s, w_up_s, w_down_s):
    lb_table = jnp.cumsum(jax.nn.softmax(hgrn_lb_logits.astype(F32), axis=0), axis=0)
    depth = w_ada.shape[0]
    for l in range(depth):
        x = _layer(x, c, lb_table[l], w_ada[l], b_ada[l], pre_norm_mix[l], post_norm_mix[l], w_in[l],
                   hgrn_norm[l], w_branch_attn[l], w_branch_hgrn[l], w_out[l], pre_norm_ffn[l],
                   post_norm_ffn[l], w_router[l], router_bias[l], w_gate_e[l], w_up_e[l], w_down_e[l],
                   w_gate_s[l], w_up_s[l], w_down_s[l])
    return x
```

```python
import functools

import jax
import jax.numpy as jnp
from jax import lax
from jax.experimental import pallas as pl
from jax.experimental.pallas import tpu as pltpu

F32 = jnp.float32
BF16 = jnp.bfloat16

HEAD_DIM = 128
ATT_GROUPS = ((128, 1), (512, 4), (2048, 16))
ATT_HPG = 4
ATT_HEADS = ATT_HPG * len(ATT_GROUPS)
ATT_WIDTH = ATT_HEADS * HEAD_DIM
ATT_GW = ATT_HPG * HEAD_DIM
ATT_BLOCK = 128
ATT_OUT_COLS = ATT_GW + HEAD_DIM
LSE_LANES = HEAD_DIM // ATT_HPG
HGRN_HEADS = 8
HGRN_D = 128
HGRN_WIDTH = HGRN_HEADS * HGRN_D
HGRN_CHUNK = 64
HGRN_DIRECT = 8
N_EXPERTS = 64
N_GROUPS = 8
TOPK_GROUPS = 4
TOP_K = 8
ROUTED_SCALE = 2.5
MOE_BLOCK = 256
NORM_EPS = 1e-6
V7X_VMEM_LIMIT = 56 * 1024 * 1024

QKV_COLS = 3 * ATT_WIDTH
RQ_OFF = 0
RF_OFF = RQ_OFF + HGRN_WIDTH
RI_OFF = RF_OFF + HGRN_WIDTH
RG_OFF = RI_OFF + HGRN_WIDTH
GATE_OFF = RG_OFF + HGRN_WIDTH


def _pick(n, cands):
    for c in cands:
        if n % c == 0:
            return c
    raise ValueError(f"no tile of {cands} divides {n}")


def _params(sem, vmem_mib=None):
    kw = dict(dimension_semantics=sem)
    if vmem_mib is not None:
        kw["vmem_limit_bytes"] = min(vmem_mib * 1024 * 1024, V7X_VMEM_LIMIT)
    return pltpu.CompilerParams(**kw)


def _dot(a, b):
    return jnp.dot(a, b, preferred_element_type=F32)


def _dot_nt(a, b):
    return lax.dot_general(a, b, (((1,), (1,)), ((), ())), preferred_element_type=F32)


def _dot_tn(a, b):
    return lax.dot_general(a, b, (((0,), (0,)), ((), ())), preferred_element_type=F32)


def _rms(y):
    return y * lax.rsqrt(jnp.mean(y * y, axis=-1, keepdims=True) + NORM_EPS)


def _silu(a):
    return a * jax.nn.sigmoid(a)


ADA_KCHUNK = 128


def _ada_kernel(ct_ref, w_ref, b_ref, o_ref, *, nb, d):
    a = _silu(ct_ref[...])
    tn = w_ref.shape[1]
    accs = [jnp.zeros((8, tn), F32) for _ in range(nb)]
    for kc in range(d // ADA_KCHUNK):
        wc = w_ref[kc * ADA_KCHUNK:(kc + 1) * ADA_KCHUNK, :]
        ac = a[kc * ADA_KCHUNK:(kc + 1) * ADA_KCHUNK, :]
        for b in range(nb):
            p = ac[:, b:b + 1] * wc
            accs[b] = accs[b] + p.reshape(ADA_KCHUNK // 8, 8, tn).sum(axis=0)
    rows = [jnp.sum(acc, axis=0, keepdims=True) for acc in accs]
    o_ref[...] = jnp.concatenate(rows, axis=0) + b_ref[...]


def _ada(c, w_ada, b_ada):
    nb, d = c.shape
    n = w_ada.shape[1]
    tn = _pick(n, (512, 256, 128))
    return pl.pallas_call(
        functools.partial(_ada_kernel, nb=nb, d=d),
        out_shape=jax.ShapeDtypeStruct((nb, n), F32),
        grid=(n // tn,),
        in_specs=[pl.BlockSpec((d, nb), lambda j: (0, 0)),
                  pl.BlockSpec((d, tn), lambda j: (0, j)),
                  pl.BlockSpec((1, tn), lambda j: (0, j))],
        out_specs=pl.BlockSpec((nb, tn), lambda j: (0, j)),
        compiler_params=_params(("arbitrary",), 32),
        name="ada_mod",
    )(c.T, w_ada, b_ada.reshape(1, n))


def _inproj_kernel(x_ref, g_ref, sh_ref, sc_ref, w_ref, o_ref, h_scr, *slab, dil):
    @pl.when(pl.program_id(1) == 0)
    def _():
        y = _rms(x_ref[...]) * g_ref[...]
        h_scr[...] = (y * (1.0 + sc_ref[...]) + sh_ref[...]).astype(BF16)

    o = _dot(h_scr[...], w_ref[...])
    if dil == 1:
        o_ref[...] = o.astype(o_ref.dtype)
    else:
        slab_ref, = slab
        tm, tn = o.shape
        for c in range(tn // 128):
            slab_ref[c] = o[:, c * 128:(c + 1) * 128]
        for r in range(dil):
            for c in range(tn // 128):
                o_ref[r, :, c * 128:(c + 1) * 128] = slab_ref[c, pl.ds(r, tm // dil, stride=dil), :].astype(o_ref.dtype)


def _inproj(x2, gain, shift, scale, w_bf, nb, seq, dil=1, tn=None):
    t, d = x2.shape
    n = w_bf.shape[1]
    tm = _pick(seq, (1024, 512, 256) if tn is None else (512, 256))
    tn = tn or _pick(n, (1024, 512, 256, 128))
    per = seq // tm
    if dil == 1:
        out_shape = jax.ShapeDtypeStruct((t, n), BF16)
        out_spec = pl.BlockSpec((tm, tn), lambda i, j: (i, j))
        scratch = [pltpu.VMEM((tm, d), BF16)]
    else:
        out_shape = jax.ShapeDtypeStruct((nb, dil, seq // dil, n), BF16)
        out_spec = pl.BlockSpec((None, dil, tm // dil, tn), lambda i, j: (i // per, 0, i % per, j))
        scratch = [pltpu.VMEM((tm, d), BF16), pltpu.VMEM((tn // 128, tm, 128), F32)]
    return pl.pallas_call(
        functools.partial(_inproj_kernel, dil=dil),
        out_shape=out_shape,
        grid=(t // tm, n // tn),
        in_specs=[pl.BlockSpec((tm, d), lambda i, j: (i, 0)),
                  pl.BlockSpec((1, d), lambda i, j: (0, 0)),
                  pl.BlockSpec((None, 1, d), lambda i, j: (i // per, 0, 0)),
                  pl.BlockSpec((None, 1, d), lambda i, j: (i // per, 0, 0)),
                  pl.BlockSpec((d, tn), lambda i, j: (0, j))],
        out_specs=out_spec,
        scratch_shapes=scratch,
        compiler_params=_params(("parallel", "arbitrary"), 48),
        name=f"in_proj_d{dil}_n{n}",
    )(x2, gain.reshape(1, d), shift, scale, w_bf)


def _attn_kernel(q_ref, kp_ref, kc_ref, vp_ref, vc_ref, o_ref, *, dil, slopes, scale, qb):
    n = pl.program_id(2)
    qi = lax.broadcasted_iota(jnp.int32, (ATT_BLOCK, ATT_BLOCK), 0)
    ki = lax.broadcasted_iota(jnp.int32, (ATT_BLOCK, ATT_BLOCK), 1)
    jc = qi - ki
    jp = jc + ATT_BLOCK
    first_lim = jnp.where(n > 0, ATT_BLOCK, -1)
    valid_c = jc >= 0
    lane = lax.broadcasted_iota(jnp.int32, (ATT_BLOCK, HEAD_DIM), 1)
    ones = jnp.ones((ATT_BLOCK, HEAD_DIM), BF16)
    for j in range(qb):
        rows = slice(j * ATT_BLOCK, (j + 1) * ATT_BLOCK)
        prow = slice((j - 1) * ATT_BLOCK, j * ATT_BLOCK)
        valid_p = jp <= (first_lim if j == 0 else ATT_BLOCK)
        lse_blk = jnp.zeros((ATT_BLOCK, HEAD_DIM), F32)
        for h in range(ATT_HPG):
            hs = slice(h * HEAD_DIM, (h + 1) * HEAD_DIM)
            q = q_ref[rows, hs]
            k_p = kp_ref[:, hs] if j == 0 else kc_ref[prow, hs]
            v_p = vp_ref[:, hs] if j == 0 else vc_ref[prow, hs]
            bias = slopes[h] * dil
            s_c = _dot_nt(q, kc_ref[rows, hs]) * scale - bias * jc.astype(F32)
            s_p = _dot_nt(q, k_p) * scale - bias * jp.astype(F32)
            s_c = jnp.where(valid_c, s_c, -jnp.inf)
            s_p = jnp.where(valid_p, s_p, -jnp.inf)
            m = jnp.maximum(jnp.max(s_c, axis=-1, keepdims=True), jnp.max(s_p, axis=-1, keepdims=True))
            p_c = jnp.exp(s_c - m).astype(BF16)
            p_p = jnp.exp(s_p - m).astype(BF16)
            oa = (_dot(p_c, jnp.concatenate([vc_ref[rows, hs], ones], axis=1))
                  + _dot(p_p, jnp.concatenate([v_p, ones], axis=1)))
            l = oa[:, HEAD_DIM:HEAD_DIM + 1]
            o_ref[rows, hs] = oa[:, :HEAD_DIM] / l
            lse = m + jnp.log(l)
            lse_blk = jnp.where((lane >= h * LSE_LANES) & (lane < (h + 1) * LSE_LANES), lse, lse_blk)
        o_ref[rows, ATT_GW:] = lse_blk


def _attention_group(qkv, g):
    nb_, dil, l, _ = qkv.shape
    window, dil_ = ATT_GROUPS[g]
    assert dil == dil_ and window // dil == ATT_BLOCK and l % ATT_BLOCK == 0
    qb = _pick(l // ATT_BLOCK, (4, 2, 1))
    slopes = tuple(2.0 ** (-8.0 * (g * ATT_HPG + i + 1) / ATT_HEADS) for i in range(ATT_HPG))

    def spec(part, prev):
        if prev:
            return pl.BlockSpec((None, None, ATT_BLOCK, ATT_GW),
                                lambda b, r, n: (b, r, jnp.maximum(n * qb - 1, 0), part))
        return pl.BlockSpec((None, None, qb * ATT_BLOCK, ATT_GW), lambda b, r, n: (b, r, n, part))

    return pl.pallas_call(
        functools.partial(_attn_kernel, dil=float(dil), slopes=slopes, scale=HEAD_DIM ** -0.5, qb=qb),
        out_shape=jax.ShapeDtypeStruct((nb_, dil, l, ATT_OUT_COLS), F32),
        grid=(nb_, dil, l // (qb * ATT_BLOCK)),
        in_specs=[spec(0, False), spec(1, True), spec(1, False), spec(2, True), spec(2, False)],
        out_specs=pl.BlockSpec((None, None, qb * ATT_BLOCK, ATT_OUT_COLS), lambda b, r, n: (b, r, n, 0)),
        compiler_params=_params(("parallel", "parallel", "arbitrary")),
        name=f"attn_g{g}",
    )(qkv, qkv, qkv, qkv, qkv)


def _hgrn_kernel(q_ref, f_ref, i_ref, g_ref, lb_ref, gn_ref, tri_ref, o_ref, st_ref, sh_ref, *, tb):
    @pl.when(pl.program_id(2) == 0)
    def _():
        st_ref[...] = jnp.zeros_like(st_ref)
        sh_ref[:, :, 0:HGRN_DIRECT, :] = jnp.zeros((3, tb // HGRN_DIRECT, HGRN_DIRECT, HGRN_D), F32)

    c_ = HGRN_CHUNK
    nc = tb // c_
    q = q_ref[...].astype(F32)
    v = i_ref[...].astype(F32)
    lb = lb_ref[...]
    f = lb + (1.0 - lb) * jax.nn.sigmoid(f_ref[...].astype(F32))
    kk = 1.0 - f
    lf = jnp.log(f)

    hi = lf.astype(BF16)
    r1 = lf - hi.astype(F32)
    mid = r1.astype(BF16)
    lo = (r1 - mid.astype(F32)).astype(BF16)
    tri = tri_ref[...]
    b = _dot(tri, hi) + _dot(tri, mid) + _dot(tri, lo)

    def ref_rows(blk, row):
        b3 = b.reshape(tb // blk, blk, HGRN_D)
        return jnp.broadcast_to(b3[:, row:row + 1, :], (tb // blk, blk, HGRN_D)).reshape(tb, HGRN_D)

    ti = lax.broadcasted_iota(jnp.int32, (c_, c_), 0)
    si = lax.broadcasted_iota(jnp.int32, (c_, c_), 1)
    ssum = jnp.zeros((nc, c_, c_), F32)
    half = c_ // 2
    while half >= HGRN_DIRECT:
        bref = ref_rows(2 * half, half)
        ql = (q * jnp.exp(jnp.minimum(b - bref, 0.0))).astype(BF16).reshape(nc, c_, HGRN_D)
        kl = (kk * jnp.exp(jnp.minimum(bref - b, 0.0))).astype(BF16).reshape(nc, c_, HGRN_D)
        s_l = jnp.einsum("ctk,csk->cts", ql, kl, preferred_element_type=F32)
        mask = ((ti // (2 * half)) == (si // (2 * half))) & (((ti // half) % 2) == 1) & (((si // half) % 2) == 0)
        ssum = ssum + jnp.where(mask[None], s_l, 0.0)
        half //= 2
    v3 = v.astype(BF16).reshape(nc, c_, HGRN_D)
    o_acc = jnp.einsum("cts,csv->ctv", ssum.astype(BF16), v3, preferred_element_type=F32).reshape(tb, HGRN_D)

    nt8 = tb // HGRN_DIRECT
    for plane, val in enumerate((kk, f, v)):
        sh_ref[plane, :, HGRN_DIRECT:, :] = val.reshape(nt8, HGRN_DIRECT, HGRN_D)

    def shifted(plane, lag):
        st = HGRN_DIRECT - lag
        return sh_ref[plane, :, st:st + HGRN_DIRECT, :].reshape(tb, HGRN_D)

    ones = jnp.ones((HGRN_D, HGRN_D), BF16)
    decay = None
    for lag in range(HGRN_DIRECT):
        if lag == 0:
            w = q * kk
        else:
            decay = f if lag == 1 else decay * shifted(1, lag - 1)
            w = q * shifted(0, lag) * decay
        o_acc = o_acc + _dot(w.astype(BF16), ones) * (v if lag == 0 else shifted(2, lag))

    blast = ref_rows(c_, c_ - 1)
    qe = (q * jnp.exp(b)).astype(BF16)
    kt = (kk * jnp.exp(blast - b)).astype(BF16)
    vb = v.astype(BF16)
    st_t = st_ref[...]
    inter = []
    for c in range(nc):
        rows = slice(c * c_, (c + 1) * c_)
        inter.append(_dot_nt(qe[rows], st_t.astype(BF16)))
        dec = jnp.exp(blast[c * c_:c * c_ + 1, :])
        st_t = st_t * dec + _dot_tn(vb[rows], kt[rows])
    st_ref[...] = st_t
    o = o_acc + jnp.concatenate(inter, axis=0)

    y = _rms(o) * gn_ref[...]
    o_ref[...] = (y * _silu(g_ref[...].astype(F32))).astype(o_ref.dtype)


def _hgrn(proj3, lb, gnorm):
    nb_, s, nc = proj3.shape
    tb = _pick(s, (512, 256, 128, 64))
    idx = jnp.arange(tb)
    tri = ((idx[:, None] // HGRN_CHUNK == idx[None, :] // HGRN_CHUNK) & (idx[None, :] <= idx[:, None])).astype(BF16)

    def spec(off):
        cb = off // HGRN_D
        return pl.BlockSpec((None, tb, HGRN_D), lambda b, h, n: (b, n, cb + h))

    vec = pl.BlockSpec((1, HGRN_D), lambda b, h, n: (0, h))
    out = pl.pallas_call(
        functools.partial(_hgrn_kernel, tb=tb),
        out_shape=jax.ShapeDtypeStruct((nb_, s, HGRN_WIDTH), BF16),
        grid=(nb_, HGRN_HEADS, s // tb),
        in_specs=[spec(RQ_OFF), spec(RF_OFF), spec(RI_OFF), spec(RG_OFF), vec, vec,
                  pl.BlockSpec((tb, tb), lambda b, h, n: (0, 0))],
        out_specs=pl.BlockSpec((None, tb, HGRN_D), lambda b, h, n: (b, n, h)),
        scratch_shapes=[pltpu.VMEM((HGRN_D, HGRN_D), F32),
                        pltpu.VMEM((3, tb // HGRN_DIRECT, 2 * HGRN_DIRECT, HGRN_D), F32)],
        compiler_params=_params(("parallel", "parallel", "arbitrary"), 32),
        name="hgrn2",
    )(proj3, proj3, proj3, proj3, lb.reshape(1, HGRN_WIDTH), gnorm.reshape(1, HGRN_WIDTH), tri)
    return out.reshape(nb_ * s, HGRN_WIDTH)


def _branch_kernel(a0_ref, a1_ref, a2_ref, r_ref, ga_ref, gb_ref, wa_ref, wr_ref, o_ref, att_scr, nat_scr,
                   *, dils):
    @pl.when(pl.program_id(1) == 0)
    def _():
        tm = att_scr.shape[0]
        nslab = ATT_OUT_COLS // 128
        for gi, (a_ref, dil) in enumerate(zip((a1_ref, a2_ref), dils)):
            for r in range(dil):
                for c in range(nslab):
                    nat_scr[gi, c, pl.ds(r, tm // dil, stride=dil), :] = a_ref[r, :, c * 128:(c + 1) * 128]

        def head(g, h):
            return a0_ref[:, h * HEAD_DIM:(h + 1) * HEAD_DIM] if g == 0 else nat_scr[g - 1, h]

        def lse(g, h):
            if g == 0:
                return a0_ref[:, ATT_GW + h * LSE_LANES:ATT_GW + h * LSE_LANES + 1]
            return nat_scr[g - 1, ATT_HPG, :, h * LSE_LANES:h * LSE_LANES + 1]

        for h in range(ATT_HPG):
            ls = [lse(g, h) for g in range(3)]
            m = jnp.maximum(jnp.maximum(ls[0], ls[1]), ls[2])
            es = [jnp.exp(l - m) for l in ls]
            den = es[0] + es[1] + es[2]
            o = ((es[0] / den) * head(0, h) + (es[1] / den) * head(1, h)) + (es[2] / den) * head(2, h)
            att_scr[:, h * HEAD_DIM:(h + 1) * HEAD_DIM] = o.astype(BF16)

    a = _dot(att_scr[...], wa_ref[...])
    r = _dot(r_ref[...], wr_ref[...])
    o_ref[...] = (jax.nn.sigmoid(ga_ref[...].astype(F32)) * a
                  + jax.nn.sigmoid(gb_ref[...].astype(F32)) * r).astype(o_ref.dtype)


def _branch(att, orec, proj2, wa_bf, wr_bf, d, seq):
    t = orec.shape[0]
    tm = _pick(seq, (512, 256))
    per = seq // tm
    gw = _pick(d, (512, 256, 128))
    assert GATE_OFF % gw == 0
    ga0 = GATE_OFF // gw
    gb0 = (GATE_OFF + d) // gw
    dils = tuple(dil for _, dil in ATT_GROUPS[1:])

    def aspec(dil):
        return pl.BlockSpec((None, dil, tm // dil, ATT_OUT_COLS), lambda i, j: (i // per, 0, i % per, 0))

    return pl.pallas_call(
        functools.partial(_branch_kernel, dils=dils),
        out_shape=jax.ShapeDtypeStruct((t, d), BF16),
        grid=(t // tm, d // gw),
        in_specs=[pl.BlockSpec((tm, ATT_OUT_COLS), lambda i, j: (i, 0)), aspec(dils[0]), aspec(dils[1]),
                  pl.BlockSpec((tm, HGRN_WIDTH), lambda i, j: (i, 0)),
                  pl.BlockSpec((tm, gw), lambda i, j: (i, ga0 + j)),
                  pl.BlockSpec((tm, gw), lambda i, j: (i, gb0 + j)),
                  pl.BlockSpec((ATT_GW, gw), lambda i, j: (0, j)),
                  pl.BlockSpec((HGRN_WIDTH, gw), lambda i, j: (0, j))],
        out_specs=pl.BlockSpec((tm, gw), lambda i, j: (i, j)),
        scratch_shapes=[pltpu.VMEM((tm, ATT_GW), BF16),
                        pltpu.VMEM((len(dils), ATT_OUT_COLS // 128, tm, 128), F32)],
        compiler_params=_params(("parallel", "arbitrary"), 32),
        name="branch_merge",
    )(att[0].reshape(t, ATT_OUT_COLS), att[1], att[2], orec, proj2, proj2, wa_bf, wr_bf)


def _rows_to_tiles(a):
    m, n = a.shape
    return jnp.swapaxes(jnp.stack([a[:, c * 128:(c + 1) * 128] for c in range(n // 128)], axis=0), 0, 1)


def _tiles_to_rows(a):
    at = jnp.swapaxes(a, 0, 1)
    return jnp.concatenate([at[c] for c in range(a.shape[1])], axis=1)


def _outproj_kernel(m_ref, w_ref, x_ref, gm_ref, pnm_ref, pnf_ref, sh_ref, sc_ref, x1_ref, ht_ref, hb_ref):
    y = _dot(m_ref[...], w_ref[...])
    x1 = x_ref[...] + gm_ref[...] * (_rms(y) * pnm_ref[...])
    x1_ref[...] = x1
    h2 = ((_rms(x1) * pnf_ref[...]) * (1.0 + sc_ref[...]) + sh_ref[...]).astype(BF16)
    hb_ref[...] = h2
    ht_ref[...] = _rows_to_tiles(h2)


def _outproj(merged, w_bf, x2, gate_m, pnm, pnf, shift_f, scale_f, seq):
    t, d = x2.shape
    tm = _pick(seq, (256, 128))
    per = seq // tm
    row = pl.BlockSpec((tm, d), lambda i: (i, 0))
    vec = pl.BlockSpec((1, d), lambda i: (0, 0))
    bvec = pl.BlockSpec((None, 1, d), lambda i: (i // per, 0, 0))
    return pl.pallas_call(
        _outproj_kernel,
        out_shape=(jax.ShapeDtypeStruct((t, d), F32), jax.ShapeDtypeStruct((t, d // 128, 128), BF16),
                   jax.ShapeDtypeStruct((t, d), BF16)),
        grid=(t // tm,),
        in_specs=[row, pl.BlockSpec((d, d), lambda i: (0, 0)), row, bvec, vec, vec, bvec, bvec],
        out_specs=(row, pl.BlockSpec((tm, d // 128, 128), lambda i: (i, 0, 0)), row),
        compiler_params=_params(("parallel",), 48),
        name="out_proj",
    )(merged, w_bf, x2, gate_m, pnm.reshape(1, d), pnf.reshape(1, d), shift_f, scale_f)


def _router_kernel(h_ref, w_ref, bias_ref, up_ref, eid_ref, gate_ref, pos_ref, cnt_ref, carry_ref, *, tr):
    @pl.when(pl.program_id(0) == 0)
    def _():
        carry_ref[...] = jnp.zeros_like(carry_ref)

    per_group = N_EXPERTS // N_GROUPS
    sig = jax.nn.sigmoid(_dot_nt(w_ref[...], h_ref[...]))
    choice = sig + bias_ref[...]
    eidx = lax.broadcasted_iota(jnp.int32, (N_EXPERTS, tr), 0)

    c3 = choice.reshape(N_GROUPS, per_group, tr)
    sub = lax.broadcasted_iota(jnp.int32, (N_GROUPS, per_group, tr), 1)
    m1 = jnp.max(c3, axis=1, keepdims=True)
    first = jnp.min(jnp.where(c3 == m1, sub, per_group), axis=1, keepdims=True)
    m2 = jnp.max(jnp.where(sub == first, -jnp.inf, c3), axis=1, keepdims=True)
    gs = (m1 + m2).reshape(N_GROUPS, tr)

    gidx = lax.broadcasted_iota(jnp.int32, (N_GROUPS, tr), 0)
    grank = jnp.zeros((N_GROUPS, tr), jnp.int32)
    for g in range(N_GROUPS):
        row = gs[g:g + 1, :]
        grank = grank + ((row > gs) | ((row == gs) & (gidx > g))).astype(jnp.int32)
    gsel = jnp.where(grank < TOPK_GROUPS, 1.0, 0.0)
    emask = jnp.broadcast_to(gsel.reshape(N_GROUPS, 1, tr), (N_GROUPS, per_group, tr)).reshape(N_EXPERTS, tr)
    cm = jnp.where(emask > 0.5, choice, -jnp.inf)

    rank = jnp.zeros((N_EXPERTS, tr), jnp.int32)
    for e in range(N_EXPERTS):
        row = cm[e:e + 1, :]
        rank = rank + ((row > cm) | ((row == cm) & (eidx > e))).astype(jnp.int32)
    sel = rank < TOP_K

    denom = jnp.sum(jnp.where(sel, sig, 0.0), axis=0, keepdims=True)
    gate_full = sig / denom * ROUTED_SCALE

    sel_b = jnp.where(sel, 1.0, 0.0).astype(BF16)
    carry = carry_ref[...]
    cum = _dot(sel_b, up_ref[...]) + jnp.concatenate([carry] * (tr // 128), axis=1)
    carry_new = carry + _dot(sel_b, jnp.ones((tr, 128), BF16))
    carry_ref[...] = carry_new
    cnt_ref[...] = carry_new.astype(jnp.int32)
    posi = cum.astype(jnp.int32)

    eids, gates, poss = [], [], []
    for r in range(TOP_K):
        hit = rank == r
        eids.append(jnp.sum(jnp.where(hit, eidx, 0), axis=0, keepdims=True))
        gates.append(jnp.sum(jnp.where(hit, gate_full, 0.0), axis=0, keepdims=True))
        poss.append(jnp.sum(jnp.where(hit, posi, 0), axis=0, keepdims=True))
    eid_ref[...] = jnp.concatenate(eids, axis=0)
    gate_ref[...] = jnp.concatenate(gates, axis=0)
    pos_ref[...] = jnp.concatenate(poss, axis=0)


def _router(h2b, w_router, router_bias):
    t, d = h2b.shape
    tr = _pick(t, (256, 128))
    idx = jnp.arange(tr)
    upper = (idx[:, None] < idx[None, :]).astype(BF16)
    kout = pl.BlockSpec((TOP_K, tr), lambda i: (0, i))
    return pl.pallas_call(
        functools.partial(_router_kernel, tr=tr),
        out_shape=(jax.ShapeDtypeStruct((TOP_K, t), jnp.int32), jax.ShapeDtypeStruct((TOP_K, t), F32),
                   jax.ShapeDtypeStruct((TOP_K, t), jnp.int32), jax.ShapeDtypeStruct((N_EXPERTS, 128), jnp.int32)),
        grid=(t // tr,),
        in_specs=[pl.BlockSpec((tr, d), lambda i: (i, 0)),
                  pl.BlockSpec((N_EXPERTS, d), lambda i: (0, 0)),
                  pl.BlockSpec((N_EXPERTS, 1), lambda i: (0, 0)),
                  pl.BlockSpec((tr, tr), lambda i: (0, 0))],
        out_specs=(kout, kout, kout, pl.BlockSpec((N_EXPERTS, 128), lambda i: (0, 0))),
        scratch_shapes=[pltpu.VMEM((N_EXPERTS, 128), F32)],
        compiler_params=_params(("arbitrary",), 32),
        name="router",
    )(h2b, w_router.T.astype(BF16), router_bias.reshape(N_EXPERTS, 1).astype(F32), upper)


def _dispatch_kernel(zs_ref, zn_ref, d_ref, h_ref, xs_hbm, zero_scr, sem, zsem, *, tm, n_blocks):
    i = pl.program_id(0)

    for r in range(tm):
        for k in range(TOP_K):
            pltpu.make_async_copy(h_ref.at[pl.ds(r, 1)], xs_hbm.at[pl.ds(d_ref[0, k * tm + r], 1)],
                                  sem).start(priority=(r + k) % 2)

    @pl.when(i == 0)
    def _():
        zero_scr[...] = jnp.zeros_like(zero_scr)
        for e in range(N_EXPERTS):
            def zbody(u, carry, e=e):
                pltpu.make_async_copy(zero_scr.at[pl.ds(0, 1)], xs_hbm.at[pl.ds(zs_ref[e] + u, 1)], zsem).start()
                return carry
            lax.fori_loop(0, zn_ref[e], zbody, 0)
        for e in range(N_EXPERTS):
            def zwait(u, carry):
                pltpu.make_async_copy(zero_scr.at[pl.ds(0, 1)], xs_hbm.at[pl.ds(0, 1)], zsem).wait()
                return carry
            lax.fori_loop(0, zn_ref[e], zwait, 0)

        def tbody(blk, carry):
            pltpu.make_async_copy(zero_scr, xs_hbm.at[pl.ds(blk * MOE_BLOCK, MOE_BLOCK)], zsem).start()
            return carry
        lax.fori_loop(zn_ref[N_EXPERTS], n_blocks, tbody, 0)

        def twait(blk, carry):
            pltpu.make_async_copy(zero_scr, xs_hbm.at[pl.ds(0, MOE_BLOCK)], zsem).wait()
            return carry
        lax.fori_loop(zn_ref[N_EXPERTS], n_blocks, twait, 0)

    for k in range(TOP_K):
        pltpu.make_async_copy(h_ref, xs_hbm.at[pl.ds(0, tm)], sem).wait()


def _dispatch(h2t, dest3, zstart, znum, n_slots, tm):
    t, nch, _ = h2t.shape
    grid_spec = pltpu.PrefetchScalarGridSpec(
        num_scalar_prefetch=2,
        grid=(t // tm,),
        in_specs=[pl.BlockSpec((None, 1, TOP_K * tm), lambda i, zs, zn: (i, 0, 0), memory_space=pltpu.SMEM),
                  pl.BlockSpec((tm, nch, 128), lambda i, zs, zn: (i, 0, 0))],
        out_specs=pl.BlockSpec(memory_space=pl.ANY),
        scratch_shapes=[pltpu.VMEM((MOE_BLOCK, nch, 128), h2t.dtype), pltpu.SemaphoreType.DMA(()),
                        pltpu.SemaphoreType.DMA(())],
    )
    return pl.pallas_call(
        functools.partial(_dispatch_kernel, tm=tm, n_blocks=n_slots // MOE_BLOCK),
        out_shape=jax.ShapeDtypeStruct((n_slots, nch, 128), h2t.dtype),
        grid_spec=grid_spec,
        compiler_params=_params(("arbitrary",), 32),
        name="dispatch",
    )(zstart, znum, dest3, h2t)


def _expert_kernel(be_ref, nu_ref, first_ref, par_ref, nxt_ref, x_ref, wg_hbm, wu_hbm, wd_hbm, o_ref,
                   wg_f, wu_f, wd_f, wg_bf, wu_bf, wd_bf, sem):
    i = pl.program_id(0)

    def weight_copies(e, s):
        return [pltpu.make_async_copy(src.at[e], dst.at[s], sem.at[s])
                for src, dst in ((wg_hbm, wg_f), (wu_hbm, wu_f), (wd_hbm, wd_f))]

    @pl.when(i == 0)
    def _():
        for cp in weight_copies(be_ref[0], 0):
            cp.start()

    @pl.when(first_ref[i] == 1)
    def _():
        p = par_ref[i]
        for cp in weight_copies(be_ref[i], p):
            cp.wait()
        wg_bf[...] = wg_f[p].astype(BF16)
        wu_bf[...] = wu_f[p].astype(BF16)
        wd_bf[...] = wd_f[p].astype(BF16)

        @pl.when(nxt_ref[i] >= 0)
        def _():
            for cp in weight_copies(nxt_ref[i], 1 - p):
                cp.start()

    @pl.when(i < nu_ref[0])
    def _():
        xb = _tiles_to_rows(x_ref[...])
        hid = _silu(_dot(xb, wg_bf[...])) * _dot(xb, wu_bf[...])
        o_ref[...] = _rows_to_tiles(_dot(hid.astype(BF16), wd_bf[...]).astype(o_ref.dtype))

    @pl.when(i >= nu_ref[0])
    def _():
        o_ref[...] = jnp.zeros_like(o_ref)


def _experts(x_sorted, block_expert, n_used, first, par, nxt, wg, wu, wd):
    n_slots, nch, _ = x_sorted.shape
    d = nch * 128
    n_blocks = block_expert.shape[0]
    de = wg.shape[-1]
    hbm = pl.BlockSpec(memory_space=pl.ANY)
    grid_spec = pltpu.PrefetchScalarGridSpec(
        num_scalar_prefetch=5,
        grid=(n_blocks,),
        in_specs=[pl.BlockSpec((MOE_BLOCK, nch, 128), lambda i, be, nu, *_: (jnp.minimum(i, nu[0] - 1), 0, 0)),
                  hbm, hbm, hbm],
        out_specs=pl.BlockSpec((MOE_BLOCK, nch, 128), lambda i, *_: (i, 0, 0)),
        scratch_shapes=[pltpu.VMEM((2, d, de), F32), pltpu.VMEM((2, d, de), F32), pltpu.VMEM((2, de, d), F32),
                        pltpu.VMEM((d, de), BF16), pltpu.VMEM((d, de), BF16), pltpu.VMEM((de, d), BF16),
                        pltpu.SemaphoreType.DMA((2,))],
    )
    return pl.pallas_call(
        _expert_kernel,
        out_shape=jax.ShapeDtypeStruct((n_slots, nch, 128), BF16),
        grid_spec=grid_spec,
        compiler_params=_params(("arbitrary",), 56),
        name="experts",
    )(block_expert, n_used, first, par, nxt, x_sorted, wg, wu, wd)


def _combine_kernel(dc_ref, dn_ref, y_hbm, gt_ref, h_ref, x1_ref, wg_ref, wu_ref, wd_ref, gf_ref, pn_ref,
                    o_ref, buf, sem, *, tm):
    i = pl.program_id(0)
    nsteps = pl.num_programs(0)
    slot = i % 2

    def issue_tile(d_ref, s):
        for r in range(tm):
            for k in range(TOP_K):
                pltpu.make_async_copy(y_hbm.at[pl.ds(d_ref[0, k * tm + r], 1)], buf.at[s, k, pl.ds(r, 1)],
                                      sem.at[s]).start(priority=(r + k) % 2)

    @pl.when(i == 0)
    def _():
        issue_tile(dc_ref, 0)

    @pl.when(i + 1 < nsteps)
    def _():
        issue_tile(dn_ref, 1 - slot)

    hb = h_ref[...]
    shared = _dot((_silu(_dot(hb, wg_ref[...])) * _dot(hb, wu_ref[...])).astype(BF16), wd_ref[...])

    for k in range(TOP_K):
        pltpu.make_async_copy(y_hbm.at[pl.ds(0, tm)], buf.at[slot, k], sem.at[slot]).wait()

    gt = gt_ref[...]
    acc = shared
    for k in range(TOP_K):
        acc = acc + gt[:, k:k + 1] * _tiles_to_rows(buf[slot, k]).astype(F32)
    o_ref[...] = x1_ref[...] + gf_ref[...] * (_rms(acc) * pn_ref[...])


def _combine(y_sorted, dest3, gate_t, h2b, x1, wgs, wus, wds, gate_f, pnf, seq, tm):
    t, d = x1.shape
    ds_ = wgs.shape[-1]
    per = seq // tm
    nt = t // tm
    smem = functools.partial(pl.BlockSpec, memory_space=pltpu.SMEM)
    row = pl.BlockSpec((tm, d), lambda i: (i, 0))
    return pl.pallas_call(
        functools.partial(_combine_kernel, tm=tm),
        out_shape=jax.ShapeDtypeStruct((t, d), F32),
        grid=(nt,),
        in_specs=[smem((None, 1, TOP_K * tm), lambda i: (i, 0, 0)),
                  smem((None, 1, TOP_K * tm), lambda i: (jnp.minimum(i + 1, nt - 1), 0, 0)),
                  pl.BlockSpec(memory_space=pl.ANY),
                  pl.BlockSpec((tm, TOP_K), lambda i: (i, 0)),
                  row, row,
                  pl.BlockSpec((d, ds_), lambda i: (0, 0)),
                  pl.BlockSpec((d, ds_), lambda i: (0, 0)),
                  pl.BlockSpec((ds_, d), lambda i: (0, 0)),
                  pl.BlockSpec((None, 1, d), lambda i: (i // per, 0, 0)),
                  pl.BlockSpec((1, d), lambda i: (0, 0))],
        out_specs=row,
        scratch_shapes=[pltpu.VMEM((2, TOP_K, tm) + y_sorted.shape[1:], y_sorted.dtype),
                        pltpu.SemaphoreType.DMA((2,))],
        compiler_params=_params(("arbitrary",), 48),
        name="combine",
    )(dest3, dest3, y_sorted, gate_t, h2b, x1, wgs, wus, wds, gate_f, pnf.reshape(1, d))


def _layer(x, c, lb, w_ada, b_ada, pre_norm_mix, post_norm_mix, w_in, hgrn_norm, w_branch_attn, w_branch_hgrn,
           w_out, pre_norm_ffn, post_norm_ffn, w_router, router_bias, w_gate_e, w_up_e, w_down_e,
           w_gate_s, w_up_s, w_down_s):
    nb, s, d = x.shape
    t = nb * s
    x2 = x.reshape(t, d)

    mod = _ada(c, w_ada, b_ada).reshape(nb, 6, 1, d)
    shift_m, scale_m, gate_m, shift_f, scale_f, gate_f = (mod[:, k] for k in range(6))

    w_in_bf = w_in.astype(BF16)
    qkv = []
    for g, (_, dil) in enumerate(ATT_GROUPS):
        cols = jnp.concatenate([w_in_bf[:, p * ATT_WIDTH + g * ATT_GW:p * ATT_WIDTH + (g + 1) * ATT_GW]
                                for p in range(3)], axis=1)
        o = _inproj(x2, pre_norm_mix, shift_m, scale_m, cols, nb, s, dil, tn=3 * ATT_GW)
        qkv.append(o.reshape(nb, dil, s // dil, 3 * ATT_GW))
    proj2 = _inproj(x2, pre_norm_mix, shift_m, scale_m, w_in_bf[:, QKV_COLS:], nb, s)
    proj3 = proj2.reshape(nb, s, -1)

    att = [_attention_group(qkv[g], g) for g in range(len(ATT_GROUPS))]
    orec = _hgrn(proj3, lb, hgrn_norm)
    merged = _branch(att, orec, proj2, w_branch_attn.astype(BF16), w_branch_hgrn.astype(BF16), d, s)
    x1, h2t, h2b = _outproj(merged, w_out.astype(BF16), x2, gate_m, post_norm_mix, pre_norm_ffn,
                            shift_f, scale_f, s)

    eid, gate, pos, cnt = _router(h2b, w_router, router_bias)

    counts = cnt[:, 0]
    padded = (counts + MOE_BLOCK - 1) // MOE_BLOCK * MOE_BLOCK
    pad_end = jnp.cumsum(padded)
    pad_start = pad_end - padded
    n_blocks = -(-(t * TOP_K) // MOE_BLOCK) + N_EXPERTS
    onehot = eid[None] == jnp.arange(N_EXPERTS, dtype=jnp.int32)[:, None, None]
    dest = jnp.sum(jnp.where(onehot, pad_start[:, None, None], 0), axis=0) + pos
    block_start = jnp.arange(n_blocks, dtype=jnp.int32) * MOE_BLOCK
    block_expert = jnp.minimum(jnp.sum((block_start[:, None] >= pad_end[None, :]).astype(jnp.int32), axis=1),
                               N_EXPERTS - 1)
    n_used = (pad_end[-1:] // MOE_BLOCK).astype(jnp.int32)

    tm = _pick(s, (128,))
    nt = t // tm
    dest3 = dest.reshape(TOP_K, nt, tm).transpose(1, 0, 2).reshape(nt, 1, TOP_K * tm)
    znum = jnp.concatenate([padded - counts, n_used]).astype(jnp.int32)
    x_sorted = _dispatch(h2t, dest3, (pad_start + counts).astype(jnp.int32), znum, n_blocks * MOE_BLOCK, tm)
    eidx = jnp.arange(N_EXPERTS, dtype=jnp.int32)
    has = counts > 0
    rank = jnp.cumsum(has.astype(jnp.int32)) - 1
    later = (eidx[None, :] > eidx[:, None]) & has[None, :]
    nxt_e = jnp.min(jnp.where(later, eidx[None, :], N_EXPERTS), axis=1)
    nxt_e = jnp.where(nxt_e == N_EXPERTS, -1, nxt_e).astype(jnp.int32)
    oh_b = block_expert[:, None] == eidx[None, :]
    pick = lambda v: jnp.sum(jnp.where(oh_b, v[None, :], 0), axis=1).astype(jnp.int32)
    first = ((block_start == pick(pad_start)) & (block_start < pad_end[-1])).astype(jnp.int32)
    y_sorted = _experts(x_sorted, block_expert, n_used, first, pick(rank % 2), pick(nxt_e),
                        w_gate_e, w_up_e, w_down_e)
    out = _combine(y_sorted, dest3, gate.T, h2b, x1, w_gate_s.astype(BF16), w_up_s.astype(BF16),
                   w_down_s.astype(BF16), gate_f, post_norm_ffn, s, tm)
    return out.reshape(nb, s, d)


def kernel(x, c, w_ada, b_ada, pre_norm_mix, post_norm_mix, w_in, hgrn_lb_logits, hgrn_norm, w_branch_attn,
           w_branch_hgrn, w_out, pre_norm_ffn, post_norm_ffn, w_router, router_bias, w_gate_e, w_up_e, w_down_e,
           w_gate_s, w_up_s, w_down_s):
    lb_table = jnp.cumsum(jax.nn.softmax(hgrn_lb_logits.astype(F32), axis=0), axis=0)
    depth = w_ada.shape[0]
    for l in range(depth):
        x = _layer(x, c, lb_table[l], w_ada[l], b_ada[l], pre_norm_mix[l], post_norm_mix[l], w_in[l],
                   hgrn_norm[l], w_branch_attn[l], w_branch_hgrn[l], w_out[l], pre_norm_ffn[l],
                   post_norm_ffn[l], w_router[l], router_bias[l], w_gate_e[l], w_up_e[l], w_down_e[l],
                   w_gate_s[l], w_up_s[l], w_down_s[l])
    return x
```

```python
import functools

import jax
import jax.numpy as jnp
from jax import lax
from jax.experimental import pallas as pl
from jax.experimental.pallas import tpu as pltpu

F32 = jnp.float32
BF16 = jnp.bfloat16

HEAD_DIM = 128
ATT_GROUPS = ((128, 1), (512, 4), (2048, 16))
ATT_HPG = 4
ATT_HEADS = ATT_HPG * len(ATT_GROUPS)
ATT_WIDTH = ATT_HEADS * HEAD_DIM
ATT_GW = ATT_HPG * HEAD_DIM
ATT_BLOCK = 128
ATT_OUT_COLS = ATT_GW + HEAD_DIM
LSE_LANES = HEAD_DIM // ATT_HPG
HGRN_HEADS = 8
HGRN_D = 128
HGRN_WIDTH = HGRN_HEADS * HGRN_D
HGRN_CHUNK = 64
HGRN_DIRECT = 8
N_EXPERTS = 64
N_GROUPS = 8
TOPK_GROUPS = 4
TOP_K = 8
ROUTED_SCALE = 2.5
MOE_BLOCK = 256
NORM_EPS = 1e-6
V7X_VMEM_LIMIT = 56 * 1024 * 1024

QKV_COLS = 3 * ATT_WIDTH
RQ_OFF = 0
RF_OFF = RQ_OFF + HGRN_WIDTH
RI_OFF = RF_OFF + HGRN_WIDTH
RG_OFF = RI_OFF + HGRN_WIDTH
GATE_OFF = RG_OFF + HGRN_WIDTH


def _pick(n, cands):
    for c in cands:
        if n % c == 0:
            return c
    raise ValueError(f"no tile of {cands} divides {n}")


def _params(sem, vmem_mib=None):
    kw = dict(dimension_semantics=sem)
    if vmem_mib is not None:
        kw["vmem_limit_bytes"] = min(vmem_mib * 1024 * 1024, V7X_VMEM_LIMIT)
    return pltpu.CompilerParams(**kw)


def _dot(a, b):
    return jnp.dot(a, b, preferred_element_type=F32)


def _dot_nt(a, b):
    return lax.dot_general(a, b, (((1,), (1,)), ((), ())), preferred_element_type=F32)


def _dot_tn(a, b):
    return lax.dot_general(a, b, (((0,), (0,)), ((), ())), preferred_element_type=F32)


def _rms(y):
    return y * lax.rsqrt(jnp.mean(y * y, axis=-1, keepdims=True) + NORM_EPS)


def _silu(a):
    return a * jax.nn.sigmoid(a)


ADA_KCHUNK = 128


def _ada_kernel(ct_ref, w_ref, b_ref, o_ref, *, nb, d):
    a = _silu(ct_ref[...])
    tn = w_ref.shape[1]
    accs = [jnp.zeros((8, tn), F32) for _ in range(nb)]
    for kc in range(d // ADA_KCHUNK):
        wc = w_ref[kc * ADA_KCHUNK:(kc + 1) * ADA_KCHUNK, :]
        ac = a[kc * ADA_KCHUNK:(kc + 1) * ADA_KCHUNK, :]
        for b in range(nb):
            p = ac[:, b:b + 1] * wc
            accs[b] = accs[b] + p.reshape(ADA_KCHUNK // 8, 8, tn).sum(axis=0)
    rows = [jnp.sum(acc, axis=0, keepdims=True) for acc in accs]
    o_ref[...] = jnp.concatenate(rows, axis=0) + b_ref[...]


def _ada(c, w_ada, b_ada):
    nb, d = c.shape
    n = w_ada.shape[1]
    tn = _pick(n, (512, 256, 128))
    return pl.pallas_call(
        functools.partial(_ada_kernel, nb=nb, d=d),
        out_shape=jax.ShapeDtypeStruct((nb, n), F32),
        grid=(n // tn,),
        in_specs=[pl.BlockSpec((d, nb), lambda j: (0, 0)),
                  pl.BlockSpec((d, tn), lambda j: (0, j)),
                  pl.BlockSpec((1, tn), lambda j: (0, j))],
        out_specs=pl.BlockSpec((nb, tn), lambda j: (0, j)),
        compiler_params=_params(("arbitrary",), 32),
        name="ada_mod",
    )(c.T, w_ada, b_ada.reshape(1, n))


def _inproj_kernel(x_ref, g_ref, sh_ref, sc_ref, w_ref, o_ref, h_scr, *slab, dil):
    @pl.when(pl.program_id(1) == 0)
    def _():
        y = _rms(x_ref[...]) * g_ref[...]
        h_scr[...] = (y * (1.0 + sc_ref[...]) + sh_ref[...]).astype(BF16)

    o = _dot(h_scr[...], w_ref[...])
    if dil == 1:
        o_ref[...] = o.astype(o_ref.dtype)
    else:
        slab_ref, = slab
        tm, tn = o.shape
        for c in range(tn // 128):
            slab_ref[c] = o[:, c * 128:(c + 1) * 128]
        for r in range(dil):
            for c in range(tn // 128):
                o_ref[r, :, c * 128:(c + 1) * 128] = slab_ref[c, pl.ds(r, tm // dil, stride=dil), :].astype(o_ref.dtype)


def _inproj(x2, gain, shift, scale, w_bf, nb, seq, dil=1, tn=None):
    t, d = x2.shape
    n = w_bf.shape[1]
    tm = _pick(seq, (1024, 512, 256) if tn is None else (512, 256))
    tn = tn or _pick(n, (1024, 512, 256, 128))
    per = seq // tm
    if dil == 1:
        out_shape = jax.ShapeDtypeStruct((t, n), BF16)
        out_spec = pl.BlockSpec((tm, tn), lambda i, j: (i, j))
        scratch = [pltpu.VMEM((tm, d), BF16)]
    else:
        out_shape = jax.ShapeDtypeStruct((nb, dil, seq // dil, n), BF16)
        out_spec = pl.BlockSpec((None, dil, tm // dil, tn), lambda i, j: (i // per, 0, i % per, j))
        scratch = [pltpu.VMEM((tm, d), BF16), pltpu.VMEM((tn // 128, tm, 128), F32)]
    return pl.pallas_call(
        functools.partial(_inproj_kernel, dil=dil),
        out_shape=out_shape,
        grid=(t // tm, n // tn),
        in_specs=[pl.BlockSpec((tm, d), lambda i, j: (i, 0)),
                  pl.BlockSpec((1, d), lambda i, j: (0, 0)),
                  pl.BlockSpec((None, 1, d), lambda i, j: (i // per, 0, 0)),
                  pl.BlockSpec((None, 1, d), lambda i, j: (i // per, 0, 0)),
                  pl.BlockSpec((d, tn), lambda i, j: (0, j))],
        out_specs=out_spec,
        scratch_shapes=scratch,
        compiler_params=_params(("parallel", "arbitrary"), 48),
        name=f"in_proj_d{dil}_n{n}",
    )(x2, gain.reshape(1, d), shift, scale, w_bf)


def _attn_kernel(q_ref, kp_ref, kc_ref, vp_ref, vc_ref, o_ref, *, dil, slopes, scale, qb):
    n = pl.program_id(2)
    qi = lax.broadcasted_iota(jnp.int32, (ATT_BLOCK, ATT_BLOCK), 0)
    ki = lax.broadcasted_iota(jnp.int32, (ATT_BLOCK, ATT_BLOCK), 1)
    jc = qi - ki
    jp = jc + ATT_BLOCK
    first_lim = jnp.where(n > 0, ATT_BLOCK, -1)
    valid_c = jc >= 0
    lane = lax.broadcasted_iota(jnp.int32, (ATT_BLOCK, HEAD_DIM), 1)
    ones = jnp.ones((ATT_BLOCK, HEAD_DIM), BF16)
    for j in range(qb):
        rows = slice(j * ATT_BLOCK, (j + 1) * ATT_BLOCK)
        prow = slice((j - 1) * ATT_BLOCK, j * ATT_BLOCK)
        valid_p = jp <= (first_lim if j == 0 else ATT_BLOCK)
        lse_blk = jnp.zeros((ATT_BLOCK, HEAD_DIM), F32)
        for h in range(ATT_HPG):
            hs = slice(h * HEAD_DIM, (h + 1) * HEAD_DIM)
            q = q_ref[rows, hs]
            k_p = kp_ref[:, hs] if j == 0 else kc_ref[prow, hs]
            v_p = vp_ref[:, hs] if j == 0 else vc_ref[prow, hs]
            bias = slopes[h] * dil
            s_c = _dot_nt(q, kc_ref[rows, hs]) * scale - bias * jc.astype(F32)
            s_p = _dot_nt(q, k_p) * scale - bias * jp.astype(F32)
            s_c = jnp.where(valid_c, s_c, -jnp.inf)
            s_p = jnp.where(valid_p, s_p, -jnp.inf)
            m = jnp.maximum(jnp.max(s_c, axis=-1, keepdims=True), jnp.max(s_p, axis=-1, keepdims=True))
            p_c = jnp.exp(s_c - m).astype(BF16)
            p_p = jnp.exp(s_p - m).astype(BF16)
            oa = (_dot(p_c, jnp.concatenate([vc_ref[rows, hs], ones], axis=1))
                  + _dot(p_p, jnp.concatenate([v_p, ones], axis=1)))
            l = oa[:, HEAD_DIM:HEAD_DIM + 1]
            o_ref[rows, hs] = oa[:, :HEAD_DIM] / l
            lse = m + jnp.log(l)
            lse_blk = jnp.where((lane >= h * LSE_LANES) & (lane < (h + 1) * LSE_LANES), lse, lse_blk)
        o_ref[rows, ATT_GW:] = lse_blk


def _attention_group(qkv, g):
    nb_, dil, l, _ = qkv.shape
    window, dil_ = ATT_GROUPS[g]
    assert dil == dil_ and window // dil == ATT_BLOCK and l % ATT_BLOCK == 0
    qb = _pick(l // ATT_BLOCK, (4, 2, 1))
    slopes = tuple(2.0 ** (-8.0 * (g * ATT_HPG + i + 1) / ATT_HEADS) for i in range(ATT_HPG))

    def spec(part, prev):
        if prev:
            return pl.BlockSpec((None, None, ATT_BLOCK, ATT_GW),
                                lambda b, r, n: (b, r, jnp.maximum(n * qb - 1, 0), part))
        return pl.BlockSpec((None, None, qb * ATT_BLOCK, ATT_GW), lambda b, r, n: (b, r, n, part))

    return pl.pallas_call(
        functools.partial(_attn_kernel, dil=float(dil), slopes=slopes, scale=HEAD_DIM ** -0.5, qb=qb),
        out_shape=jax.ShapeDtypeStruct((nb_, dil, l, ATT_OUT_COLS), F32),
        grid=(nb_, dil, l // (qb * ATT_BLOCK)),
        in_specs=[spec(0, False), spec(1, True), spec(1, False), spec(2, True), spec(2, False)],
        out_specs=pl.BlockSpec((None, None, qb * ATT_BLOCK, ATT_OUT_COLS), lambda b, r, n: (b, r, n, 0)),
        compiler_params=_params(("parallel", "parallel", "arbitrary")),
        name=f"attn_g{g}",
    )(qkv, qkv, qkv, qkv, qkv)


def _hgrn_kernel(q_ref, f_ref, i_ref, g_ref, lb_ref, gn_ref, tri_ref, o_ref, st_ref, sh_ref, *, tb, hp):
    @pl.when(pl.program_id(2) == 0)
    def _():
        st_ref[...] = jnp.zeros_like(st_ref)
        sh_ref[:, :, :, 0:HGRN_DIRECT, :] = jnp.zeros((hp, 3, tb // HGRN_DIRECT, HGRN_DIRECT, HGRN_D), F32)

    for hh in range(hp):
        cols = slice(hh * HGRN_D, (hh + 1) * HGRN_D)
        o_ref[:, cols] = _hgrn_head(q_ref[:, cols], f_ref[:, cols], i_ref[:, cols], g_ref[:, cols],
                                    lb_ref[:, cols], gn_ref[:, cols], tri_ref[...], st_ref.at[hh], sh_ref.at[hh],
                                    tb).astype(o_ref.dtype)


def _hgrn_head(q_in, f_in, i_in, g_in, lb, gn, tri, st_ref, sh_ref, tb):
    c_ = HGRN_CHUNK
    nc = tb // c_
    q = q_in.astype(F32)
    v = i_in.astype(F32)
    f = lb + (1.0 - lb) * jax.nn.sigmoid(f_in.astype(F32))
    kk = 1.0 - f
    lf = jnp.log(f)

    hi = lf.astype(BF16)
    r1 = lf - hi.astype(F32)
    mid = r1.astype(BF16)
    lo = (r1 - mid.astype(F32)).astype(BF16)
    b = _dot(tri, hi) + _dot(tri, mid) + _dot(tri, lo)

    def ref_rows(blk, row):
        b3 = b.reshape(tb // blk, blk, HGRN_D)
        return jnp.broadcast_to(b3[:, row:row + 1, :], (tb // blk, blk, HGRN_D)).reshape(tb, HGRN_D)

    ti = lax.broadcasted_iota(jnp.int32, (c_, c_), 0)
    si = lax.broadcasted_iota(jnp.int32, (c_, c_), 1)
    ssum = jnp.zeros((nc, c_, c_), F32)
    half = c_ // 2
    while half >= HGRN_DIRECT:
        bref = ref_rows(2 * half, half)
        ql = (q * jnp.exp(b - bref)).astype(BF16).reshape(nc, c_, HGRN_D)
        kl = (kk * jnp.exp(bref - b)).astype(BF16).reshape(nc, c_, HGRN_D)
        s_l = jnp.einsum("ctk,csk->cts", ql, kl, preferred_element_type=F32)
        mask = ((ti // (2 * half)) == (si // (2 * half))) & (((ti // half) % 2) == 1) & (((si // half) % 2) == 0)
        ssum = ssum + jnp.where(mask[None], s_l, 0.0)
        half //= 2
    v3 = v.astype(BF16).reshape(nc, c_, HGRN_D)
    o_acc = jnp.einsum("cts,csv->ctv", ssum.astype(BF16), v3, preferred_element_type=F32).reshape(tb, HGRN_D)

    nt8 = tb // HGRN_DIRECT
    for plane, val in enumerate((kk, f, v)):
        sh_ref[plane, :, HGRN_DIRECT:, :] = val.reshape(nt8, HGRN_DIRECT, HGRN_D)

    def shifted(plane, lag):
        st = HGRN_DIRECT - lag
        return sh_ref[plane, :, st:st + HGRN_DIRECT, :].reshape(tb, HGRN_D)

    ones = jnp.ones((HGRN_D, HGRN_D), BF16)
    decay = None
    for lag in range(HGRN_DIRECT):
        if lag == 0:
            w = q * kk
        else:
            decay = f if lag == 1 else decay * shifted(1, lag - 1)
            w = q * shifted(0, lag) * decay
        o_acc = o_acc + _dot(w.astype(BF16), ones) * (v if lag == 0 else shifted(2, lag))

    blast = ref_rows(c_, c_ - 1)
    qe = (q * jnp.exp(b)).astype(BF16)
    kt = (kk * jnp.exp(blast - b)).astype(BF16)
    vb = v.astype(BF16)
    st_t = st_ref[...]
    inter = []
    for c in range(nc):
        rows = slice(c * c_, (c + 1) * c_)
        inter.append(_dot_nt(qe[rows], st_t.astype(BF16)))
        dec = jnp.exp(blast[c * c_:c * c_ + 1, :])
        st_t = st_t * dec + _dot_tn(vb[rows], kt[rows])
    st_ref[...] = st_t
    o = o_acc + jnp.concatenate(inter, axis=0)

    return (_rms(o) * gn) * _silu(g_in.astype(F32))


HGRN_HEADS_PER_STEP = 4


def _hgrn(proj3, lb, gnorm):
    nb_, s, nc = proj3.shape
    tb = _pick(s, (512, 256, 128, 64))
    hp = HGRN_HEADS_PER_STEP
    w = hp * HGRN_D
    idx = jnp.arange(tb)
    tri = ((idx[:, None] // HGRN_CHUNK == idx[None, :] // HGRN_CHUNK) & (idx[None, :] <= idx[:, None])).astype(BF16)

    def spec(off):
        cb = off // w
        return pl.BlockSpec((None, tb, w), lambda b, h, n: (b, n, cb + h))

    vec = pl.BlockSpec((1, w), lambda b, h, n: (0, h))
    out = pl.pallas_call(
        functools.partial(_hgrn_kernel, tb=tb, hp=hp),
        out_shape=jax.ShapeDtypeStruct((nb_, s, HGRN_WIDTH), BF16),
        grid=(nb_, HGRN_HEADS // hp, s // tb),
        in_specs=[spec(RQ_OFF), spec(RF_OFF), spec(RI_OFF), spec(RG_OFF), vec, vec,
                  pl.BlockSpec((tb, tb), lambda b, h, n: (0, 0))],
        out_specs=pl.BlockSpec((None, tb, w), lambda b, h, n: (b, n, h)),
        scratch_shapes=[pltpu.VMEM((hp, HGRN_D, HGRN_D), F32),
                        pltpu.VMEM((hp, 3, tb // HGRN_DIRECT, 2 * HGRN_DIRECT, HGRN_D), F32)],
        compiler_params=_params(("parallel", "parallel", "arbitrary"), 32),
        name="hgrn2",
    )(proj3, proj3, proj3, proj3, lb.reshape(1, HGRN_WIDTH), gnorm.reshape(1, HGRN_WIDTH), tri)
    return out.reshape(nb_ * s, HGRN_WIDTH)


def _branch_kernel(a0_ref, a1_ref, a2_ref, r_ref, ga_ref, gb_ref, wa_ref, wr_ref, o_ref, att_scr, nat_scr,
                   *, dils):
    tm = att_scr.shape[0]
    nslab = ATT_OUT_COLS // 128
    for gi, (a_ref, dil) in enumerate(zip((a1_ref, a2_ref), dils)):
        for r in range(dil):
            for c in range(nslab):
                nat_scr[gi, c, pl.ds(r, tm // dil, stride=dil), :] = a_ref[r, :, c * 128:(c + 1) * 128]

    def head(g, h):
        return a0_ref[:, h * HEAD_DIM:(h + 1) * HEAD_DIM] if g == 0 else nat_scr[g - 1, h]

    def lse(g, h):
        if g == 0:
            return a0_ref[:, ATT_GW + h * LSE_LANES:ATT_GW + h * LSE_LANES + 1]
        return nat_scr[g - 1, ATT_HPG, :, h * LSE_LANES:h * LSE_LANES + 1]

    for h in range(ATT_HPG):
        ls = [lse(g, h) for g in range(3)]
        m = jnp.maximum(jnp.maximum(ls[0], ls[1]), ls[2])
        es = [jnp.exp(l - m) for l in ls]
        den = es[0] + es[1] + es[2]
        o = ((es[0] / den) * head(0, h) + (es[1] / den) * head(1, h)) + (es[2] / den) * head(2, h)
        att_scr[:, h * HEAD_DIM:(h + 1) * HEAD_DIM] = o.astype(BF16)

    a = _dot(att_scr[...], wa_ref[...])
    r = _dot(r_ref[...], wr_ref[...])
    o_ref[...] = (jax.nn.sigmoid(ga_ref[...].astype(F32)) * a
                  + jax.nn.sigmoid(gb_ref[...].astype(F32)) * r).astype(o_ref.dtype)


def _branch(att, orec, proj2, wa_bf, wr_bf, d, seq):
    t = orec.shape[0]
    tm = _pick(seq, (256,))
    per = seq // tm
    assert GATE_OFF % d == 0
    ga0 = GATE_OFF // d
    dils = tuple(dil for _, dil in ATT_GROUPS[1:])

    def aspec(dil):
        return pl.BlockSpec((None, dil, tm // dil, ATT_OUT_COLS), lambda i: (i // per, 0, i % per, 0))

    return pl.pallas_call(
        functools.partial(_branch_kernel, dils=dils),
        out_shape=jax.ShapeDtypeStruct((t, d), BF16),
        grid=(t // tm,),
        in_specs=[pl.BlockSpec((tm, ATT_OUT_COLS), lambda i: (i, 0)), aspec(dils[0]), aspec(dils[1]),
                  pl.BlockSpec((tm, HGRN_WIDTH), lambda i: (i, 0)),
                  pl.BlockSpec((tm, d), lambda i: (i, ga0)),
                  pl.BlockSpec((tm, d), lambda i: (i, ga0 + 1)),
                  pl.BlockSpec((ATT_GW, d), lambda i: (0, 0)),
                  pl.BlockSpec((HGRN_WIDTH, d), lambda i: (0, 0))],
        out_specs=pl.BlockSpec((tm, d), lambda i: (i, 0)),
        scratch_shapes=[pltpu.VMEM((tm, ATT_GW), BF16),
                        pltpu.VMEM((len(dils), ATT_OUT_COLS // 128, tm, 128), F32)],
        compiler_params=_params(("parallel",), 48),
        name="branch_merge",
    )(att[0].reshape(t, ATT_OUT_COLS), att[1], att[2], orec, proj2, proj2, wa_bf, wr_bf)


def _rows_to_tiles(a):
    m, n = a.shape
    return jnp.swapaxes(jnp.stack([a[:, c * 128:(c + 1) * 128] for c in range(n // 128)], axis=0), 0, 1)


def _tiles_to_rows(a):
    at = jnp.swapaxes(a, 0, 1)
    return jnp.concatenate([at[c] for c in range(a.shape[1])], axis=1)


def _outproj_kernel(m_ref, w_ref, x_ref, gm_ref, pnm_ref, pnf_ref, sh_ref, sc_ref, x1_ref, ht_ref, hb_ref):
    y = _dot(m_ref[...], w_ref[...])
    x1 = x_ref[...] + gm_ref[...] * (_rms(y) * pnm_ref[...])
    x1_ref[...] = x1
    h2 = ((_rms(x1) * pnf_ref[...]) * (1.0 + sc_ref[...]) + sh_ref[...]).astype(BF16)
    hb_ref[...] = h2
    ht_ref[...] = _rows_to_tiles(h2)


def _outproj(merged, w_bf, x2, gate_m, pnm, pnf, shift_f, scale_f, seq):
    t, d = x2.shape
    tm = _pick(seq, (256, 128))
    per = seq // tm
    row = pl.BlockSpec((tm, d), lambda i: (i, 0))
    vec = pl.BlockSpec((1, d), lambda i: (0, 0))
    bvec = pl.BlockSpec((None, 1, d), lambda i: (i // per, 0, 0))
    return pl.pallas_call(
        _outproj_kernel,
        out_shape=(jax.ShapeDtypeStruct((t, d), F32), jax.ShapeDtypeStruct((t, d // 128, 128), BF16),
                   jax.ShapeDtypeStruct((t, d), BF16)),
        grid=(t // tm,),
        in_specs=[row, pl.BlockSpec((d, d), lambda i: (0, 0)), row, bvec, vec, vec, bvec, bvec],
        out_specs=(row, pl.BlockSpec((tm, d // 128, 128), lambda i: (i, 0, 0)), row),
        compiler_params=_params(("parallel",), 48),
        name="out_proj",
    )(merged, w_bf, x2, gate_m, pnm.reshape(1, d), pnf.reshape(1, d), shift_f, scale_f)


def _router_kernel(h_ref, w_ref, bias_ref, up_ref, eid_ref, gate_ref, pos_ref, cnt_ref, carry_ref, *, tr):
    @pl.when(pl.program_id(0) == 0)
    def _():
        carry_ref[...] = jnp.zeros_like(carry_ref)

    per_group = N_EXPERTS // N_GROUPS
    sig = jax.nn.sigmoid(_dot_nt(w_ref[...], h_ref[...]))
    choice = sig + bias_ref[...]
    eidx = lax.broadcasted_iota(jnp.int32, (N_EXPERTS, tr), 0)

    c3 = choice.reshape(N_GROUPS, per_group, tr)
    sub = lax.broadcasted_iota(jnp.int32, (N_GROUPS, per_group, tr), 1)
    m1 = jnp.max(c3, axis=1, keepdims=True)
    first = jnp.min(jnp.where(c3 == m1, sub, per_group), axis=1, keepdims=True)
    m2 = jnp.max(jnp.where(sub == first, -jnp.inf, c3), axis=1, keepdims=True)
    gs = (m1 + m2).reshape(N_GROUPS, tr)

    gidx = lax.broadcasted_iota(jnp.int32, (N_GROUPS, tr), 0)
    grank = jnp.zeros((N_GROUPS, tr), jnp.int32)
    for g in range(N_GROUPS):
        row = gs[g:g + 1, :]
        grank = grank + ((row > gs) | ((row == gs) & (gidx > g))).astype(jnp.int32)
    gsel = jnp.where(grank < TOPK_GROUPS, 1.0, 0.0)
    emask = jnp.broadcast_to(gsel.reshape(N_GROUPS, 1, tr), (N_GROUPS, per_group, tr)).reshape(N_EXPERTS, tr)
    cm = jnp.where(emask > 0.5, choice, -jnp.inf)

    rank = jnp.zeros((N_EXPERTS, tr), jnp.int32)
    for e in range(N_EXPERTS):
        row = cm[e:e + 1, :]
        rank = rank + ((row > cm) | ((row == cm) & (eidx > e))).astype(jnp.int32)
    sel = rank < TOP_K

    denom = jnp.sum(jnp.where(sel, sig, 0.0), axis=0, keepdims=True)
    gate_full = sig / denom * ROUTED_SCALE

    sel_b = jnp.where(sel, 1.0, 0.0).astype(BF16)
    carry = carry_ref[...]
    cum = _dot(sel_b, up_ref[...]) + jnp.concatenate([carry] * (tr // 128), axis=1)
    carry_new = carry + _dot(sel_b, jnp.ones((tr, 128), BF16))
    carry_ref[...] = carry_new
    cnt_ref[...] = carry_new.astype(jnp.int32)
    posi = cum.astype(jnp.int32)

    eids, gates, poss = [], [], []
    for r in range(TOP_K):
        hit = rank == r
        eids.append(jnp.sum(jnp.where(hit, eidx, 0), axis=0, keepdims=True))
        gates.append(jnp.sum(jnp.where(hit, gate_full, 0.0), axis=0, keepdims=True))
        poss.append(jnp.sum(jnp.where(hit, posi, 0), axis=0, keepdims=True))
    eid_ref[...] = jnp.concatenate(eids, axis=0)
    gate_ref[...] = jnp.concatenate(gates, axis=0)
    pos_ref[...] = jnp.concatenate(poss, axis=0)


def _router(h2b, w_router, router_bias):
    t, d = h2b.shape
    tr = _pick(t, (256, 128))
    idx = jnp.arange(tr)
    upper = (idx[:, None] < idx[None, :]).astype(BF16)
    kout = pl.BlockSpec((TOP_K, tr), lambda i: (0, i))
    return pl.pallas_call(
        functools.partial(_router_kernel, tr=tr),
        out_shape=(jax.ShapeDtypeStruct((TOP_K, t), jnp.int32), jax.ShapeDtypeStruct((TOP_K, t), F32),
                   jax.ShapeDtypeStruct((TOP_K, t), jnp.int32), jax.ShapeDtypeStruct((N_EXPERTS, 128), jnp.int32)),
        grid=(t // tr,),
        in_specs=[pl.BlockSpec((tr, d), lambda i: (i, 0)),
                  pl.BlockSpec((N_EXPERTS, d), lambda i: (0, 0)),
                  pl.BlockSpec((N_EXPERTS, 1), lambda i: (0, 0)),
                  pl.BlockSpec((tr, tr), lambda i: (0, 0))],
        out_specs=(kout, kout, kout, pl.BlockSpec((N_EXPERTS, 128), lambda i: (0, 0))),
        scratch_shapes=[pltpu.VMEM((N_EXPERTS, 128), F32)],
        compiler_params=_params(("arbitrary",), 32),
        name="router",
    )(h2b, w_router.T.astype(BF16), router_bias.reshape(N_EXPERTS, 1).astype(F32), upper)


def _dispatch_kernel(zs_ref, zn_ref, d_ref, h_ref, xs_hbm, zero_scr, sem, zsem, *, tm, n_blocks):
    i = pl.program_id(0)

    for r in range(tm):
        for k in range(TOP_K):
            pltpu.make_async_copy(h_ref.at[pl.ds(r, 1)], xs_hbm.at[pl.ds(d_ref[0, k * tm + r], 1)],
                                  sem).start(priority=(r + k) % 2)

    @pl.when(i == 0)
    def _():
        zero_scr[...] = jnp.zeros_like(zero_scr)
        for e in range(N_EXPERTS):
            def zbody(u, carry, e=e):
                pltpu.make_async_copy(zero_scr.at[pl.ds(0, 1)], xs_hbm.at[pl.ds(zs_ref[e] + u, 1)], zsem).start()
                return carry
            lax.fori_loop(0, zn_ref[e], zbody, 0)
        for e in range(N_EXPERTS):
            def zwait(u, carry):
                pltpu.make_async_copy(zero_scr.at[pl.ds(0, 1)], xs_hbm.at[pl.ds(0, 1)], zsem).wait()
                return carry
            lax.fori_loop(0, zn_ref[e], zwait, 0)

        def tbody(blk, carry):
            pltpu.make_async_copy(zero_scr, xs_hbm.at[pl.ds(blk * MOE_BLOCK, MOE_BLOCK)], zsem).start()
            return carry
        lax.fori_loop(zn_ref[N_EXPERTS], n_blocks, tbody, 0)

        def twait(blk, carry):
            pltpu.make_async_copy(zero_scr, xs_hbm.at[pl.ds(0, MOE_BLOCK)], zsem).wait()
            return carry
        lax.fori_loop(zn_ref[N_EXPERTS], n_blocks, twait, 0)

    for k in range(TOP_K):
        pltpu.make_async_copy(h_ref, xs_hbm.at[pl.ds(0, tm)], sem).wait()


def _dispatch(h2t, dest3, zstart, znum, n_slots, tm):
    t, nch, _ = h2t.shape
    grid_spec = pltpu.PrefetchScalarGridSpec(
        num_scalar_prefetch=2,
        grid=(t // tm,),
        in_specs=[pl.BlockSpec((None, 1, TOP_K * tm), lambda i, zs, zn: (i, 0, 0), memory_space=pltpu.SMEM),
                  pl.BlockSpec((tm, nch, 128), lambda i, zs, zn: (i, 0, 0))],
        out_specs=pl.BlockSpec(memory_space=pl.ANY),
        scratch_shapes=[pltpu.VMEM((MOE_BLOCK, nch, 128), h2t.dtype), pltpu.SemaphoreType.DMA(()),
                        pltpu.SemaphoreType.DMA(())],
    )
    return pl.pallas_call(
        functools.partial(_dispatch_kernel, tm=tm, n_blocks=n_slots // MOE_BLOCK),
        out_shape=jax.ShapeDtypeStruct((n_slots, nch, 128), h2t.dtype),
        grid_spec=grid_spec,
        compiler_params=_params(("arbitrary",), 32),
        name="dispatch",
    )(zstart, znum, dest3, h2t)


def _expert_kernel(be_ref, nu_ref, first_ref, par_ref, nxt_ref, x_ref, wg_hbm, wu_hbm, wd_hbm, o_ref,
                   wg_f, wu_f, wd_f, wg_bf, wu_bf, wd_bf, sem):
    i = pl.program_id(0)

    def weight_copies(e, s):
        return [pltpu.make_async_copy(src.at[e], dst.at[s], sem.at[s])
                for src, dst in ((wg_hbm, wg_f), (wu_hbm, wu_f), (wd_hbm, wd_f))]

    @pl.when(i == 0)
    def _():
        for cp in weight_copies(be_ref[0], 0):
            cp.start()

    @pl.when(first_ref[i] == 1)
    def _():
        p = par_ref[i]
        for cp in weight_copies(be_ref[i], p):
            cp.wait()
        wg_bf[...] = wg_f[p].astype(BF16)
        wu_bf[...] = wu_f[p].astype(BF16)
        wd_bf[...] = wd_f[p].astype(BF16)

        @pl.when(nxt_ref[i] >= 0)
        def _():
            for cp in weight_copies(nxt_ref[i], 1 - p):
                cp.start()

    @pl.when(i < nu_ref[0])
    def _():
        xb = _tiles_to_rows(x_ref[...])
        hid = _silu(_dot(xb, wg_bf[...])) * _dot(xb, wu_bf[...])
        o_ref[...] = _rows_to_tiles(_dot(hid.astype(BF16), wd_bf[...]).astype(o_ref.dtype))

    @pl.when(i >= nu_ref[0])
    def _():
        o_ref[...] = jnp.zeros_like(o_ref)


def _experts(x_sorted, block_expert, n_used, first, par, nxt, wg, wu, wd):
    n_slots, nch, _ = x_sorted.shape
    d = nch * 128
    n_blocks = block_expert.shape[0]
    de = wg.shape[-1]
    hbm = pl.BlockSpec(memory_space=pl.ANY)
    grid_spec = pltpu.PrefetchScalarGridSpec(
        num_scalar_prefetch=5,
        grid=(n_blocks,),
        in_specs=[pl.BlockSpec((MOE_BLOCK, nch, 128), lambda i, be, nu, *_: (jnp.minimum(i, nu[0] - 1), 0, 0)),
                  hbm, hbm, hbm],
        out_specs=pl.BlockSpec((MOE_BLOCK, nch, 128), lambda i, *_: (i, 0, 0)),
        scratch_shapes=[pltpu.VMEM((2, d, de), F32), pltpu.VMEM((2, d, de), F32), pltpu.VMEM((2, de, d), F32),
                        pltpu.VMEM((d, de), BF16), pltpu.VMEM((d, de), BF16), pltpu.VMEM((de, d), BF16),
                        pltpu.SemaphoreType.DMA((2,))],
    )
    return pl.pallas_call(
        _expert_kernel,
        out_shape=jax.ShapeDtypeStruct((n_slots, nch, 128), BF16),
        grid_spec=grid_spec,
        compiler_params=_params(("arbitrary",), 56),
        name="experts",
    )(block_expert, n_used, first, par, nxt, x_sorted, wg, wu, wd)


def _combine_kernel(dc_ref, dn_ref, y_hbm, gt_ref, h_ref, x1_ref, wg_ref, wu_ref, wd_ref, gf_ref, pn_ref,
                    o_ref, buf, sem, *, tm):
    i = pl.program_id(0)
    nsteps = pl.num_programs(0)
    slot = i % 2

    def issue_tile(d_ref, s):
        for r in range(tm):
            for k in range(TOP_K):
                pltpu.make_async_copy(y_hbm.at[pl.ds(d_ref[0, k * tm + r], 1)], buf.at[s, k, pl.ds(r, 1)],
                                      sem.at[s]).start(priority=(r + k) % 2)

    @pl.when(i == 0)
    def _():
        issue_tile(dc_ref, 0)

    @pl.when(i + 1 < nsteps)
    def _():
        issue_tile(dn_ref, 1 - slot)

    hb = h_ref[...]
    shared = _dot((_silu(_dot(hb, wg_ref[...])) * _dot(hb, wu_ref[...])).astype(BF16), wd_ref[...])

    for k in range(TOP_K):
        pltpu.make_async_copy(y_hbm.at[pl.ds(0, tm)], buf.at[slot, k], sem.at[slot]).wait()

    gt = gt_ref[...]
    acc = shared
    for k in range(TOP_K):
        acc = acc + gt[:, k:k + 1] * _tiles_to_rows(buf[slot, k]).astype(F32)
    o_ref[...] = x1_ref[...] + gf_ref[...] * (_rms(acc) * pn_ref[...])


def _combine(y_sorted, dest3, gate_t, h2b, x1, wgs, wus, wds, gate_f, pnf, seq, tm):
    t, d = x1.shape
    ds_ = wgs.shape[-1]
    per = seq // tm
    nt = t // tm
    smem = functools.partial(pl.BlockSpec, memory_space=pltpu.SMEM)
    row = pl.BlockSpec((tm, d), lambda i: (i, 0))
    return pl.pallas_call(
        functools.partial(_combine_kernel, tm=tm),
        out_shape=jax.ShapeDtypeStruct((t, d), F32),
        grid=(nt,),
        in_specs=[smem((None, 1, TOP_K * tm), lambda i: (i, 0, 0)),
                  smem((None, 1, TOP_K * tm), lambda i: (jnp.minimum(i + 1, nt - 1), 0, 0)),
                  pl.BlockSpec(memory_space=pl.ANY),
                  pl.BlockSpec((tm, TOP_K), lambda i: (i, 0)),
                  row, row,
                  pl.BlockSpec((d, ds_), lambda i: (0, 0)),
                  pl.BlockSpec((d, ds_), lambda i: (0, 0)),
                  pl.BlockSpec((ds_, d), lambda i: (0, 0)),
                  pl.BlockSpec((None, 1, d), lambda i: (i // per, 0, 0)),
                  pl.BlockSpec((1, d), lambda i: (0, 0))],
        out_specs=row,
        scratch_shapes=[pltpu.VMEM((2, TOP_K, tm) + y_sorted.shape[1:], y_sorted.dtype),
                        pltpu.SemaphoreType.DMA((2,))],
        compiler_params=_params(("arbitrary",), 48),
        name="combine",
    )(dest3, dest3, y_sorted, gate_t, h2b, x1, wgs, wus, wds, gate_f, pnf.reshape(1, d))


def _layer(x, c, lb, w_ada, b_ada, pre_norm_mix, post_norm_mix, w_in, hgrn_norm, w_branch_attn, w_branch_hgrn,
           w_out, pre_norm_ffn, post_norm_ffn, w_router, router_bias, w_gate_e, w_up_e, w_down_e,
           w_gate_s, w_up_s, w_down_s):
    nb, s, d = x.shape
    t = nb * s
    x2 = x.reshape(t, d)

    mod = _ada(c, w_ada, b_ada).reshape(nb, 6, 1, d)
    shift_m, scale_m, gate_m, shift_f, scale_f, gate_f = (mod[:, k] for k in range(6))

    w_in_bf = w_in.astype(BF16)
    qkv = []
    for g, (_, dil) in enumerate(ATT_GROUPS):
        cols = jnp.concatenate([w_in_bf[:, p * ATT_WIDTH + g * ATT_GW:p * ATT_WIDTH + (g + 1) * ATT_GW]
                                for p in range(3)], axis=1)
        o = _inproj(x2, pre_norm_mix, shift_m, scale_m, cols, nb, s, dil, tn=3 * ATT_GW)
        qkv.append(o.reshape(nb, dil, s // dil, 3 * ATT_GW))
    proj2 = _inproj(x2, pre_norm_mix, shift_m, scale_m, w_in_bf[:, QKV_COLS:], nb, s)
    proj3 = proj2.reshape(nb, s, -1)

    att = [_attention_group(qkv[g], g) for g in range(len(ATT_GROUPS))]
    orec = _hgrn(proj3, lb, hgrn_norm)
    merged = _branch(att, orec, proj2, w_branch_attn.astype(BF16), w_branch_hgrn.astype(BF16), d, s)
    x1, h2t, h2b = _outproj(merged, w_out.astype(BF16), x2, gate_m, post_norm_mix, pre_norm_ffn,
                            shift_f, scale_f, s)

    eid, gate, pos, cnt = _router(h2b, w_router, router_bias)

    counts = cnt[:, 0]
    padded = (counts + MOE_BLOCK - 1) // MOE_BLOCK * MOE_BLOCK
    pad_end = jnp.cumsum(padded)
    pad_start = pad_end - padded
    n_blocks = -(-(t * TOP_K) // MOE_BLOCK) + N_EXPERTS
    onehot = eid[None] == jnp.arange(N_EXPERTS, dtype=jnp.int32)[:, None, None]
    dest = jnp.sum(jnp.where(onehot, pad_start[:, None, None], 0), axis=0) + pos
    block_start = jnp.arange(n_blocks, dtype=jnp.int32) * MOE_BLOCK
    block_expert = jnp.minimum(jnp.sum((block_start[:, None] >= pad_end[None, :]).astype(jnp.int32), axis=1),
                               N_EXPERTS - 1)
    n_used = (pad_end[-1:] // MOE_BLOCK).astype(jnp.int32)

    tm = _pick(s, (128,))
    nt = t // tm
    dest3 = dest.reshape(TOP_K, nt, tm).transpose(1, 0, 2).reshape(nt, 1, TOP_K * tm)
    znum = jnp.concatenate([padded - counts, n_used]).astype(jnp.int32)
    x_sorted = _dispatch(h2t, dest3, (pad_start + counts).astype(jnp.int32), znum, n_blocks * MOE_BLOCK, tm)
    eidx = jnp.arange(N_EXPERTS, dtype=jnp.int32)
    has = counts > 0
    rank = jnp.cumsum(has.astype(jnp.int32)) - 1
    later = (eidx[None, :] > eidx[:, None]) & has[None, :]
    nxt_e = jnp.min(jnp.where(later, eidx[None, :], N_EXPERTS), axis=1)
    nxt_e = jnp.where(nxt_e == N_EXPERTS, -1, nxt_e).astype(jnp.int32)
    oh_b = block_expert[:, None] == eidx[None, :]
    pick = lambda v: jnp.sum(jnp.where(oh_b, v[None, :], 0), axis=1).astype(jnp.int32)
    first = ((block_start == pick(pad_start)) & (block_start < pad_end[-1])).astype(jnp.int32)
    y_sorted = _experts(x_sorted, block_expert, n_used, first, pick(rank % 2), pick(nxt_e),
                        w_gate_e, w_up_e, w_down_e)
    out = _combine(y_sorted, dest3, gate.T, h2b, x1, w_gate_s.astype(BF16), w_up_s.astype(BF16),
                   w_down_s.astype(BF16), gate_f, post_norm_ffn, s, tm)
    return out.reshape(nb, s, d)


def kernel(x, c, w_ada, b_ada, pre_norm_mix, post_norm_mix, w_in, hgrn_lb_logits, hgrn_norm, w_branch_attn,
           w_branch_hgrn, w_out, pre_norm_ffn, post_norm_ffn, w_router, router_bias, w_gate_e, w_up_e, w_down_e,
           w_gate_s, w_up_s, w_down_s):
    lb_table = jnp.cumsum(jax.nn.softmax(hgrn_lb_logits.astype(F32), axis=0), axis=0)
    depth = w_ada.shape[0]
    for l in range(depth):
        x = _layer(x, c, lb_table[l], w_ada[l], b_ada[l], pre_norm_mix[l], post_norm_mix[l], w_in[l],
                   hgrn_norm[l], w_branch_attn[l], w_branch_hgrn[l], w_out[l], pre_norm_ffn[l],
                   post_norm_ffn[l], w_router[l], router_bias[l], w_gate_e[l], w_up_e[l], w_down_e[l],
                   w_gate_s[l], w_up_s[l], w_down_s[l])
    return x
```

```python
import functools

import jax
import jax.numpy as jnp
from jax import lax
from jax.experimental import pallas as pl
from jax.experimental.pallas import tpu as pltpu

F32 = jnp.float32
BF16 = jnp.bfloat16

HEAD_DIM = 128
ATT_GROUPS = ((128, 1), (512, 4), (2048, 16))
ATT_HPG = 4
ATT_HEADS = ATT_HPG * len(ATT_GROUPS)
ATT_WIDTH = ATT_HEADS * HEAD_DIM
ATT_GW = ATT_HPG * HEAD_DIM
ATT_BLOCK = 128
ATT_OUT_COLS = ATT_GW + HEAD_DIM
LSE_LANES = HEAD_DIM // ATT_HPG
HGRN_HEADS = 8
HGRN_D = 128
HGRN_WIDTH = HGRN_HEADS * HGRN_D
HGRN_CHUNK = 64
HGRN_DIRECT = 8
N_EXPERTS = 64
N_GROUPS = 8
TOPK_GROUPS = 4
TOP_K = 8
ROUTED_SCALE = 2.5
MOE_BLOCK = 512
NORM_EPS = 1e-6
V7X_VMEM_LIMIT = 56 * 1024 * 1024

QKV_COLS = 3 * ATT_WIDTH
RQ_OFF = 0
RF_OFF = RQ_OFF + HGRN_WIDTH
RI_OFF = RF_OFF + HGRN_WIDTH
RG_OFF = RI_OFF + HGRN_WIDTH
GATE_OFF = RG_OFF + HGRN_WIDTH


def _pick(n, cands):
    for c in cands:
        if n % c == 0:
            return c
    raise ValueError(f"no tile of {cands} divides {n}")


def _params(sem, vmem_mib=None):
    kw = dict(dimension_semantics=sem)
    if vmem_mib is not None:
        kw["vmem_limit_bytes"] = min(vmem_mib * 1024 * 1024, V7X_VMEM_LIMIT)
    return pltpu.CompilerParams(**kw)


def _dot(a, b):
    return jnp.dot(a, b, preferred_element_type=F32)


def _dot_nt(a, b):
    return lax.dot_general(a, b, (((1,), (1,)), ((), ())), preferred_element_type=F32)


def _dot_tn(a, b):
    return lax.dot_general(a, b, (((0,), (0,)), ((), ())), preferred_element_type=F32)


def _rms(y):
    return y * lax.rsqrt(jnp.mean(y * y, axis=-1, keepdims=True) + NORM_EPS)


def _silu(a):
    return a * jax.nn.sigmoid(a)


ADA_KCHUNK = 128


def _ada_kernel(ct_ref, w_ref, b_ref, o_ref, *, nb, d):
    a = _silu(ct_ref[...])
    tn = w_ref.shape[1]
    accs = [jnp.zeros((8, tn), F32) for _ in range(nb)]
    for kc in range(d // ADA_KCHUNK):
        wc = w_ref[kc * ADA_KCHUNK:(kc + 1) * ADA_KCHUNK, :]
        ac = a[kc * ADA_KCHUNK:(kc + 1) * ADA_KCHUNK, :]
        for b in range(nb):
            p = ac[:, b:b + 1] * wc
            accs[b] = accs[b] + p.reshape(ADA_KCHUNK // 8, 8, tn).sum(axis=0)
    rows = [jnp.sum(acc, axis=0, keepdims=True) for acc in accs]
    o_ref[...] = jnp.concatenate(rows, axis=0) + b_ref[...]


def _ada(c, w_ada, b_ada):
    nb, d = c.shape
    n = w_ada.shape[1]
    tn = _pick(n, (512, 256, 128))
    return pl.pallas_call(
        functools.partial(_ada_kernel, nb=nb, d=d),
        out_shape=jax.ShapeDtypeStruct((nb, n), F32),
        grid=(n // tn,),
        in_specs=[pl.BlockSpec((d, nb), lambda j: (0, 0)),
                  pl.BlockSpec((d, tn), lambda j: (0, j)),
                  pl.BlockSpec((1, tn), lambda j: (0, j))],
        out_specs=pl.BlockSpec((nb, tn), lambda j: (0, j)),
        compiler_params=_params(("arbitrary",), 32),
        name="ada_mod",
    )(c.T, w_ada, b_ada.reshape(1, n))


def _inproj_kernel(x_ref, g_ref, sh_ref, sc_ref, w_ref, o_ref, h_scr, *slab, dil):
    @pl.when(pl.program_id(1) == 0)
    def _():
        y = _rms(x_ref[...]) * g_ref[...]
        h_scr[...] = (y * (1.0 + sc_ref[...]) + sh_ref[...]).astype(BF16)

    o = _dot(h_scr[...], w_ref[...])
    if dil == 1:
        o_ref[...] = o.astype(o_ref.dtype)
    else:
        slab_ref, = slab
        tm, tn = o.shape
        for c in range(tn // 128):
            slab_ref[c] = o[:, c * 128:(c + 1) * 128]
        for r in range(dil):
            for c in range(tn // 128):
                o_ref[r, :, c * 128:(c + 1) * 128] = slab_ref[c, pl.ds(r, tm // dil, stride=dil), :].astype(o_ref.dtype)


def _inproj(x2, gain, shift, scale, w_bf, nb, seq, dil=1, tn=None):
    t, d = x2.shape
    n = w_bf.shape[1]
    tm = _pick(seq, (1024, 512, 256) if tn is None else (512, 256))
    tn = tn or _pick(n, (1024, 512, 256, 128))
    per = seq // tm
    if dil == 1:
        out_shape = jax.ShapeDtypeStruct((t, n), BF16)
        out_spec = pl.BlockSpec((tm, tn), lambda i, j: (i, j))
        scratch = [pltpu.VMEM((tm, d), BF16)]
    else:
        out_shape = jax.ShapeDtypeStruct((nb, dil, seq // dil, n), BF16)
        out_spec = pl.BlockSpec((None, dil, tm // dil, tn), lambda i, j: (i // per, 0, i % per, j))
        scratch = [pltpu.VMEM((tm, d), BF16), pltpu.VMEM((tn // 128, tm, 128), F32)]
    return pl.pallas_call(
        functools.partial(_inproj_kernel, dil=dil),
        out_shape=out_shape,
        grid=(t // tm, n // tn),
        in_specs=[pl.BlockSpec((tm, d), lambda i, j: (i, 0)),
                  pl.BlockSpec((1, d), lambda i, j: (0, 0)),
                  pl.BlockSpec((None, 1, d), lambda i, j: (i // per, 0, 0)),
                  pl.BlockSpec((None, 1, d), lambda i, j: (i // per, 0, 0)),
                  pl.BlockSpec((d, tn), lambda i, j: (0, j))],
        out_specs=out_spec,
        scratch_shapes=scratch,
        compiler_params=_params(("parallel", "arbitrary"), 48),
        name=f"in_proj_d{dil}_n{n}",
    )(x2, gain.reshape(1, d), shift, scale, w_bf)


def _attn_kernel(q_ref, kp_ref, kc_ref, vp_ref, vc_ref, o_ref, *, dil, slopes, scale, qb):
    n = pl.program_id(2)
    qi = lax.broadcasted_iota(jnp.int32, (ATT_BLOCK, ATT_BLOCK), 0)
    ki = lax.broadcasted_iota(jnp.int32, (ATT_BLOCK, ATT_BLOCK), 1)
    jc = qi - ki
    jp = jc + ATT_BLOCK
    first_lim = jnp.where(n > 0, ATT_BLOCK, -1)
    valid_c = jc >= 0
    lane = lax.broadcasted_iota(jnp.int32, (ATT_BLOCK, HEAD_DIM), 1)
    ones = jnp.ones((ATT_BLOCK, HEAD_DIM), BF16)
    for j in range(qb):
        rows = slice(j * ATT_BLOCK, (j + 1) * ATT_BLOCK)
        prow = slice((j - 1) * ATT_BLOCK, j * ATT_BLOCK)
        valid_p = jp <= (first_lim if j == 0 else ATT_BLOCK)
        lse_blk = jnp.zeros((ATT_BLOCK, HEAD_DIM), F32)
        for h in range(ATT_HPG):
            hs = slice(h * HEAD_DIM, (h + 1) * HEAD_DIM)
            q = q_ref[rows, hs]
            k_p = kp_ref[:, hs] if j == 0 else kc_ref[prow, hs]
            v_p = vp_ref[:, hs] if j == 0 else vc_ref[prow, hs]
            bias = slopes[h] * dil
            s_c = _dot_nt(q, kc_ref[rows, hs]) * scale - bias * jc.astype(F32)
            s_p = _dot_nt(q, k_p) * scale - bias * jp.astype(F32)
            s_c = jnp.where(valid_c, s_c, -jnp.inf)
            s_p = jnp.where(valid_p, s_p, -jnp.inf)
            m = jnp.maximum(jnp.max(s_c, axis=-1, keepdims=True), jnp.max(s_p, axis=-1, keepdims=True))
            p_c = jnp.exp(s_c - m).astype(BF16)
            p_p = jnp.exp(s_p - m).astype(BF16)
            oa = (_dot(p_c, jnp.concatenate([vc_ref[rows, hs], ones], axis=1))
                  + _dot(p_p, jnp.concatenate([v_p, ones], axis=1)))
            l = oa[:, HEAD_DIM:HEAD_DIM + 1]
            o_ref[rows, hs] = oa[:, :HEAD_DIM] / l
            lse = m + jnp.log(l)
            lse_blk = jnp.where((lane >= h * LSE_LANES) & (lane < (h + 1) * LSE_LANES), lse, lse_blk)
        o_ref[rows, ATT_GW:] = lse_blk


def _attention_group(qkv, g):
    nb_, dil, l, _ = qkv.shape
    window, dil_ = ATT_GROUPS[g]
    assert dil == dil_ and window // dil == ATT_BLOCK and l % ATT_BLOCK == 0
    qb = _pick(l // ATT_BLOCK, (4, 2, 1))
    slopes = tuple(2.0 ** (-8.0 * (g * ATT_HPG + i + 1) / ATT_HEADS) for i in range(ATT_HPG))

    def spec(part, prev):
        if prev:
            return pl.BlockSpec((None, None, ATT_BLOCK, ATT_GW),
                                lambda b, r, n: (b, r, jnp.maximum(n * qb - 1, 0), part))
        return pl.BlockSpec((None, None, qb * ATT_BLOCK, ATT_GW), lambda b, r, n: (b, r, n, part))

    return pl.pallas_call(
        functools.partial(_attn_kernel, dil=float(dil), slopes=slopes, scale=HEAD_DIM ** -0.5, qb=qb),
        out_shape=jax.ShapeDtypeStruct((nb_, dil, l, ATT_OUT_COLS), F32),
        grid=(nb_, dil, l // (qb * ATT_BLOCK)),
        in_specs=[spec(0, False), spec(1, True), spec(1, False), spec(2, True), spec(2, False)],
        out_specs=pl.BlockSpec((None, None, qb * ATT_BLOCK, ATT_OUT_COLS), lambda b, r, n: (b, r, n, 0)),
        compiler_params=_params(("parallel", "parallel", "arbitrary")),
        name=f"attn_g{g}",
    )(qkv, qkv, qkv, qkv, qkv)


def _hgrn_kernel(q_ref, f_ref, i_ref, g_ref, lb_ref, gn_ref, tri_ref, o_ref, st_ref, sh_ref, *, tb, hp):
    @pl.when(pl.program_id(2) == 0)
    def _():
        st_ref[...] = jnp.zeros_like(st_ref)
        sh_ref[:, :, :, 0:HGRN_DIRECT, :] = jnp.zeros((hp, 3, tb // HGRN_DIRECT, HGRN_DIRECT, HGRN_D), F32)

    for hh in range(hp):
        cols = slice(hh * HGRN_D, (hh + 1) * HGRN_D)
        o_ref[:, cols] = _hgrn_head(q_ref[:, cols], f_ref[:, cols], i_ref[:, cols], g_ref[:, cols],
                                    lb_ref[:, cols], gn_ref[:, cols], tri_ref[...], st_ref.at[hh], sh_ref.at[hh],
                                    tb).astype(o_ref.dtype)


def _hgrn_head(q_in, f_in, i_in, g_in, lb, gn, tri, st_ref, sh_ref, tb):
    c_ = HGRN_CHUNK
    nc = tb // c_
    q = q_in.astype(F32)
    v = i_in.astype(F32)
    f = lb + (1.0 - lb) * jax.nn.sigmoid(f_in.astype(F32))
    kk = 1.0 - f
    lf = jnp.log(f)

    hi = lf.astype(BF16)
    r1 = lf - hi.astype(F32)
    mid = r1.astype(BF16)
    lo = (r1 - mid.astype(F32)).astype(BF16)
    b = _dot(tri, hi) + _dot(tri, mid) + _dot(tri, lo)

    def ref_rows(blk, row):
        b3 = b.reshape(tb // blk, blk, HGRN_D)
        return jnp.broadcast_to(b3[:, row:row + 1, :], (tb // blk, blk, HGRN_D)).reshape(tb, HGRN_D)

    ti = lax.broadcasted_iota(jnp.int32, (c_, c_), 0)
    si = lax.broadcasted_iota(jnp.int32, (c_, c_), 1)
    ssum = jnp.zeros((nc, c_, c_), F32)
    half = c_ // 2
    while half >= HGRN_DIRECT:
        bref = ref_rows(2 * half, half)
        ql = (q * jnp.exp(b - bref)).astype(BF16).reshape(nc, c_, HGRN_D)
        kl = (kk * jnp.exp(bref - b)).astype(BF16).reshape(nc, c_, HGRN_D)
        s_l = jnp.einsum("ctk,csk->cts", ql, kl, preferred_element_type=F32)
        mask = ((ti // (2 * half)) == (si // (2 * half))) & (((ti // half) % 2) == 1) & (((si // half) % 2) == 0)
        ssum = ssum + jnp.where(mask[None], s_l, 0.0)
        half //= 2
    v3 = v.astype(BF16).reshape(nc, c_, HGRN_D)
    o_acc = jnp.einsum("cts,csv->ctv", ssum.astype(BF16), v3, preferred_element_type=F32).reshape(tb, HGRN_D)

    nt8 = tb // HGRN_DIRECT
    for plane, val in enumerate((kk, f, v)):
        sh_ref[plane, :, HGRN_DIRECT:, :] = val.reshape(nt8, HGRN_DIRECT, HGRN_D)

    def shifted(plane, lag):
        st = HGRN_DIRECT - lag
        return sh_ref[plane, :, st:st + HGRN_DIRECT, :].reshape(tb, HGRN_D)

    ones = jnp.ones((HGRN_D, HGRN_D), BF16)
    decay = None
    for lag in range(HGRN_DIRECT):
        if lag == 0:
            w = q * kk
        else:
            decay = f if lag == 1 else decay * shifted(1, lag - 1)
            w = q * shifted(0, lag) * decay
        o_acc = o_acc + _dot(w.astype(BF16), ones) * (v if lag == 0 else shifted(2, lag))

    blast = ref_rows(c_, c_ - 1)
    qe = (q * jnp.exp(b)).astype(BF16)
    kt = (kk * jnp.exp(blast - b)).astype(BF16)
    vb = v.astype(BF16)
    st_t = st_ref[...]
    inter = []
    for c in range(nc):
        rows = slice(c * c_, (c + 1) * c_)
        inter.append(_dot_nt(qe[rows], st_t.astype(BF16)))
        dec = jnp.exp(blast[c * c_:c * c_ + 1, :])
        st_t = st_t * dec + _dot_tn(vb[rows], kt[rows])
    st_ref[...] = st_t
    o = o_acc + jnp.concatenate(inter, axis=0)

    return (_rms(o) * gn) * _silu(g_in.astype(F32))


HGRN_HEADS_PER_STEP = 4


def _hgrn(proj3, lb, gnorm):
    nb_, s, nc = proj3.shape
    tb = _pick(s, (512, 256, 128, 64))
    hp = HGRN_HEADS_PER_STEP
    w = hp * HGRN_D
    idx = jnp.arange(tb)
    tri = ((idx[:, None] // HGRN_CHUNK == idx[None, :] // HGRN_CHUNK) & (idx[None, :] <= idx[:, None])).astype(BF16)

    def spec(off):
        cb = off // w
        return pl.BlockSpec((None, tb, w), lambda b, h, n: (b, n, cb + h))

    vec = pl.BlockSpec((1, w), lambda b, h, n: (0, h))
    out = pl.pallas_call(
        functools.partial(_hgrn_kernel, tb=tb, hp=hp),
        out_shape=jax.ShapeDtypeStruct((nb_, s, HGRN_WIDTH), BF16),
        grid=(nb_, HGRN_HEADS // hp, s // tb),
        in_specs=[spec(RQ_OFF), spec(RF_OFF), spec(RI_OFF), spec(RG_OFF), vec, vec,
                  pl.BlockSpec((tb, tb), lambda b, h, n: (0, 0))],
        out_specs=pl.BlockSpec((None, tb, w), lambda b, h, n: (b, n, h)),
        scratch_shapes=[pltpu.VMEM((hp, HGRN_D, HGRN_D), F32),
                        pltpu.VMEM((hp, 3, tb // HGRN_DIRECT, 2 * HGRN_DIRECT, HGRN_D), F32)],
        compiler_params=_params(("parallel", "parallel", "arbitrary"), 32),
        name="hgrn2",
    )(proj3, proj3, proj3, proj3, lb.reshape(1, HGRN_WIDTH), gnorm.reshape(1, HGRN_WIDTH), tri)
    return out.reshape(nb_ * s, HGRN_WIDTH)


def _branch_kernel(a0_ref, a1_ref, a2_ref, r_ref, ga_ref, gb_ref, wa_ref, wr_ref, o_ref, att_scr, nat_scr,
                   *, dils):
    tm = att_scr.shape[0]
    nslab = ATT_OUT_COLS // 128
    for gi, (a_ref, dil) in enumerate(zip((a1_ref, a2_ref), dils)):
        for r in range(dil):
            for c in range(nslab):
                nat_scr[gi, c, pl.ds(r, tm // dil, stride=dil), :] = a_ref[r, :, c * 128:(c + 1) * 128]

    def head(g, h):
        return a0_ref[:, h * HEAD_DIM:(h + 1) * HEAD_DIM] if g == 0 else nat_scr[g - 1, h]

    def lse(g, h):
        if g == 0:
            return a0_ref[:, ATT_GW + h * LSE_LANES:ATT_GW + h * LSE_LANES + 1]
        return nat_scr[g - 1, ATT_HPG, :, h * LSE_LANES:h * LSE_LANES + 1]

    for h in range(ATT_HPG):
        ls = [lse(g, h) for g in range(3)]
        m = jnp.maximum(jnp.maximum(ls[0], ls[1]), ls[2])
        es = [jnp.exp(l - m) for l in ls]
        den = es[0] + es[1] + es[2]
        o = ((es[0] / den) * head(0, h) + (es[1] / den) * head(1, h)) + (es[2] / den) * head(2, h)
        att_scr[:, h * HEAD_DIM:(h + 1) * HEAD_DIM] = o.astype(BF16)

    a = _dot(att_scr[...], wa_ref[...])
    r = _dot(r_ref[...], wr_ref[...])
    o_ref[...] = (jax.nn.sigmoid(ga_ref[...].astype(F32)) * a
                  + jax.nn.sigmoid(gb_ref[...].astype(F32)) * r).astype(o_ref.dtype)


def _branch(att, orec, proj2, wa_bf, wr_bf, d, seq):
    t = orec.shape[0]
    tm = _pick(seq, (256,))
    per = seq // tm
    assert GATE_OFF % d == 0
    ga0 = GATE_OFF // d
    dils = tuple(dil for _, dil in ATT_GROUPS[1:])

    def aspec(dil):
        return pl.BlockSpec((None, dil, tm // dil, ATT_OUT_COLS), lambda i: (i // per, 0, i % per, 0))

    return pl.pallas_call(
        functools.partial(_branch_kernel, dils=dils),
        out_shape=jax.ShapeDtypeStruct((t, d), BF16),
        grid=(t // tm,),
        in_specs=[pl.BlockSpec((tm, ATT_OUT_COLS), lambda i: (i, 0)), aspec(dils[0]), aspec(dils[1]),
                  pl.BlockSpec((tm, HGRN_WIDTH), lambda i: (i, 0)),
                  pl.BlockSpec((tm, d), lambda i: (i, ga0)),
                  pl.BlockSpec((tm, d), lambda i: (i, ga0 + 1)),
                  pl.BlockSpec((ATT_GW, d), lambda i: (0, 0)),
                  pl.BlockSpec((HGRN_WIDTH, d), lambda i: (0, 0))],
        out_specs=pl.BlockSpec((tm, d), lambda i: (i, 0)),
        scratch_shapes=[pltpu.VMEM((tm, ATT_GW), BF16),
                        pltpu.VMEM((len(dils), ATT_OUT_COLS // 128, tm, 128), F32)],
        compiler_params=_params(("parallel",), 48),
        name="branch_merge",
    )(att[0].reshape(t, ATT_OUT_COLS), att[1], att[2], orec, proj2, proj2, wa_bf, wr_bf)


def _rows_to_tiles(a):
    m, n = a.shape
    return jnp.swapaxes(jnp.stack([a[:, c * 128:(c + 1) * 128] for c in range(n // 128)], axis=0), 0, 1)


def _tiles_to_rows(a):
    at = jnp.swapaxes(a, 0, 1)
    return jnp.concatenate([at[c] for c in range(a.shape[1])], axis=1)


def _outproj_kernel(m_ref, w_ref, x_ref, gm_ref, pnm_ref, pnf_ref, sh_ref, sc_ref, x1_ref, ht_ref, hb_ref):
    y = _dot(m_ref[...], w_ref[...])
    x1 = x_ref[...] + gm_ref[...] * (_rms(y) * pnm_ref[...])
    x1_ref[...] = x1
    h2 = ((_rms(x1) * pnf_ref[...]) * (1.0 + sc_ref[...]) + sh_ref[...]).astype(BF16)
    hb_ref[...] = h2
    ht_ref[...] = _rows_to_tiles(h2)


def _outproj(merged, w_bf, x2, gate_m, pnm, pnf, shift_f, scale_f, seq):
    t, d = x2.shape
    tm = _pick(seq, (256, 128))
    per = seq // tm
    row = pl.BlockSpec((tm, d), lambda i: (i, 0))
    vec = pl.BlockSpec((1, d), lambda i: (0, 0))
    bvec = pl.BlockSpec((None, 1, d), lambda i: (i // per, 0, 0))
    return pl.pallas_call(
        _outproj_kernel,
        out_shape=(jax.ShapeDtypeStruct((t, d), F32), jax.ShapeDtypeStruct((t, d // 128, 128), BF16),
                   jax.ShapeDtypeStruct((t, d), BF16)),
        grid=(t // tm,),
        in_specs=[row, pl.BlockSpec((d, d), lambda i: (0, 0)), row, bvec, vec, vec, bvec, bvec],
        out_specs=(row, pl.BlockSpec((tm, d // 128, 128), lambda i: (i, 0, 0)), row),
        compiler_params=_params(("parallel",), 48),
        name="out_proj",
    )(merged, w_bf, x2, gate_m, pnm.reshape(1, d), pnf.reshape(1, d), shift_f, scale_f)


def _router_kernel(h_ref, w_ref, bias_ref, up_ref, eid_ref, gate_ref, pos_ref, cnt_ref, carry_ref, *, tr):
    @pl.when(pl.program_id(0) == 0)
    def _():
        carry_ref[...] = jnp.zeros_like(carry_ref)

    per_group = N_EXPERTS // N_GROUPS
    sig = jax.nn.sigmoid(_dot_nt(w_ref[...], h_ref[...]))
    choice = sig + bias_ref[...]
    eidx = lax.broadcasted_iota(jnp.int32, (N_EXPERTS, tr), 0)

    c3 = choice.reshape(N_GROUPS, per_group, tr)
    sub = lax.broadcasted_iota(jnp.int32, (N_GROUPS, per_group, tr), 1)
    m1 = jnp.max(c3, axis=1, keepdims=True)
    first = jnp.min(jnp.where(c3 == m1, sub, per_group), axis=1, keepdims=True)
    m2 = jnp.max(jnp.where(sub == first, -jnp.inf, c3), axis=1, keepdims=True)
    gs = (m1 + m2).reshape(N_GROUPS, tr)

    gidx = lax.broadcasted_iota(jnp.int32, (N_GROUPS, tr), 0)
    grank = jnp.zeros((N_GROUPS, tr), jnp.int32)
    for g in range(N_GROUPS):
        row = gs[g:g + 1, :]
        grank = grank + ((row > gs) | ((row == gs) & (gidx > g))).astype(jnp.int32)
    gsel = jnp.where(grank < TOPK_GROUPS, 1.0, 0.0)
    emask = jnp.broadcast_to(gsel.reshape(N_GROUPS, 1, tr), (N_GROUPS, per_group, tr)).reshape(N_EXPERTS, tr)
    cm = jnp.where(emask > 0.5, choice, -jnp.inf)

    rank = jnp.zeros((N_EXPERTS, tr), jnp.int32)
    for e in range(N_EXPERTS):
        row = cm[e:e + 1, :]
        rank = rank + ((row > cm) | ((row == cm) & (eidx > e))).astype(jnp.int32)
    sel = rank < TOP_K

    denom = jnp.sum(jnp.where(sel, sig, 0.0), axis=0, keepdims=True)
    gate_full = sig / denom * ROUTED_SCALE

    sel_b = jnp.where(sel, 1.0, 0.0).astype(BF16)
    carry = carry_ref[...]
    cum = _dot(sel_b, up_ref[...]) + jnp.concatenate([carry] * (tr // 128), axis=1)
    carry_new = carry + _dot(sel_b, jnp.ones((tr, 128), BF16))
    carry_ref[...] = carry_new
    cnt_ref[...] = carry_new.astype(jnp.int32)
    posi = cum.astype(jnp.int32)

    eids, gates, poss = [], [], []
    for r in range(TOP_K):
        hit = rank == r
        eids.append(jnp.sum(jnp.where(hit, eidx, 0), axis=0, keepdims=True))
        gates.append(jnp.sum(jnp.where(hit, gate_full, 0.0), axis=0, keepdims=True))
        poss.append(jnp.sum(jnp.where(hit, posi, 0), axis=0, keepdims=True))
    eid_ref[...] = jnp.concatenate(eids, axis=0)
    gate_ref[...] = jnp.concatenate(gates, axis=0)
    pos_ref[...] = jnp.concatenate(poss, axis=0)


def _router(h2b, w_router, router_bias):
    t, d = h2b.shape
    tr = _pick(t, (256, 128))
    idx = jnp.arange(tr)
    upper = (idx[:, None] < idx[None, :]).astype(BF16)
    kout = pl.BlockSpec((TOP_K, tr), lambda i: (0, i))
    return pl.pallas_call(
        functools.partial(_router_kernel, tr=tr),
        out_shape=(jax.ShapeDtypeStruct((TOP_K, t), jnp.int32), jax.ShapeDtypeStruct((TOP_K, t), F32),
                   jax.ShapeDtypeStruct((TOP_K, t), jnp.int32), jax.ShapeDtypeStruct((N_EXPERTS, 128), jnp.int32)),
        grid=(t // tr,),
        in_specs=[pl.BlockSpec((tr, d), lambda i: (i, 0)),
                  pl.BlockSpec((N_EXPERTS, d), lambda i: (0, 0)),
                  pl.BlockSpec((N_EXPERTS, 1), lambda i: (0, 0)),
                  pl.BlockSpec((tr, tr), lambda i: (0, 0))],
        out_specs=(kout, kout, kout, pl.BlockSpec((N_EXPERTS, 128), lambda i: (0, 0))),
        scratch_shapes=[pltpu.VMEM((N_EXPERTS, 128), F32)],
        compiler_params=_params(("arbitrary",), 32),
        name="router",
    )(h2b, w_router.T.astype(BF16), router_bias.reshape(N_EXPERTS, 1).astype(F32), upper)


def _dispatch_kernel(zs_ref, zn_ref, d_ref, h_ref, xs_hbm, zero_scr, sem, zsem, *, tm, n_blocks):
    i = pl.program_id(0)

    for r in range(tm):
        for k in range(TOP_K):
            pltpu.make_async_copy(h_ref.at[pl.ds(r, 1)], xs_hbm.at[pl.ds(d_ref[0, k * tm + r], 1)],
                                  sem).start(priority=(r + k) % 2)

    @pl.when(i == 0)
    def _():
        zero_scr[...] = jnp.zeros_like(zero_scr)
        for e in range(N_EXPERTS):
            def zbody(u, carry, e=e):
                pltpu.make_async_copy(zero_scr.at[pl.ds(0, 1)], xs_hbm.at[pl.ds(zs_ref[e] + u, 1)], zsem).start()
                return carry
            lax.fori_loop(0, zn_ref[e], zbody, 0)
        for e in range(N_EXPERTS):
            def zwait(u, carry):
                pltpu.make_async_copy(zero_scr.at[pl.ds(0, 1)], xs_hbm.at[pl.ds(0, 1)], zsem).wait()
                return carry
            lax.fori_loop(0, zn_ref[e], zwait, 0)

        def tbody(blk, carry):
            pltpu.make_async_copy(zero_scr, xs_hbm.at[pl.ds(blk * MOE_BLOCK, MOE_BLOCK)], zsem).start()
            return carry
        lax.fori_loop(zn_ref[N_EXPERTS], n_blocks, tbody, 0)

        def twait(blk, carry):
            pltpu.make_async_copy(zero_scr, xs_hbm.at[pl.ds(0, MOE_BLOCK)], zsem).wait()
            return carry
        lax.fori_loop(zn_ref[N_EXPERTS], n_blocks, twait, 0)

    for k in range(TOP_K):
        pltpu.make_async_copy(h_ref, xs_hbm.at[pl.ds(0, tm)], sem).wait()


def _dispatch(h2t, dest3, zstart, znum, n_slots, tm):
    t, nch, _ = h2t.shape
    grid_spec = pltpu.PrefetchScalarGridSpec(
        num_scalar_prefetch=2,
        grid=(t // tm,),
        in_specs=[pl.BlockSpec((None, 1, TOP_K * tm), lambda i, zs, zn: (i, 0, 0), memory_space=pltpu.SMEM),
                  pl.BlockSpec((tm, nch, 128), lambda i, zs, zn: (i, 0, 0))],
        out_specs=pl.BlockSpec(memory_space=pl.ANY),
        scratch_shapes=[pltpu.VMEM((MOE_BLOCK, nch, 128), h2t.dtype), pltpu.SemaphoreType.DMA(()),
                        pltpu.SemaphoreType.DMA(())],
    )
    return pl.pallas_call(
        functools.partial(_dispatch_kernel, tm=tm, n_blocks=n_slots // MOE_BLOCK),
        out_shape=jax.ShapeDtypeStruct((n_slots, nch, 128), h2t.dtype),
        grid_spec=grid_spec,
        compiler_params=_params(("arbitrary",), 32),
        name="dispatch",
    )(zstart, znum, dest3, h2t)


def _expert_kernel(be_ref, nu_ref, first_ref, par_ref, nxt_ref, x_ref, wg_hbm, wu_hbm, wd_hbm, o_ref,
                   wg_f, wu_f, wd_f, wg_bf, wu_bf, wd_bf, sem):
    i = pl.program_id(0)

    def weight_copies(e, s):
        return [pltpu.make_async_copy(src.at[e], dst.at[s], sem.at[s])
                for src, dst in ((wg_hbm, wg_f), (wu_hbm, wu_f), (wd_hbm, wd_f))]

    @pl.when(i == 0)
    def _():
        for cp in weight_copies(be_ref[0], 0):
            cp.start()

    @pl.when(first_ref[i] == 1)
    def _():
        p = par_ref[i]
        for cp in weight_copies(be_ref[i], p):
            cp.wait()
        wg_bf[...] = wg_f[p].astype(BF16)
        wu_bf[...] = wu_f[p].astype(BF16)
        wd_bf[...] = wd_f[p].astype(BF16)

        @pl.when(nxt_ref[i] >= 0)
        def _():
            for cp in weight_copies(nxt_ref[i], 1 - p):
                cp.start()

    @pl.when(i < nu_ref[0])
    def _():
        xb = _tiles_to_rows(x_ref[...])
        hid = _silu(_dot(xb, wg_bf[...])) * _dot(xb, wu_bf[...])
        o_ref[...] = _rows_to_tiles(_dot(hid.astype(BF16), wd_bf[...]).astype(o_ref.dtype))

    @pl.when(i >= nu_ref[0])
    def _():
        o_ref[...] = jnp.zeros_like(o_ref)


def _experts(x_sorted, block_expert, n_used, first, par, nxt, wg, wu, wd):
    n_slots, nch, _ = x_sorted.shape
    d = nch * 128
    n_blocks = block_expert.shape[0]
    de = wg.shape[-1]
    hbm = pl.BlockSpec(memory_space=pl.ANY)
    grid_spec = pltpu.PrefetchScalarGridSpec(
        num_scalar_prefetch=5,
        grid=(n_blocks,),
        in_specs=[pl.BlockSpec((MOE_BLOCK, nch, 128), lambda i, be, nu, *_: (jnp.minimum(i, nu[0] - 1), 0, 0)),
                  hbm, hbm, hbm],
        out_specs=pl.BlockSpec((MOE_BLOCK, nch, 128), lambda i, *_: (i, 0, 0)),
        scratch_shapes=[pltpu.VMEM((2, d, de), F32), pltpu.VMEM((2, d, de), F32), pltpu.VMEM((2, de, d), F32),
                        pltpu.VMEM((d, de), BF16), pltpu.VMEM((d, de), BF16), pltpu.VMEM((de, d), BF16),
                        pltpu.SemaphoreType.DMA((2,))],
    )
    return pl.pallas_call(
        _expert_kernel,
        out_shape=jax.ShapeDtypeStruct((n_slots, nch, 128), BF16),
        grid_spec=grid_spec,
        compiler_params=_params(("arbitrary",), 56),
        name="experts",
    )(block_expert, n_used, first, par, nxt, x_sorted, wg, wu, wd)


def _combine_kernel(dc_ref, dn_ref, y_hbm, gt_ref, h_ref, x1_ref, wg_ref, wu_ref, wd_ref, gf_ref, pn_ref,
                    o_ref, buf, sem, *, tm):
    i = pl.program_id(0)
    nsteps = pl.num_programs(0)
    slot = i % 2

    def issue_tile(d_ref, s):
        for r in range(tm):
            for k in range(TOP_K):
                pltpu.make_async_copy(y_hbm.at[pl.ds(d_ref[0, k * tm + r], 1)], buf.at[s, k, pl.ds(r, 1)],
                                      sem.at[s]).start(priority=(r + k) % 2)

    @pl.when(i == 0)
    def _():
        issue_tile(dc_ref, 0)

    @pl.when(i + 1 < nsteps)
    def _():
        issue_tile(dn_ref, 1 - slot)

    hb = h_ref[...]
    shared = _dot((_silu(_dot(hb, wg_ref[...])) * _dot(hb, wu_ref[...])).astype(BF16), wd_ref[...])

    for k in range(TOP_K):
        pltpu.make_async_copy(y_hbm.at[pl.ds(0, tm)], buf.at[slot, k], sem.at[slot]).wait()

    gt = gt_ref[...]
    acc = shared
    for k in range(TOP_K):
        acc = acc + gt[:, k:k + 1] * _tiles_to_rows(buf[slot, k]).astype(F32)
    o_ref[...] = x1_ref[...] + gf_ref[...] * (_rms(acc) * pn_ref[...])


def _combine(y_sorted, dest3, gate_t, h2b, x1, wgs, wus, wds, gate_f, pnf, seq, tm):
    t, d = x1.shape
    ds_ = wgs.shape[-1]
    per = seq // tm
    nt = t // tm
    smem = functools.partial(pl.BlockSpec, memory_space=pltpu.SMEM)
    row = pl.BlockSpec((tm, d), lambda i: (i, 0))
    return pl.pallas_call(
        functools.partial(_combine_kernel, tm=tm),
        out_shape=jax.ShapeDtypeStruct((t, d), F32),
        grid=(nt,),
        in_specs=[smem((None, 1, TOP_K * tm), lambda i: (i, 0, 0)),
                  smem((None, 1, TOP_K * tm), lambda i: (jnp.minimum(i + 1, nt - 1), 0, 0)),
                  pl.BlockSpec(memory_space=pl.ANY),
                  pl.BlockSpec((tm, TOP_K), lambda i: (i, 0)),
                  row, row,
                  pl.BlockSpec((d, ds_), lambda i: (0, 0)),
                  pl.BlockSpec((d, ds_), lambda i: (0, 0)),
                  pl.BlockSpec((ds_, d), lambda i: (0, 0)),
                  pl.BlockSpec((None, 1, d), lambda i: (i // per, 0, 0)),
                  pl.BlockSpec((1, d), lambda i: (0, 0))],
        out_specs=row,
        scratch_shapes=[pltpu.VMEM((2, TOP_K, tm) + y_sorted.shape[1:], y_sorted.dtype),
                        pltpu.SemaphoreType.DMA((2,))],
        compiler_params=_params(("arbitrary",), 48),
        name="combine",
    )(dest3, dest3, y_sorted, gate_t, h2b, x1, wgs, wus, wds, gate_f, pnf.reshape(1, d))


def _layer(x, c, lb, w_ada, b_ada, pre_norm_mix, post_norm_mix, w_in, hgrn_norm, w_branch_attn, w_branch_hgrn,
           w_out, pre_norm_ffn, post_norm_ffn, w_router, router_bias, w_gate_e, w_up_e, w_down_e,
           w_gate_s, w_up_s, w_down_s):
    nb, s, d = x.shape
    t = nb * s
    x2 = x.reshape(t, d)

    mod = _ada(c, w_ada, b_ada).reshape(nb, 6, 1, d)
    shift_m, scale_m, gate_m, shift_f, scale_f, gate_f = (mod[:, k] for k in range(6))

    qkv = []
    for g, (_, dil) in enumerate(ATT_GROUPS):
        cols = jnp.concatenate([w_in[:, p * ATT_WIDTH + g * ATT_GW:p * ATT_WIDTH + (g + 1) * ATT_GW]
                                for p in range(3)], axis=1).astype(BF16)
        o = _inproj(x2, pre_norm_mix, shift_m, scale_m, cols, nb, s, dil, tn=3 * ATT_GW)
        qkv.append(o.reshape(nb, dil, s // dil, 3 * ATT_GW))
    proj2 = _inproj(x2, pre_norm_mix, shift_m, scale_m, w_in[:, QKV_COLS:].astype(BF16), nb, s)
    proj3 = proj2.reshape(nb, s, -1)

    att = [_attention_group(qkv[g], g) for g in range(len(ATT_GROUPS))]
    orec = _hgrn(proj3, lb, hgrn_norm)
    merged = _branch(att, orec, proj2, w_branch_attn.astype(BF16), w_branch_hgrn.astype(BF16), d, s)
    x1, h2t, h2b = _outproj(merged, w_out.astype(BF16), x2, gate_m, post_norm_mix, pre_norm_ffn,
                            shift_f, scale_f, s)

    eid, gate, pos, cnt = _router(h2b, w_router, router_bias)

    counts = cnt[:, 0]
    padded = (counts + MOE_BLOCK - 1) // MOE_BLOCK * MOE_BLOCK
    pad_end = jnp.cumsum(padded)
    pad_start = pad_end - padded
    n_blocks = -(-(t * TOP_K) // MOE_BLOCK) + N_EXPERTS
    onehot = eid[None] == jnp.arange(N_EXPERTS, dtype=jnp.int32)[:, None, None]
    dest = jnp.sum(jnp.where(onehot, pad_start[:, None, None], 0), axis=0) + pos
    block_start = jnp.arange(n_blocks, dtype=jnp.int32) * MOE_BLOCK
    block_expert = jnp.minimum(jnp.sum((block_start[:, None] >= pad_end[None, :]).astype(jnp.int32), axis=1),
                               N_EXPERTS - 1)
    n_used = (pad_end[-1:] // MOE_BLOCK).astype(jnp.int32)

    def tile_major(tile):
        return dest.reshape(TOP_K, t // tile, tile).transpose(1, 0, 2).reshape(t // tile, 1, TOP_K * tile)

    tm = _pick(s, (128,))
    tm_d = _pick(s, (512, 256, 128))
    dest3 = tile_major(tm)
    znum = jnp.concatenate([padded - counts, n_used]).astype(jnp.int32)
    x_sorted = _dispatch(h2t, tile_major(tm_d), (pad_start + counts).astype(jnp.int32), znum,
                         n_blocks * MOE_BLOCK, tm_d)
    eidx = jnp.arange(N_EXPERTS, dtype=jnp.int32)
    has = counts > 0
    rank = jnp.cumsum(has.astype(jnp.int32)) - 1
    later = (eidx[None, :] > eidx[:, None]) & has[None, :]
    nxt_e = jnp.min(jnp.where(later, eidx[None, :], N_EXPERTS), axis=1)
    nxt_e = jnp.where(nxt_e == N_EXPERTS, -1, nxt_e).astype(jnp.int32)
    oh_b = block_expert[:, None] == eidx[None, :]
    pick = lambda v: jnp.sum(jnp.where(oh_b, v[None, :], 0), axis=1).astype(jnp.int32)
    first = ((block_start == pick(pad_start)) & (block_start < pad_end[-1])).astype(jnp.int32)
    y_sorted = _experts(x_sorted, block_expert, n_used, first, pick(rank % 2), pick(nxt_e),
                        w_gate_e, w_up_e, w_down_e)
    out = _combine(y_sorted, dest3, gate.T, h2b, x1, w_gate_s.astype(BF16), w_up_s.astype(BF16),
                   w_down_s.astype(BF16), gate_f, post_norm_ffn, s, tm)
    return out.reshape(nb, s, d)


def kernel(x, c, w_ada, b_ada, pre_norm_mix, post_norm_mix, w_in, hgrn_lb_logits, hgrn_norm, w_branch_attn,
           w_branch_hgrn, w_out, pre_norm_ffn, post_norm_ffn, w_router, router_bias, w_gate_e, w_up_e, w_down_e,
           w_gate_s, w_up_s, w_down_s):
    lb_table = jnp.cumsum(jax.nn.softmax(hgrn_lb_logits.astype(F32), axis=0), axis=0)
    depth = w_ada.shape[0]
    for l in range(depth):
        x = _layer(x, c, lb_table[l], w_ada[l], b_ada[l], pre_norm_mix[l], post_norm_mix[l], w_in[l],
                   hgrn_norm[l], w_branch_attn[l], w_branch_hgrn[l], w_out[l], pre_norm_ffn[l],
                   post_norm_ffn[l], w_router[l], router_bias[l], w_gate_e[l], w_up_e[l], w_down_e[l],
                   w_gate_s[l], w_up_s[l], w_down_s[l])
    return x
```

```python
import functools

import jax
import jax.numpy as jnp
from jax import lax
from jax.experimental import pallas as pl
from jax.experimental.pallas import tpu as pltpu

F32 = jnp.float32
BF16 = jnp.bfloat16

HEAD_DIM = 128
ATT_GROUPS = ((128, 1), (512, 4), (2048, 16))
ATT_HPG = 4
ATT_HEADS = ATT_HPG * len(ATT_GROUPS)
ATT_WIDTH = ATT_HEADS * HEAD_DIM
ATT_GW = ATT_HPG * HEAD_DIM
ATT_BLOCK = 128
ATT_OUT_COLS = ATT_GW + HEAD_DIM
LSE_LANES = HEAD_DIM // ATT_HPG
HGRN_HEADS = 8
HGRN_D = 128
HGRN_WIDTH = HGRN_HEADS * HGRN_D
HGRN_CHUNK = 64
HGRN_DIRECT = 8
N_EXPERTS = 64
N_GROUPS = 8
TOPK_GROUPS = 4
TOP_K = 8
ROUTED_SCALE = 2.5
MOE_BLOCK = 512
NORM_EPS = 1e-6
V7X_VMEM_LIMIT = 56 * 1024 * 1024

QKV_COLS = 3 * ATT_WIDTH
RQ_OFF = 0
RF_OFF = RQ_OFF + HGRN_WIDTH
RI_OFF = RF_OFF + HGRN_WIDTH
RG_OFF = RI_OFF + HGRN_WIDTH
GATE_OFF = RG_OFF + HGRN_WIDTH


def _pick(n, cands):
    for c in cands:
        if n % c == 0:
            return c
    raise ValueError(f"no tile of {cands} divides {n}")


def _params(sem, vmem_mib=None):
    kw = dict(dimension_semantics=sem)
    if vmem_mib is not None:
        kw["vmem_limit_bytes"] = min(vmem_mib * 1024 * 1024, V7X_VMEM_LIMIT)
    return pltpu.CompilerParams(**kw)


def _dot(a, b):
    return jnp.dot(a, b, preferred_element_type=F32)


def _dot_nt(a, b):
    return lax.dot_general(a, b, (((1,), (1,)), ((), ())), preferred_element_type=F32)


def _dot_tn(a, b):
    return lax.dot_general(a, b, (((0,), (0,)), ((), ())), preferred_element_type=F32)


def _rms(y):
    return y * lax.rsqrt(jnp.mean(y * y, axis=-1, keepdims=True) + NORM_EPS)


def _silu(a):
    return a * jax.nn.sigmoid(a)


ADA_KCHUNK = 128


def _ada_kernel(ct_ref, w_ref, b_ref, o_ref, *, nb, d):
    a = _silu(ct_ref[...])
    tn = w_ref.shape[1]
    accs = [jnp.zeros((8, tn), F32) for _ in range(nb)]
    for kc in range(d // ADA_KCHUNK):
        wc = w_ref[kc * ADA_KCHUNK:(kc + 1) * ADA_KCHUNK, :]
        ac = a[kc * ADA_KCHUNK:(kc + 1) * ADA_KCHUNK, :]
        for b in range(nb):
            p = ac[:, b:b + 1] * wc
            accs[b] = accs[b] + p.reshape(ADA_KCHUNK // 8, 8, tn).sum(axis=0)
    rows = [jnp.sum(acc, axis=0, keepdims=True) for acc in accs]
    o_ref[...] = jnp.concatenate(rows, axis=0) + b_ref[...]


def _ada(c, w_ada, b_ada):
    nb, d = c.shape
    n = w_ada.shape[1]
    tn = _pick(n, (512, 256, 128))
    return pl.pallas_call(
        functools.partial(_ada_kernel, nb=nb, d=d),
        out_shape=jax.ShapeDtypeStruct((nb, n), F32),
        grid=(n // tn,),
        in_specs=[pl.BlockSpec((d, nb), lambda j: (0, 0)),
                  pl.BlockSpec((d, tn), lambda j: (0, j)),
                  pl.BlockSpec((1, tn), lambda j: (0, j))],
        out_specs=pl.BlockSpec((nb, tn), lambda j: (0, j)),
        compiler_params=_params(("arbitrary",), 32),
        name="ada_mod",
    )(c.T, w_ada, b_ada.reshape(1, n))


def _inproj_kernel(x_ref, g_ref, sh_ref, sc_ref, w_ref, o_ref, h_scr, *slab, dil):
    @pl.when(pl.program_id(1) == 0)
    def _():
        y = _rms(x_ref[...]) * g_ref[...]
        h_scr[...] = (y * (1.0 + sc_ref[...]) + sh_ref[...]).astype(BF16)

    o = _dot(h_scr[...], w_ref[...])
    if dil == 1:
        o_ref[...] = o.astype(o_ref.dtype)
    else:
        slab_ref, = slab
        tm, tn = o.shape
        for c in range(tn // 128):
            slab_ref[c] = o[:, c * 128:(c + 1) * 128]
        for r in range(dil):
            for c in range(tn // 128):
                o_ref[r, :, c * 128:(c + 1) * 128] = slab_ref[c, pl.ds(r, tm // dil, stride=dil), :].astype(o_ref.dtype)


def _inproj(x2, gain, shift, scale, w_bf, nb, seq, dil=1, tn=None):
    t, d = x2.shape
    n = w_bf.shape[1]
    tm = _pick(seq, (1024, 512, 256) if tn is None else (512, 256))
    tn = tn or _pick(n, (1024, 512, 256, 128))
    per = seq // tm
    if dil == 1:
        out_shape = jax.ShapeDtypeStruct((t, n), BF16)
        out_spec = pl.BlockSpec((tm, tn), lambda i, j: (i, j))
        scratch = [pltpu.VMEM((tm, d), BF16)]
    else:
        out_shape = jax.ShapeDtypeStruct((nb, dil, seq // dil, n), BF16)
        out_spec = pl.BlockSpec((None, dil, tm // dil, tn), lambda i, j: (i // per, 0, i % per, j))
        scratch = [pltpu.VMEM((tm, d), BF16), pltpu.VMEM((tn // 128, tm, 128), F32)]
    return pl.pallas_call(
        functools.partial(_inproj_kernel, dil=dil),
        out_shape=out_shape,
        grid=(t // tm, n // tn),
        in_specs=[pl.BlockSpec((tm, d), lambda i, j: (i, 0)),
                  pl.BlockSpec((1, d), lambda i, j: (0, 0)),
                  pl.BlockSpec((None, 1, d), lambda i, j: (i // per, 0, 0)),
                  pl.BlockSpec((None, 1, d), lambda i, j: (i // per, 0, 0)),
                  pl.BlockSpec((d, tn), lambda i, j: (0, j))],
        out_specs=out_spec,
        scratch_shapes=scratch,
        compiler_params=_params(("parallel", "arbitrary"), 48),
        name=f"in_proj_d{dil}_n{n}",
    )(x2, gain.reshape(1, d), shift, scale, w_bf)


def _attn_kernel(q_ref, kp_ref, kc_ref, vp_ref, vc_ref, o_ref, *, dil, slopes, scale, qb):
    n = pl.program_id(2)
    qi = lax.broadcasted_iota(jnp.int32, (ATT_BLOCK, ATT_BLOCK), 0)
    ki = lax.broadcasted_iota(jnp.int32, (ATT_BLOCK, ATT_BLOCK), 1)
    jc = qi - ki
    jp = jc + ATT_BLOCK
    first_lim = jnp.where(n > 0, ATT_BLOCK, -1)
    valid_c = jc >= 0
    lane = lax.broadcasted_iota(jnp.int32, (ATT_BLOCK, HEAD_DIM), 1)
    ones = jnp.ones((ATT_BLOCK, HEAD_DIM), BF16)
    for j in range(qb):
        rows = slice(j * ATT_BLOCK, (j + 1) * ATT_BLOCK)
        prow = slice((j - 1) * ATT_BLOCK, j * ATT_BLOCK)
        valid_p = jp <= (first_lim if j == 0 else ATT_BLOCK)
        lse_blk = jnp.zeros((ATT_BLOCK, HEAD_DIM), F32)
        for h in range(ATT_HPG):
            hs = slice(h * HEAD_DIM, (h + 1) * HEAD_DIM)
            q = q_ref[rows, hs]
            k_p = kp_ref[:, hs] if j == 0 else kc_ref[prow, hs]
            v_p = vp_ref[:, hs] if j == 0 else vc_ref[prow, hs]
            bias = slopes[h] * dil
            s_c = _dot_nt(q, kc_ref[rows, hs]) * scale - bias * jc.astype(F32)
            s_p = _dot_nt(q, k_p) * scale - bias * jp.astype(F32)
            s_c = jnp.where(valid_c, s_c, -jnp.inf)
            s_p = jnp.where(valid_p, s_p, -jnp.inf)
            m = jnp.maximum(jnp.max(s_c, axis=-1, keepdims=True), jnp.max(s_p, axis=-1, keepdims=True))
            p_c = jnp.exp(s_c - m).astype(BF16)
            p_p = jnp.exp(s_p - m).astype(BF16)
            oa = (_dot(p_c, jnp.concatenate([vc_ref[rows, hs], ones], axis=1))
                  + _dot(p_p, jnp.concatenate([v_p, ones], axis=1)))
            l = oa[:, HEAD_DIM:HEAD_DIM + 1]
            o_ref[rows, hs] = oa[:, :HEAD_DIM] / l
            lse = m + jnp.log(l)
            lse_blk = jnp.where((lane >= h * LSE_LANES) & (lane < (h + 1) * LSE_LANES), lse, lse_blk)
        o_ref[rows, ATT_GW:] = lse_blk


def _attention_group(qkv, g):
    nb_, dil, l, _ = qkv.shape
    window, dil_ = ATT_GROUPS[g]
    assert dil == dil_ and window // dil == ATT_BLOCK and l % ATT_BLOCK == 0
    qb = _pick(l // ATT_BLOCK, (4, 2, 1))
    slopes = tuple(2.0 ** (-8.0 * (g * ATT_HPG + i + 1) / ATT_HEADS) for i in range(ATT_HPG))

    def spec(part, prev):
        if prev:
            return pl.BlockSpec((None, None, ATT_BLOCK, ATT_GW),
                                lambda b, r, n: (b, r, jnp.maximum(n * qb - 1, 0), part))
        return pl.BlockSpec((None, None, qb * ATT_BLOCK, ATT_GW), lambda b, r, n: (b, r, n, part))

    return pl.pallas_call(
        functools.partial(_attn_kernel, dil=float(dil), slopes=slopes, scale=HEAD_DIM ** -0.5, qb=qb),
        out_shape=jax.ShapeDtypeStruct((nb_, dil, l, ATT_OUT_COLS), F32),
        grid=(nb_, dil, l // (qb * ATT_BLOCK)),
        in_specs=[spec(0, False), spec(1, True), spec(1, False), spec(2, True), spec(2, False)],
        out_specs=pl.BlockSpec((None, None, qb * ATT_BLOCK, ATT_OUT_COLS), lambda b, r, n: (b, r, n, 0)),
        compiler_params=_params(("parallel", "parallel", "arbitrary")),
        name=f"attn_g{g}",
    )(qkv, qkv, qkv, qkv, qkv)


def _hgrn_kernel(q_ref, f_ref, i_ref, g_ref, lb_ref, gn_ref, tri_ref, o_ref, st_ref, sh_ref, *, tb, hp):
    @pl.when(pl.program_id(2) == 0)
    def _():
        st_ref[...] = jnp.zeros_like(st_ref)
        sh_ref[:, :, :, 0:HGRN_DIRECT, :] = jnp.zeros((hp, 3, tb // HGRN_DIRECT, HGRN_DIRECT, HGRN_D), F32)

    for hh in range(hp):
        cols = slice(hh * HGRN_D, (hh + 1) * HGRN_D)
        o_ref[:, cols] = _hgrn_head(q_ref[:, cols], f_ref[:, cols], i_ref[:, cols], g_ref[:, cols],
                                    lb_ref[:, cols], gn_ref[:, cols], tri_ref[...], st_ref.at[hh], sh_ref.at[hh],
                                    tb).astype(o_ref.dtype)


def _hgrn_head(q_in, f_in, i_in, g_in, lb, gn, tri, st_ref, sh_ref, tb):
    c_ = HGRN_CHUNK
    nc = tb // c_
    q = q_in.astype(F32)
    v = i_in.astype(F32)
    f = lb + (1.0 - lb) * jax.nn.sigmoid(f_in.astype(F32))
    kk = 1.0 - f
    lf = jnp.log(f)

    hi = lf.astype(BF16)
    r1 = lf - hi.astype(F32)
    mid = r1.astype(BF16)
    lo = (r1 - mid.astype(F32)).astype(BF16)
    b = _dot(tri, hi) + _dot(tri, mid) + _dot(tri, lo)

    def ref_rows(blk, row):
        b3 = b.reshape(tb // blk, blk, HGRN_D)
        return jnp.broadcast_to(b3[:, row:row + 1, :], (tb // blk, blk, HGRN_D)).reshape(tb, HGRN_D)

    ti = lax.broadcasted_iota(jnp.int32, (c_, c_), 0)
    si = lax.broadcasted_iota(jnp.int32, (c_, c_), 1)
    ssum = jnp.zeros((nc, c_, c_), F32)
    half = c_ // 2
    while half >= HGRN_DIRECT:
        bref = ref_rows(2 * half, half)
        ql = (q * jnp.exp(b - bref)).astype(BF16).reshape(nc, c_, HGRN_D)
        kl = (kk * jnp.exp(bref - b)).astype(BF16).reshape(nc, c_, HGRN_D)
        s_l = jnp.einsum("ctk,csk->cts", ql, kl, preferred_element_type=F32)
        mask = ((ti // (2 * half)) == (si // (2 * half))) & (((ti // half) % 2) == 1) & (((si // half) % 2) == 0)
        ssum = ssum + jnp.where(mask[None], s_l, 0.0)
        half //= 2
    v3 = v.astype(BF16).reshape(nc, c_, HGRN_D)
    o_acc = jnp.einsum("cts,csv->ctv", ssum.astype(BF16), v3, preferred_element_type=F32).reshape(tb, HGRN_D)

    nt8 = tb // HGRN_DIRECT
    for plane, val in enumerate((kk, f, v)):
        sh_ref[plane, :, HGRN_DIRECT:, :] = val.reshape(nt8, HGRN_DIRECT, HGRN_D)

    def shifted(plane, lag):
        st = HGRN_DIRECT - lag
        return sh_ref[plane, :, st:st + HGRN_DIRECT, :].reshape(tb, HGRN_D)

    ones = jnp.ones((HGRN_D, HGRN_D), BF16)
    decay = None
    for lag in range(HGRN_DIRECT):
        if lag == 0:
            w = q * kk
        else:
            decay = f if lag == 1 else decay * shifted(1, lag - 1)
            w = q * shifted(0, lag) * decay
        o_acc = o_acc + _dot(w.astype(BF16), ones) * (v if lag == 0 else shifted(2, lag))

    blast = ref_rows(c_, c_ - 1)
    qe = (q * jnp.exp(b)).astype(BF16)
    kt = (kk * jnp.exp(blast - b)).astype(BF16)
    vb = v.astype(BF16)
    st_t = st_ref[...]
    inter = []
    for c in range(nc):
        rows = slice(c * c_, (c + 1) * c_)
        inter.append(_dot_nt(qe[rows], st_t.astype(BF16)))
        dec = jnp.exp(blast[c * c_:c * c_ + 1, :])
        st_t = st_t * dec + _dot_tn(vb[rows], kt[rows])
    st_ref[...] = st_t
    o = o_acc + jnp.concatenate(inter, axis=0)

    return (_rms(o) * gn) * _silu(g_in.astype(F32))


HGRN_HEADS_PER_STEP = 4


def _hgrn(proj3, lb, gnorm):
    nb_, s, nc = proj3.shape
    tb = _pick(s, (512, 256, 128, 64))
    hp = HGRN_HEADS_PER_STEP
    w = hp * HGRN_D
    idx = jnp.arange(tb)
    tri = ((idx[:, None] // HGRN_CHUNK == idx[None, :] // HGRN_CHUNK) & (idx[None, :] <= idx[:, None])).astype(BF16)

    def spec(off):
        cb = off // w
        return pl.BlockSpec((None, tb, w), lambda b, h, n: (b, n, cb + h))

    vec = pl.BlockSpec((1, w), lambda b, h, n: (0, h))
    out = pl.pallas_call(
        functools.partial(_hgrn_kernel, tb=tb, hp=hp),
        out_shape=jax.ShapeDtypeStruct((nb_, s, HGRN_WIDTH), BF16),
        grid=(nb_, HGRN_HEADS // hp, s // tb),
        in_specs=[spec(RQ_OFF), spec(RF_OFF), spec(RI_OFF), spec(RG_OFF), vec, vec,
                  pl.BlockSpec((tb, tb), lambda b, h, n: (0, 0))],
        out_specs=pl.BlockSpec((None, tb, w), lambda b, h, n: (b, n, h)),
        scratch_shapes=[pltpu.VMEM((hp, HGRN_D, HGRN_D), F32),
                        pltpu.VMEM((hp, 3, tb // HGRN_DIRECT, 2 * HGRN_DIRECT, HGRN_D), F32)],
        compiler_params=_params(("parallel", "parallel", "arbitrary"), 32),
        name="hgrn2",
    )(proj3, proj3, proj3, proj3, lb.reshape(1, HGRN_WIDTH), gnorm.reshape(1, HGRN_WIDTH), tri)
    return out.reshape(nb_ * s, HGRN_WIDTH)


def _branch_kernel(a0_ref, a1_ref, a2_ref, r_ref, ga_ref, gb_ref, wa_ref, wr_ref, o_ref, att_scr, nat_scr,
                   *, dils):
    tm = att_scr.shape[0]
    nslab = ATT_OUT_COLS // 128
    for gi, (a_ref, dil) in enumerate(zip((a1_ref, a2_ref), dils)):
        for r in range(dil):
            for c in range(nslab):
                nat_scr[gi, c, pl.ds(r, tm // dil, stride=dil), :] = a_ref[r, :, c * 128:(c + 1) * 128]

    def head(g, h):
        return a0_ref[:, h * HEAD_DIM:(h + 1) * HEAD_DIM] if g == 0 else nat_scr[g - 1, h]

    def lse(g, h):
        if g == 0:
            return a0_ref[:, ATT_GW + h * LSE_LANES:ATT_GW + h * LSE_LANES + 1]
        return nat_scr[g - 1, ATT_HPG, :, h * LSE_LANES:h * LSE_LANES + 1]

    for h in range(ATT_HPG):
        ls = [lse(g, h) for g in range(3)]
        m = jnp.maximum(jnp.maximum(ls[0], ls[1]), ls[2])
        es = [jnp.exp(l - m) for l in ls]
        den = es[0] + es[1] + es[2]
        o = ((es[0] / den) * head(0, h) + (es[1] / den) * head(1, h)) + (es[2] / den) * head(2, h)
        att_scr[:, h * HEAD_DIM:(h + 1) * HEAD_DIM] = o.astype(BF16)

    a = _dot(att_scr[...], wa_ref[...])
    r = _dot(r_ref[...], wr_ref[...])
    o_ref[...] = (jax.nn.sigmoid(ga_ref[...].astype(F32)) * a
                  + jax.nn.sigmoid(gb_ref[...].astype(F32)) * r).astype(o_ref.dtype)


def _branch(att, orec, proj2, wa_bf, wr_bf, d, seq):
    t = orec.shape[0]
    tm = _pick(seq, (256,))
    per = seq // tm
    assert GATE_OFF % d == 0
    ga0 = GATE_OFF // d
    dils = tuple(dil for _, dil in ATT_GROUPS[1:])

    def aspec(dil):
        return pl.BlockSpec((None, dil, tm // dil, ATT_OUT_COLS), lambda i: (i // per, 0, i % per, 0))

    return pl.pallas_call(
        functools.partial(_branch_kernel, dils=dils),
        out_shape=jax.ShapeDtypeStruct((t, d), BF16),
        grid=(t // tm,),
        in_specs=[pl.BlockSpec((tm, ATT_OUT_COLS), lambda i: (i, 0)), aspec(dils[0]), aspec(dils[1]),
                  pl.BlockSpec((tm, HGRN_WIDTH), lambda i: (i, 0)),
                  pl.BlockSpec((tm, d), lambda i: (i, ga0)),
                  pl.BlockSpec((tm, d), lambda i: (i, ga0 + 1)),
                  pl.BlockSpec((ATT_GW, d), lambda i: (0, 0)),
                  pl.BlockSpec((HGRN_WIDTH, d), lambda i: (0, 0))],
        out_specs=pl.BlockSpec((tm, d), lambda i: (i, 0)),
        scratch_shapes=[pltpu.VMEM((tm, ATT_GW), BF16),
                        pltpu.VMEM((len(dils), ATT_OUT_COLS // 128, tm, 128), F32)],
        compiler_params=_params(("parallel",), 48),
        name="branch_merge",
    )(att[0].reshape(t, ATT_OUT_COLS), att[1], att[2], orec, proj2, proj2, wa_bf, wr_bf)


def _rows_to_tiles(a):
    m, n = a.shape
    return jnp.swapaxes(jnp.stack([a[:, c * 128:(c + 1) * 128] for c in range(n // 128)], axis=0), 0, 1)


def _tiles_to_rows(a):
    at = jnp.swapaxes(a, 0, 1)
    return jnp.concatenate([at[c] for c in range(a.shape[1])], axis=1)


def _outproj_kernel(m_ref, w_ref, x_ref, gm_ref, pnm_ref, pnf_ref, sh_ref, sc_ref, x1_ref, ht_ref, hb_ref):
    y = _dot(m_ref[...], w_ref[...])
    x1 = x_ref[...] + gm_ref[...] * (_rms(y) * pnm_ref[...])
    x1_ref[...] = x1
    h2 = ((_rms(x1) * pnf_ref[...]) * (1.0 + sc_ref[...]) + sh_ref[...]).astype(BF16)
    hb_ref[...] = h2
    ht_ref[...] = _rows_to_tiles(h2)


def _outproj(merged, w_bf, x2, gate_m, pnm, pnf, shift_f, scale_f, seq):
    t, d = x2.shape
    tm = _pick(seq, (256, 128))
    per = seq // tm
    row = pl.BlockSpec((tm, d), lambda i: (i, 0))
    vec = pl.BlockSpec((1, d), lambda i: (0, 0))
    bvec = pl.BlockSpec((None, 1, d), lambda i: (i // per, 0, 0))
    return pl.pallas_call(
        _outproj_kernel,
        out_shape=(jax.ShapeDtypeStruct((t, d), F32), jax.ShapeDtypeStruct((t, d // 128, 128), BF16),
                   jax.ShapeDtypeStruct((t, d), BF16)),
        grid=(t // tm,),
        in_specs=[row, pl.BlockSpec((d, d), lambda i: (0, 0)), row, bvec, vec, vec, bvec, bvec],
        out_specs=(row, pl.BlockSpec((tm, d // 128, 128), lambda i: (i, 0, 0)), row),
        compiler_params=_params(("parallel",), 48),
        name="out_proj",
    )(merged, w_bf, x2, gate_m, pnm.reshape(1, d), pnf.reshape(1, d), shift_f, scale_f)


def _router_kernel(h_ref, w_ref, bias_ref, up_ref, eid_ref, gate_ref, pos_ref, cnt_ref, carry_ref, *, tr):
    @pl.when(pl.program_id(0) == 0)
    def _():
        carry_ref[...] = jnp.zeros_like(carry_ref)

    per_group = N_EXPERTS // N_GROUPS
    sig = jax.nn.sigmoid(_dot_nt(w_ref[...], h_ref[...]))
    choice = sig + bias_ref[...]
    eidx = lax.broadcasted_iota(jnp.int32, (N_EXPERTS, tr), 0)

    c3 = choice.reshape(N_GROUPS, per_group, tr)
    sub = lax.broadcasted_iota(jnp.int32, (N_GROUPS, per_group, tr), 1)
    m1 = jnp.max(c3, axis=1, keepdims=True)
    first = jnp.min(jnp.where(c3 == m1, sub, per_group), axis=1, keepdims=True)
    m2 = jnp.max(jnp.where(sub == first, -jnp.inf, c3), axis=1, keepdims=True)
    gs = (m1 + m2).reshape(N_GROUPS, tr)

    gidx = lax.broadcasted_iota(jnp.int32, (N_GROUPS, tr), 0)
    grank = jnp.zeros((N_GROUPS, tr), jnp.int32)
    for g in range(N_GROUPS):
        row = gs[g:g + 1, :]
        grank = grank + ((row > gs) | ((row == gs) & (gidx > g))).astype(jnp.int32)
    gsel = jnp.where(grank < TOPK_GROUPS, 1.0, 0.0)
    emask = jnp.broadcast_to(gsel.reshape(N_GROUPS, 1, tr), (N_GROUPS, per_group, tr)).reshape(N_EXPERTS, tr)
    cm = jnp.where(emask > 0.5, choice, -jnp.inf)

    rank = jnp.zeros((N_EXPERTS, tr), jnp.int32)
    for e in range(N_EXPERTS):
        row = cm[e:e + 1, :]
        rank = rank + ((row > cm) | ((row == cm) & (eidx > e))).astype(jnp.int32)
    sel = rank < TOP_K

    denom = jnp.sum(jnp.where(sel, sig, 0.0), axis=0, keepdims=True)
    gate_full = sig / denom * ROUTED_SCALE

    sel_b = jnp.where(sel, 1.0, 0.0).astype(BF16)
    carry = carry_ref[...]
    cum = _dot(sel_b, up_ref[...]) + jnp.concatenate([carry] * (tr // 128), axis=1)
    carry_new = carry + _dot(sel_b, jnp.ones((tr, 128), BF16))
    carry_ref[...] = carry_new
    cnt_ref[...] = carry_new.astype(jnp.int32)
    posi = cum.astype(jnp.int32)

    eids, gates, poss = [], [], []
    for r in range(TOP_K):
        hit = rank == r
        eids.append(jnp.sum(jnp.where(hit, eidx, 0), axis=0, keepdims=True))
        gates.append(jnp.sum(jnp.where(hit, gate_full, 0.0), axis=0, keepdims=True))
        poss.append(jnp.sum(jnp.where(hit, posi, 0), axis=0, keepdims=True))
    eid_ref[...] = jnp.concatenate(eids, axis=0)
    gate_ref[...] = jnp.concatenate(gates, axis=0)
    pos_ref[...] = jnp.concatenate(poss, axis=0)


def _router(h2b, w_router, router_bias):
    t, d = h2b.shape
    tr = _pick(t, (256, 128))
    idx = jnp.arange(tr)
    upper = (idx[:, None] < idx[None, :]).astype(BF16)
    kout = pl.BlockSpec((TOP_K, tr), lambda i: (0, i))
    return pl.pallas_call(
        functools.partial(_router_kernel, tr=tr),
        out_shape=(jax.ShapeDtypeStruct((TOP_K, t), jnp.int32), jax.ShapeDtypeStruct((TOP_K, t), F32),
                   jax.ShapeDtypeStruct((TOP_K, t), jnp.int32), jax.ShapeDtypeStruct((N_EXPERTS, 128), jnp.int32)),
        grid=(t // tr,),
        in_specs=[pl.BlockSpec((tr, d), lambda i: (i, 0)),
                  pl.BlockSpec((N_EXPERTS, d), lambda i: (0, 0)),
                  pl.BlockSpec((N_EXPERTS, 1), lambda i: (0, 0)),
                  pl.BlockSpec((tr, tr), lambda i: (0, 0))],
        out_specs=(kout, kout, kout, pl.BlockSpec((N_EXPERTS, 128), lambda i: (0, 0))),
        scratch_shapes=[pltpu.VMEM((N_EXPERTS, 128), F32)],
        compiler_params=_params(("arbitrary",), 32),
        name="router",
    )(h2b, w_router.T.astype(BF16), router_bias.reshape(N_EXPERTS, 1).astype(F32), upper)


ZERO_RUN = 64


def _dispatch_kernel(zs_ref, zn_ref, d_ref, h_ref, xs_hbm, zero_scr, sem, zsem, *, tm, n_blocks):
    i = pl.program_id(0)

    for r in range(tm):
        for k in range(TOP_K):
            pltpu.make_async_copy(h_ref.at[pl.ds(r, 1)], xs_hbm.at[pl.ds(d_ref[0, k * tm + r], 1)],
                                  sem).start(priority=(r + k) % 2)

    @pl.when(i == 0)
    def _():
        zero_scr[...] = jnp.zeros_like(zero_scr)
        for e in range(N_EXPERTS):
            nrun = zn_ref[e] // ZERO_RUN

            def zrun(u, carry, e=e):
                pltpu.make_async_copy(zero_scr.at[pl.ds(0, ZERO_RUN)],
                                      xs_hbm.at[pl.ds(zs_ref[e] + u * ZERO_RUN, ZERO_RUN)], zsem).start()
                return carry
            lax.fori_loop(0, nrun, zrun, 0)

            def zrow(u, carry, e=e, nrun=nrun):
                pltpu.make_async_copy(zero_scr.at[pl.ds(0, 1)],
                                      xs_hbm.at[pl.ds(zs_ref[e] + nrun * ZERO_RUN + u, 1)], zsem).start()
                return carry
            lax.fori_loop(0, zn_ref[e] - nrun * ZERO_RUN, zrow, 0)
        for e in range(N_EXPERTS):
            nrun = zn_ref[e] // ZERO_RUN

            def zrun_wait(u, carry):
                pltpu.make_async_copy(zero_scr.at[pl.ds(0, ZERO_RUN)], xs_hbm.at[pl.ds(0, ZERO_RUN)], zsem).wait()
                return carry
            lax.fori_loop(0, nrun, zrun_wait, 0)

            def zrow_wait(u, carry):
                pltpu.make_async_copy(zero_scr.at[pl.ds(0, 1)], xs_hbm.at[pl.ds(0, 1)], zsem).wait()
                return carry
            lax.fori_loop(0, zn_ref[e] - nrun * ZERO_RUN, zrow_wait, 0)

        def tbody(blk, carry):
            pltpu.make_async_copy(zero_scr, xs_hbm.at[pl.ds(blk * MOE_BLOCK, MOE_BLOCK)], zsem).start()
            return carry
        lax.fori_loop(zn_ref[N_EXPERTS], n_blocks, tbody, 0)

        def twait(blk, carry):
            pltpu.make_async_copy(zero_scr, xs_hbm.at[pl.ds(0, MOE_BLOCK)], zsem).wait()
            return carry
        lax.fori_loop(zn_ref[N_EXPERTS], n_blocks, twait, 0)

    for k in range(TOP_K):
        pltpu.make_async_copy(h_ref, xs_hbm.at[pl.ds(0, tm)], sem).wait()


def _dispatch(h2t, dest3, zstart, znum, n_slots, tm):
    t, nch, _ = h2t.shape
    grid_spec = pltpu.PrefetchScalarGridSpec(
        num_scalar_prefetch=2,
        grid=(t // tm,),
        in_specs=[pl.BlockSpec((None, 1, TOP_K * tm), lambda i, zs, zn: (i, 0, 0), memory_space=pltpu.SMEM),
                  pl.BlockSpec((tm, nch, 128), lambda i, zs, zn: (i, 0, 0))],
        out_specs=pl.BlockSpec(memory_space=pl.ANY),
        scratch_shapes=[pltpu.VMEM((MOE_BLOCK, nch, 128), h2t.dtype), pltpu.SemaphoreType.DMA(()),
                        pltpu.SemaphoreType.DMA(())],
    )
    return pl.pallas_call(
        functools.partial(_dispatch_kernel, tm=tm, n_blocks=n_slots // MOE_BLOCK),
        out_shape=jax.ShapeDtypeStruct((n_slots, nch, 128), h2t.dtype),
        grid_spec=grid_spec,
        compiler_params=_params(("arbitrary",), 32),
        name="dispatch",
    )(zstart, znum, dest3, h2t)


def _expert_kernel(be_ref, nu_ref, first_ref, par_ref, nxt_ref, x_ref, wg_hbm, wu_hbm, wd_hbm, o_ref,
                   wg_f, wu_f, wd_f, wg_bf, wu_bf, wd_bf, sem):
    i = pl.program_id(0)

    def weight_copies(e, s):
        return [pltpu.make_async_copy(src.at[e], dst.at[s], sem.at[s])
                for src, dst in ((wg_hbm, wg_f), (wu_hbm, wu_f), (wd_hbm, wd_f))]

    @pl.when(i == 0)
    def _():
        for cp in weight_copies(be_ref[0], 0):
            cp.start()

    @pl.when(first_ref[i] == 1)
    def _():
        p = par_ref[i]
        for cp in weight_copies(be_ref[i], p):
            cp.wait()
        wg_bf[...] = wg_f[p].astype(BF16)
        wu_bf[...] = wu_f[p].astype(BF16)
        wd_bf[...] = wd_f[p].astype(BF16)

        @pl.when(nxt_ref[i] >= 0)
        def _():
            for cp in weight_copies(nxt_ref[i], 1 - p):
                cp.start()

    @pl.when(i < nu_ref[0])
    def _():
        xb = _tiles_to_rows(x_ref[...])
        hid = _silu(_dot(xb, wg_bf[...])) * _dot(xb, wu_bf[...])
        o_ref[...] = _rows_to_tiles(_dot(hid.astype(BF16), wd_bf[...]).astype(o_ref.dtype))

    @pl.when(i >= nu_ref[0])
    def _():
        o_ref[...] = jnp.zeros_like(o_ref)


def _experts(x_sorted, block_expert, n_used, first, par, nxt, wg, wu, wd):
    n_slots, nch, _ = x_sorted.shape
    d = nch * 128
    n_blocks = block_expert.shape[0]
    de = wg.shape[-1]
    hbm = pl.BlockSpec(memory_space=pl.ANY)
    grid_spec = pltpu.PrefetchScalarGridSpec(
        num_scalar_prefetch=5,
        grid=(n_blocks,),
        in_specs=[pl.BlockSpec((MOE_BLOCK, nch, 128), lambda i, be, nu, *_: (jnp.minimum(i, nu[0] - 1), 0, 0)),
                  hbm, hbm, hbm],
        out_specs=pl.BlockSpec((MOE_BLOCK, nch, 128), lambda i, *_: (i, 0, 0)),
        scratch_shapes=[pltpu.VMEM((2, d, de), F32), pltpu.VMEM((2, d, de), F32), pltpu.VMEM((2, de, d), F32),
                        pltpu.VMEM((d, de), BF16), pltpu.VMEM((d, de), BF16), pltpu.VMEM((de, d), BF16),
                        pltpu.SemaphoreType.DMA((2,))],
    )
    return pl.pallas_call(
        _expert_kernel,
        out_shape=jax.ShapeDtypeStruct((n_slots, nch, 128), BF16),
        grid_spec=grid_spec,
        compiler_params=_params(("arbitrary",), 56),
        name="experts",
    )(block_expert, n_used, first, par, nxt, x_sorted, wg, wu, wd)


def _combine_kernel(dc_ref, dn_ref, y_hbm, gt_ref, h_ref, x1_ref, wg_ref, wu_ref, wd_ref, gf_ref, pn_ref,
                    o_ref, buf, sem, *, tm):
    i = pl.program_id(0)
    nsteps = pl.num_programs(0)
    slot = i % 2

    def issue_tile(d_ref, s):
        for r in range(tm):
            for k in range(TOP_K):
                pltpu.make_async_copy(y_hbm.at[pl.ds(d_ref[0, k * tm + r], 1)], buf.at[s, k, pl.ds(r, 1)],
                                      sem.at[s]).start(priority=(r + k) % 2)

    @pl.when(i == 0)
    def _():
        issue_tile(dc_ref, 0)

    @pl.when(i + 1 < nsteps)
    def _():
        issue_tile(dn_ref, 1 - slot)

    hb = h_ref[...]
    shared = _dot((_silu(_dot(hb, wg_ref[...])) * _dot(hb, wu_ref[...])).astype(BF16), wd_ref[...])

    for k in range(TOP_K):
        pltpu.make_async_copy(y_hbm.at[pl.ds(0, tm)], buf.at[slot, k], sem.at[slot]).wait()

    gt = gt_ref[...]
    acc = shared
    for k in range(TOP_K):
        acc = acc + gt[:, k:k + 1] * _tiles_to_rows(buf[slot, k]).astype(F32)
    o_ref[...] = x1_ref[...] + gf_ref[...] * (_rms(acc) * pn_ref[...])


def _combine(y_sorted, dest3, gate_t, h2b, x1, wgs, wus, wds, gate_f, pnf, seq, tm):
    t, d = x1.shape
    ds_ = wgs.shape[-1]
    per = seq // tm
    nt = t // tm
    smem = functools.partial(pl.BlockSpec, memory_space=pltpu.SMEM)
    row = pl.BlockSpec((tm, d), lambda i: (i, 0))
    return pl.pallas_call(
        functools.partial(_combine_kernel, tm=tm),
        out_shape=jax.ShapeDtypeStruct((t, d), F32),
        grid=(nt,),
        in_specs=[smem((None, 1, TOP_K * tm), lambda i: (i, 0, 0)),
                  smem((None, 1, TOP_K * tm), lambda i: (jnp.minimum(i + 1, nt - 1), 0, 0)),
                  pl.BlockSpec(memory_space=pl.ANY),
                  pl.BlockSpec((tm, TOP_K), lambda i: (i, 0)),
                  row, row,
                  pl.BlockSpec((d, ds_), lambda i: (0, 0)),
                  pl.BlockSpec((d, ds_), lambda i: (0, 0)),
                  pl.BlockSpec((ds_, d), lambda i: (0, 0)),
                  pl.BlockSpec((None, 1, d), lambda i: (i // per, 0, 0)),
                  pl.BlockSpec((1, d), lambda i: (0, 0))],
        out_specs=row,
        scratch_shapes=[pltpu.VMEM((2, TOP_K, tm) + y_sorted.shape[1:], y_sorted.dtype),
                        pltpu.SemaphoreType.DMA((2,))],
        compiler_params=_params(("arbitrary",), 48),
        name="combine",
    )(dest3, dest3, y_sorted, gate_t, h2b, x1, wgs, wus, wds, gate_f, pnf.reshape(1, d))


def _layer(x, c, lb, w_ada, b_ada, pre_norm_mix, post_norm_mix, w_in, hgrn_norm, w_branch_attn, w_branch_hgrn,
           w_out, pre_norm_ffn, post_norm_ffn, w_router, router_bias, w_gate_e, w_up_e, w_down_e,
           w_gate_s, w_up_s, w_down_s):
    nb, s, d = x.shape
    t = nb * s
    x2 = x.reshape(t, d)

    mod = _ada(c, w_ada, b_ada).reshape(nb, 6, 1, d)
    shift_m, scale_m, gate_m, shift_f, scale_f, gate_f = (mod[:, k] for k in range(6))

    qkv = []
    for g, (_, dil) in enumerate(ATT_GROUPS):
        cols = jnp.concatenate([w_in[:, p * ATT_WIDTH + g * ATT_GW:p * ATT_WIDTH + (g + 1) * ATT_GW]
                                for p in range(3)], axis=1).astype(BF16)
        o = _inproj(x2, pre_norm_mix, shift_m, scale_m, cols, nb, s, dil, tn=3 * ATT_GW)
        qkv.append(o.reshape(nb, dil, s // dil, 3 * ATT_GW))
    proj2 = _inproj(x2, pre_norm_mix, shift_m, scale_m, w_in[:, QKV_COLS:].astype(BF16), nb, s)
    proj3 = proj2.reshape(nb, s, -1)

    att = [_attention_group(qkv[g], g) for g in range(len(ATT_GROUPS))]
    orec = _hgrn(proj3, lb, hgrn_norm)
    merged = _branch(att, orec, proj2, w_branch_attn.astype(BF16), w_branch_hgrn.astype(BF16), d, s)
    x1, h2t, h2b = _outproj(merged, w_out.astype(BF16), x2, gate_m, post_norm_mix, pre_norm_ffn,
                            shift_f, scale_f, s)

    eid, gate, pos, cnt = _router(h2b, w_router, router_bias)

    counts = cnt[:, 0]
    padded = (counts + MOE_BLOCK - 1) // MOE_BLOCK * MOE_BLOCK
    pad_end = jnp.cumsum(padded)
    pad_start = pad_end - padded
    n_blocks = -(-(t * TOP_K) // MOE_BLOCK) + N_EXPERTS
    onehot = eid[None] == jnp.arange(N_EXPERTS, dtype=jnp.int32)[:, None, None]
    dest = jnp.sum(jnp.where(onehot, pad_start[:, None, None], 0), axis=0) + pos
    block_start = jnp.arange(n_blocks, dtype=jnp.int32) * MOE_BLOCK
    block_expert = jnp.minimum(jnp.sum((block_start[:, None] >= pad_end[None, :]).astype(jnp.int32), axis=1),
                               N_EXPERTS - 1)
    n_used = (pad_end[-1:] // MOE_BLOCK).astype(jnp.int32)

    def tile_major(tile):
        return dest.reshape(TOP_K, t // tile, tile).transpose(1, 0, 2).reshape(t // tile, 1, TOP_K * tile)

    tm = _pick(s, (128,))
    dest3 = tile_major(tm)
    znum = jnp.concatenate([padded - counts, n_used]).astype(jnp.int32)
    x_sorted = _dispatch(h2t, dest3, (pad_start + counts).astype(jnp.int32), znum, n_blocks * MOE_BLOCK, tm)
    eidx = jnp.arange(N_EXPERTS, dtype=jnp.int32)
    has = counts > 0
    rank = jnp.cumsum(has.astype(jnp.int32)) - 1
    later = (eidx[None, :] > eidx[:, None]) & has[None, :]
    nxt_e = jnp.min(jnp.where(later, eidx[None, :], N_EXPERTS), axis=1)
    nxt_e = jnp.where(nxt_e == N_EXPERTS, -1, nxt_e).astype(jnp.int32)
    oh_b = block_expert[:, None] == eidx[None, :]
    pick = lambda v: jnp.sum(jnp.where(oh_b, v[None, :], 0), axis=1).astype(jnp.int32)
    first = ((block_start == pick(pad_start)) & (block_start < pad_end[-1])).astype(jnp.int32)
    y_sorted = _experts(x_sorted, block_expert, n_used, first, pick(rank % 2), pick(nxt_e),
                        w_gate_e, w_up_e, w_down_e)
    out = _combine(y_sorted, dest3, gate.T, h2b, x1, w_gate_s.astype(BF16), w_up_s.astype(BF16),
                   w_down_s.astype(BF16), gate_f, post_norm_ffn, s, tm)
    return out.reshape(nb, s, d)


def kernel(x, c, w_ada, b_ada, pre_norm_mix, post_norm_mix, w_in, hgrn_lb_logits, hgrn_norm, w_branch_attn,
           w_branch_hgrn, w_out, pre_norm_ffn, post_norm_ffn, w_router, router_bias, w_gate_e, w_up_e, w_down_e,
           w_gate_s, w_up_s, w_down_s):
    lb_table = jnp.cumsum(jax.nn.softmax(hgrn_lb_logits.astype(F32), axis=0), axis=0)
    depth = w_ada.shape[0]
    for l in range(depth):
        x = _layer(x, c, lb_table[l], w_ada[l], b_ada[l], pre_norm_mix[l], post_norm_mix[l], w_in[l],
                   hgrn_norm[l], w_branch_attn[l], w_branch_hgrn[l], w_out[l], pre_norm_ffn[l],
                   post_norm_ffn[l], w_router[l], router_bias[l], w_gate_e[l], w_up_e[l], w_down_e[l],
                   w_gate_s[l], w_up_s[l], w_down_s[l])
    return x
```

```python
import functools

import jax
import jax.numpy as jnp
from jax import lax
from jax.experimental import pallas as pl
from jax.experimental.pallas import tpu as pltpu

F32 = jnp.float32
BF16 = jnp.bfloat16

HEAD_DIM = 128
ATT_GROUPS = ((128, 1), (512, 4), (2048, 16))
ATT_HPG = 4
ATT_HEADS = ATT_HPG * len(ATT_GROUPS)
ATT_WIDTH = ATT_HEADS * HEAD_DIM
ATT_GW = ATT_HPG * HEAD_DIM
ATT_BLOCK = 128
ATT_OUT_COLS = ATT_GW + HEAD_DIM
LSE_LANES = HEAD_DIM // ATT_HPG
HGRN_HEADS = 8
HGRN_D = 128
HGRN_WIDTH = HGRN_HEADS * HGRN_D
HGRN_CHUNK = 64
HGRN_DIRECT = 8
N_EXPERTS = 64
N_GROUPS = 8
TOPK_GROUPS = 4
TOP_K = 8
ROUTED_SCALE = 2.5
MOE_BLOCK = 512
NORM_EPS = 1e-6
V7X_VMEM_LIMIT = 56 * 1024 * 1024

QKV_COLS = 3 * ATT_WIDTH
RQ_OFF = 0
RF_OFF = RQ_OFF + HGRN_WIDTH
RI_OFF = RF_OFF + HGRN_WIDTH
RG_OFF = RI_OFF + HGRN_WIDTH
GATE_OFF = RG_OFF + HGRN_WIDTH


def _pick(n, cands):
    for c in cands:
        if n % c == 0:
            return c
    raise ValueError(f"no tile of {cands} divides {n}")


def _params(sem, vmem_mib=None):
    kw = dict(dimension_semantics=sem)
    if vmem_mib is not None:
        kw["vmem_limit_bytes"] = min(vmem_mib * 1024 * 1024, V7X_VMEM_LIMIT)
    return pltpu.CompilerParams(**kw)


def _dot(a, b):
    return jnp.dot(a, b, preferred_element_type=F32)


def _dot_nt(a, b):
    return lax.dot_general(a, b, (((1,), (1,)), ((), ())), preferred_element_type=F32)


def _dot_tn(a, b):
    return lax.dot_general(a, b, (((0,), (0,)), ((), ())), preferred_element_type=F32)


def _rms(y):
    return y * lax.rsqrt(jnp.mean(y * y, axis=-1, keepdims=True) + NORM_EPS)


def _silu(a):
    return a * jax.nn.sigmoid(a)


ADA_KCHUNK = 128


def _ada_kernel(ct_ref, w_ref, b_ref, o_ref, *, nb, d):
    a = _silu(ct_ref[...])
    tn = w_ref.shape[1]
    accs = [jnp.zeros((8, tn), F32) for _ in range(nb)]
    for kc in range(d // ADA_KCHUNK):
        wc = w_ref[kc * ADA_KCHUNK:(kc + 1) * ADA_KCHUNK, :]
        ac = a[kc * ADA_KCHUNK:(kc + 1) * ADA_KCHUNK, :]
        for b in range(nb):
            p = ac[:, b:b + 1] * wc
            accs[b] = accs[b] + p.reshape(ADA_KCHUNK // 8, 8, tn).sum(axis=0)
    rows = [jnp.sum(acc, axis=0, keepdims=True) for acc in accs]
    o_ref[...] = jnp.concatenate(rows, axis=0) + b_ref[...]


def _ada(c, w_ada, b_ada):
    nb, d = c.shape
    n = w_ada.shape[1]
    tn = _pick(n, (512, 256, 128))
    return pl.pallas_call(
        functools.partial(_ada_kernel, nb=nb, d=d),
        out_shape=jax.ShapeDtypeStruct((nb, n), F32),
        grid=(n // tn,),
        in_specs=[pl.BlockSpec((d, nb), lambda j: (0, 0)),
                  pl.BlockSpec((d, tn), lambda j: (0, j)),
                  pl.BlockSpec((1, tn), lambda j: (0, j))],
        out_specs=pl.BlockSpec((nb, tn), lambda j: (0, j)),
        compiler_params=_params(("arbitrary",), 32),
        name="ada_mod",
    )(c.T, w_ada, b_ada.reshape(1, n))


DEINTERLEAVE_STRIDE = 4


def _inproj_kernel(x_ref, g_ref, sh_ref, sc_ref, w_ref, o_ref, h_scr, *slab, dil):
    @pl.when(pl.program_id(1) == 0)
    def _():
        y = _rms(x_ref[...]) * g_ref[...]
        h_scr[...] = (y * (1.0 + sc_ref[...]) + sh_ref[...]).astype(BF16)

    o = _dot(h_scr[...], w_ref[...])
    if dil == 1:
        o_ref[...] = o.astype(o_ref.dtype)
    else:
        slab_ref, slab2_ref = slab
        tm, tn = o.shape
        s1 = DEINTERLEAVE_STRIDE if dil > DEINTERLEAVE_STRIDE else dil
        s2 = dil // s1
        for c in range(tn // 128):
            slab_ref[c] = o[:, c * 128:(c + 1) * 128]
        for c in range(tn // 128):
            if s2 == 1:
                for r in range(dil):
                    o_ref[r, :, c * 128:(c + 1) * 128] = slab_ref[c, pl.ds(r, tm // dil, stride=dil), :].astype(o_ref.dtype)
            else:
                for r1 in range(s1):
                    slab2_ref[c, r1] = slab_ref[c, pl.ds(r1, tm // s1, stride=s1), :]
                for r1 in range(s1):
                    for r2 in range(s2):
                        o_ref[r1 + s1 * r2, :, c * 128:(c + 1) * 128] = (
                            slab2_ref[c, r1, pl.ds(r2, tm // dil, stride=s2), :].astype(o_ref.dtype))


def _inproj(x2, gain, shift, scale, w_bf, nb, seq, dil=1, tn=None):
    t, d = x2.shape
    n = w_bf.shape[1]
    tm = _pick(seq, (1024, 512, 256) if tn is None else (512, 256))
    tn = tn or _pick(n, (1024, 512, 256, 128))
    per = seq // tm
    if dil == 1:
        out_shape = jax.ShapeDtypeStruct((t, n), BF16)
        out_spec = pl.BlockSpec((tm, tn), lambda i, j: (i, j))
        scratch = [pltpu.VMEM((tm, d), BF16)]
    else:
        out_shape = jax.ShapeDtypeStruct((nb, dil, seq // dil, n), BF16)
        out_spec = pl.BlockSpec((None, dil, tm // dil, tn), lambda i, j: (i // per, 0, i % per, j))
        scratch = [pltpu.VMEM((tm, d), BF16), pltpu.VMEM((tn // 128, tm, 128), F32),
                   pltpu.VMEM((tn // 128, DEINTERLEAVE_STRIDE, tm // DEINTERLEAVE_STRIDE, 128), F32)]
    return pl.pallas_call(
        functools.partial(_inproj_kernel, dil=dil),
        out_shape=out_shape,
        grid=(t // tm, n // tn),
        in_specs=[pl.BlockSpec((tm, d), lambda i, j: (i, 0)),
                  pl.BlockSpec((1, d), lambda i, j: (0, 0)),
                  pl.BlockSpec((None, 1, d), lambda i, j: (i // per, 0, 0)),
                  pl.BlockSpec((None, 1, d), lambda i, j: (i // per, 0, 0)),
                  pl.BlockSpec((d, tn), lambda i, j: (0, j))],
        out_specs=out_spec,
        scratch_shapes=scratch,
        compiler_params=_params(("parallel", "arbitrary"), 48),
        name=f"in_proj_d{dil}_n{n}",
    )(x2, gain.reshape(1, d), shift, scale, w_bf)


def _attn_kernel(q_ref, kp_ref, kc_ref, vp_ref, vc_ref, o_ref, *, dil, slopes, scale, qb):
    n = pl.program_id(2)
    qi = lax.broadcasted_iota(jnp.int32, (ATT_BLOCK, ATT_BLOCK), 0)
    ki = lax.broadcasted_iota(jnp.int32, (ATT_BLOCK, ATT_BLOCK), 1)
    jc = qi - ki
    jp = jc + ATT_BLOCK
    first_lim = jnp.where(n > 0, ATT_BLOCK, -1)
    valid_c = jc >= 0
    lane = lax.broadcasted_iota(jnp.int32, (ATT_BLOCK, HEAD_DIM), 1)
    ones = jnp.ones((ATT_BLOCK, HEAD_DIM), BF16)
    for j in range(qb):
        rows = slice(j * ATT_BLOCK, (j + 1) * ATT_BLOCK)
        prow = slice((j - 1) * ATT_BLOCK, j * ATT_BLOCK)
        valid_p = jp <= (first_lim if j == 0 else ATT_BLOCK)
        lse_blk = jnp.zeros((ATT_BLOCK, HEAD_DIM), F32)
        for h in range(ATT_HPG):
            hs = slice(h * HEAD_DIM, (h + 1) * HEAD_DIM)
            q = q_ref[rows, hs]
            k_p = kp_ref[:, hs] if j == 0 else kc_ref[prow, hs]
            v_p = vp_ref[:, hs] if j == 0 else vc_ref[prow, hs]
            bias = slopes[h] * dil
            s_c = _dot_nt(q, kc_ref[rows, hs]) * scale - bias * jc.astype(F32)
            s_p = _dot_nt(q, k_p) * scale - bias * jp.astype(F32)
            s_c = jnp.where(valid_c, s_c, -jnp.inf)
            s_p = jnp.where(valid_p, s_p, -jnp.inf)
            m = jnp.maximum(jnp.max(s_c, axis=-1, keepdims=True), jnp.max(s_p, axis=-1, keepdims=True))
            p_c = jnp.exp(s_c - m).astype(BF16)
            p_p = jnp.exp(s_p - m).astype(BF16)
            oa = (_dot(p_c, jnp.concatenate([vc_ref[rows, hs], ones], axis=1))
                  + _dot(p_p, jnp.concatenate([v_p, ones], axis=1)))
            l = oa[:, HEAD_DIM:HEAD_DIM + 1]
            o_ref[rows, hs] = oa[:, :HEAD_DIM] / l
            lse = m + jnp.log(l)
            lse_blk = jnp.where((lane >= h * LSE_LANES) & (lane < (h + 1) * LSE_LANES), lse, lse_blk)
        o_ref[rows, ATT_GW:] = lse_blk


def _attention_group(qkv, g):
    nb_, dil, l, _ = qkv.shape
    window, dil_ = ATT_GROUPS[g]
    assert dil == dil_ and window // dil == ATT_BLOCK and l % ATT_BLOCK == 0
    qb = _pick(l // ATT_BLOCK, (4, 2, 1))
    slopes = tuple(2.0 ** (-8.0 * (g * ATT_HPG + i + 1) / ATT_HEADS) for i in range(ATT_HPG))

    def spec(part, prev):
        if prev:
            return pl.BlockSpec((None, None, ATT_BLOCK, ATT_GW),
                                lambda b, r, n: (b, r, jnp.maximum(n * qb - 1, 0), part))
        return pl.BlockSpec((None, None, qb * ATT_BLOCK, ATT_GW), lambda b, r, n: (b, r, n, part))

    return pl.pallas_call(
        functools.partial(_attn_kernel, dil=float(dil), slopes=slopes, scale=HEAD_DIM ** -0.5, qb=qb),
        out_shape=jax.ShapeDtypeStruct((nb_, dil, l, ATT_OUT_COLS), F32),
        grid=(nb_, dil, l // (qb * ATT_BLOCK)),
        in_specs=[spec(0, False), spec(1, True), spec(1, False), spec(2, True), spec(2, False)],
        out_specs=pl.BlockSpec((None, None, qb * ATT_BLOCK, ATT_OUT_COLS), lambda b, r, n: (b, r, n, 0)),
        compiler_params=_params(("parallel", "parallel", "arbitrary")),
        name=f"attn_g{g}",
    )(qkv, qkv, qkv, qkv, qkv)


def _hgrn_kernel(q_ref, f_ref, i_ref, g_ref, lb_ref, gn_ref, tri_ref, o_ref, st_ref, sh_ref, *, tb, hp):
    @pl.when(pl.program_id(2) == 0)
    def _():
        st_ref[...] = jnp.zeros_like(st_ref)
        sh_ref[:, :, :, 0:HGRN_DIRECT, :] = jnp.zeros((hp, 3, tb // HGRN_DIRECT, HGRN_DIRECT, HGRN_D), F32)

    for hh in range(hp):
        cols = slice(hh * HGRN_D, (hh + 1) * HGRN_D)
        o_ref[:, cols] = _hgrn_head(q_ref[:, cols], f_ref[:, cols], i_ref[:, cols], g_ref[:, cols],
                                    lb_ref[:, cols], gn_ref[:, cols], tri_ref[...], st_ref.at[hh], sh_ref.at[hh],
                                    tb).astype(o_ref.dtype)


def _hgrn_head(q_in, f_in, i_in, g_in, lb, gn, tri, st_ref, sh_ref, tb):
    c_ = HGRN_CHUNK
    nc = tb // c_
    q = q_in.astype(F32)
    v = i_in.astype(F32)
    f = lb + (1.0 - lb) * jax.nn.sigmoid(f_in.astype(F32))
    kk = 1.0 - f
    lf = jnp.log(f)

    hi = lf.astype(BF16)
    r1 = lf - hi.astype(F32)
    mid = r1.astype(BF16)
    lo = (r1 - mid.astype(F32)).astype(BF16)
    b = _dot(tri, hi) + _dot(tri, mid) + _dot(tri, lo)

    def ref_rows(blk, row):
        b3 = b.reshape(tb // blk, blk, HGRN_D)
        return jnp.broadcast_to(b3[:, row:row + 1, :], (tb // blk, blk, HGRN_D)).reshape(tb, HGRN_D)

    ti = lax.broadcasted_iota(jnp.int32, (c_, c_), 0)
    si = lax.broadcasted_iota(jnp.int32, (c_, c_), 1)
    ssum = jnp.zeros((nc, c_, c_), F32)
    half = c_ // 2
    while half >= HGRN_DIRECT:
        bref = ref_rows(2 * half, half)
        ql = (q * jnp.exp(b - bref)).astype(BF16).reshape(nc, c_, HGRN_D)
        kl = (kk * jnp.exp(bref - b)).astype(BF16).reshape(nc, c_, HGRN_D)
        s_l = jnp.einsum("ctk,csk->cts", ql, kl, preferred_element_type=F32)
        mask = ((ti // (2 * half)) == (si // (2 * half))) & (((ti // half) % 2) == 1) & (((si // half) % 2) == 0)
        ssum = ssum + jnp.where(mask[None], s_l, 0.0)
        half //= 2
    v3 = v.astype(BF16).reshape(nc, c_, HGRN_D)
    o_acc = jnp.einsum("cts,csv->ctv", ssum.astype(BF16), v3, preferred_element_type=F32).reshape(tb, HGRN_D)

    nt8 = tb // HGRN_DIRECT
    for plane, val in enumerate((kk, f, v)):
        sh_ref[plane, :, HGRN_DIRECT:, :] = val.reshape(nt8, HGRN_DIRECT, HGRN_D)

    def shifted(plane, lag):
        st = HGRN_DIRECT - lag
        return sh_ref[plane, :, st:st + HGRN_DIRECT, :].reshape(tb, HGRN_D)

    ones = jnp.ones((HGRN_D, HGRN_D), BF16)
    decay = None
    for lag in range(HGRN_DIRECT):
        if lag == 0:
            w = q * kk
        else:
            decay = f if lag == 1 else decay * shifted(1, lag - 1)
            w = q * shifted(0, lag) * decay
        o_acc = o_acc + _dot(w.astype(BF16), ones) * (v if lag == 0 else shifted(2, lag))

    blast = ref_rows(c_, c_ - 1)
    qe = (q * jnp.exp(b)).astype(BF16)
    kt = (kk * jnp.exp(blast - b)).astype(BF16)
    vb = v.astype(BF16)
    st_t = st_ref[...]
    inter = []
    for c in range(nc):
        rows = slice(c * c_, (c + 1) * c_)
        inter.append(_dot_nt(qe[rows], st_t.astype(BF16)))
        dec = jnp.exp(blast[c * c_:c * c_ + 1, :])
        st_t = st_t * dec + _dot_tn(vb[rows], kt[rows])
    st_ref[...] = st_t
    o = o_acc + jnp.concatenate(inter, axis=0)

    return (_rms(o) * gn) * _silu(g_in.astype(F32))


HGRN_HEADS_PER_STEP = 4


def _hgrn(proj3, lb, gnorm):
    nb_, s, nc = proj3.shape
    tb = _pick(s, (512, 256, 128, 64))
    hp = HGRN_HEADS_PER_STEP
    w = hp * HGRN_D
    idx = jnp.arange(tb)
    tri = ((idx[:, None] // HGRN_CHUNK == idx[None, :] // HGRN_CHUNK) & (idx[None, :] <= idx[:, None])).astype(BF16)

    def spec(off):
        cb = off // w
        return pl.BlockSpec((None, tb, w), lambda b, h, n: (b, n, cb + h))

    vec = pl.BlockSpec((1, w), lambda b, h, n: (0, h))
    out = pl.pallas_call(
        functools.partial(_hgrn_kernel, tb=tb, hp=hp),
        out_shape=jax.ShapeDtypeStruct((nb_, s, HGRN_WIDTH), BF16),
        grid=(nb_, HGRN_HEADS // hp, s // tb),
        in_specs=[spec(RQ_OFF), spec(RF_OFF), spec(RI_OFF), spec(RG_OFF), vec, vec,
                  pl.BlockSpec((tb, tb), lambda b, h, n: (0, 0))],
        out_specs=pl.BlockSpec((None, tb, w), lambda b, h, n: (b, n, h)),
        scratch_shapes=[pltpu.VMEM((hp, HGRN_D, HGRN_D), F32),
                        pltpu.VMEM((hp, 3, tb // HGRN_DIRECT, 2 * HGRN_DIRECT, HGRN_D), F32)],
        compiler_params=_params(("parallel", "parallel", "arbitrary"), 32),
        name="hgrn2",
    )(proj3, proj3, proj3, proj3, lb.reshape(1, HGRN_WIDTH), gnorm.reshape(1, HGRN_WIDTH), tri)
    return out.reshape(nb_ * s, HGRN_WIDTH)


def _branch_kernel(a0_ref, a1_ref, a2_ref, r_ref, ga_ref, gb_ref, wa_ref, wr_ref, o_ref, att_scr, nat_scr,
                   *, dils):
    tm = att_scr.shape[0]
    nslab = ATT_OUT_COLS // 128
    for gi, (a_ref, dil) in enumerate(zip((a1_ref, a2_ref), dils)):
        for r in range(dil):
            for c in range(nslab):
                nat_scr[gi, c, pl.ds(r, tm // dil, stride=dil), :] = a_ref[r, :, c * 128:(c + 1) * 128]

    def head(g, h):
        return a0_ref[:, h * HEAD_DIM:(h + 1) * HEAD_DIM] if g == 0 else nat_scr[g - 1, h]

    def lse(g, h):
        if g == 0:
            return a0_ref[:, ATT_GW + h * LSE_LANES:ATT_GW + h * LSE_LANES + 1]
        return nat_scr[g - 1, ATT_HPG, :, h * LSE_LANES:h * LSE_LANES + 1]

    for h in range(ATT_HPG):
        ls = [lse(g, h) for g in range(3)]
        m = jnp.maximum(jnp.maximum(ls[0], ls[1]), ls[2])
        es = [jnp.exp(l - m) for l in ls]
        den = es[0] + es[1] + es[2]
        o = ((es[0] / den) * head(0, h) + (es[1] / den) * head(1, h)) + (es[2] / den) * head(2, h)
        att_scr[:, h * HEAD_DIM:(h + 1) * HEAD_DIM] = o.astype(BF16)

    a = _dot(att_scr[...], wa_ref[...])
    r = _dot(r_ref[...], wr_ref[...])
    o_ref[...] = (jax.nn.sigmoid(ga_ref[...].astype(F32)) * a
                  + jax.nn.sigmoid(gb_ref[...].astype(F32)) * r).astype(o_ref.dtype)


def _branch(att, orec, proj2, wa_bf, wr_bf, d, seq):
    t = orec.shape[0]
    tm = _pick(seq, (256,))
    per = seq // tm
    assert GATE_OFF % d == 0
    ga0 = GATE_OFF // d
    dils = tuple(dil for _, dil in ATT_GROUPS[1:])

    def aspec(dil):
        return pl.BlockSpec((None, dil, tm // dil, ATT_OUT_COLS), lambda i: (i // per, 0, i % per, 0))

    return pl.pallas_call(
        functools.partial(_branch_kernel, dils=dils),
        out_shape=jax.ShapeDtypeStruct((t, d), BF16),
        grid=(t // tm,),
        in_specs=[pl.BlockSpec((tm, ATT_OUT_COLS), lambda i: (i, 0)), aspec(dils[0]), aspec(dils[1]),
                  pl.BlockSpec((tm, HGRN_WIDTH), lambda i: (i, 0)),
                  pl.BlockSpec((tm, d), lambda i: (i, ga0)),
                  pl.BlockSpec((tm, d), lambda i: (i, ga0 + 1)),
                  pl.BlockSpec((ATT_GW, d), lambda i: (0, 0)),
                  pl.BlockSpec((HGRN_WIDTH, d), lambda i: (0, 0))],
        out_specs=pl.BlockSpec((tm, d), lambda i: (i, 0)),
        scratch_shapes=[pltpu.VMEM((tm, ATT_GW), BF16),
                        pltpu.VMEM((len(dils), ATT_OUT_COLS // 128, tm, 128), F32)],
        compiler_params=_params(("parallel",), 48),
        name="branch_merge",
    )(att[0].reshape(t, ATT_OUT_COLS), att[1], att[2], orec, proj2, proj2, wa_bf, wr_bf)


def _rows_to_tiles(a):
    m, n = a.shape
    return jnp.swapaxes(jnp.stack([a[:, c * 128:(c + 1) * 128] for c in range(n // 128)], axis=0), 0, 1)


def _tiles_to_rows(a):
    at = jnp.swapaxes(a, 0, 1)
    return jnp.concatenate([at[c] for c in range(a.shape[1])], axis=1)


def _outproj_kernel(m_ref, w_ref, x_ref, gm_ref, pnm_ref, pnf_ref, sh_ref, sc_ref, x1_ref, ht_ref, hb_ref):
    y = _dot(m_ref[...], w_ref[...])
    x1 = x_ref[...] + gm_ref[...] * (_rms(y) * pnm_ref[...])
    x1_ref[...] = x1
    h2 = ((_rms(x1) * pnf_ref[...]) * (1.0 + sc_ref[...]) + sh_ref[...]).astype(BF16)
    hb_ref[...] = h2
    ht_ref[...] = _rows_to_tiles(h2)


def _outproj(merged, w_bf, x2, gate_m, pnm, pnf, shift_f, scale_f, seq):
    t, d = x2.shape
    tm = _pick(seq, (256, 128))
    per = seq // tm
    row = pl.BlockSpec((tm, d), lambda i: (i, 0))
    vec = pl.BlockSpec((1, d), lambda i: (0, 0))
    bvec = pl.BlockSpec((None, 1, d), lambda i: (i // per, 0, 0))
    return pl.pallas_call(
        _outproj_kernel,
        out_shape=(jax.ShapeDtypeStruct((t, d), F32), jax.ShapeDtypeStruct((t, d // 128, 128), BF16),
                   jax.ShapeDtypeStruct((t, d), BF16)),
        grid=(t // tm,),
        in_specs=[row, pl.BlockSpec((d, d), lambda i: (0, 0)), row, bvec, vec, vec, bvec, bvec],
        out_specs=(row, pl.BlockSpec((tm, d // 128, 128), lambda i: (i, 0, 0)), row),
        compiler_params=_params(("parallel",), 48),
        name="out_proj",
    )(merged, w_bf, x2, gate_m, pnm.reshape(1, d), pnf.reshape(1, d), shift_f, scale_f)


def _router_kernel(h_ref, w_ref, bias_ref, up_ref, eid_ref, gate_ref, pos_ref, cnt_ref, carry_ref, *, tr):
    @pl.when(pl.program_id(0) == 0)
    def _():
        carry_ref[...] = jnp.zeros_like(carry_ref)

    per_group = N_EXPERTS // N_GROUPS
    sig = jax.nn.sigmoid(_dot_nt(w_ref[...], h_ref[...]))
    choice = sig + bias_ref[...]
    eidx = lax.broadcasted_iota(jnp.int32, (N_EXPERTS, tr), 0)

    c3 = choice.reshape(N_GROUPS, per_group, tr)
    sub = lax.broadcasted_iota(jnp.int32, (N_GROUPS, per_group, tr), 1)
    m1 = jnp.max(c3, axis=1, keepdims=True)
    first = jnp.min(jnp.where(c3 == m1, sub, per_group), axis=1, keepdims=True)
    m2 = jnp.max(jnp.where(sub == first, -jnp.inf, c3), axis=1, keepdims=True)
    gs = (m1 + m2).reshape(N_GROUPS, tr)

    gidx = lax.broadcasted_iota(jnp.int32, (N_GROUPS, tr), 0)
    grank = jnp.zeros((N_GROUPS, tr), jnp.int32)
    for g in range(N_GROUPS):
        row = gs[g:g + 1, :]
        grank = grank + ((row > gs) | ((row == gs) & (gidx > g))).astype(jnp.int32)
    gsel = jnp.where(grank < TOPK_GROUPS, 1.0, 0.0)
    emask = jnp.broadcast_to(gsel.reshape(N_GROUPS, 1, tr), (N_GROUPS, per_group, tr)).reshape(N_EXPERTS, tr)
    cm = jnp.where(emask > 0.5, choice, -jnp.inf)

    rank = jnp.zeros((N_EXPERTS, tr), jnp.int32)
    for e in range(N_EXPERTS):
        row = cm[e:e + 1, :]
        rank = rank + ((row > cm) | ((row == cm) & (eidx > e))).astype(jnp.int32)
    sel = rank < TOP_K

    denom = jnp.sum(jnp.where(sel, sig, 0.0), axis=0, keepdims=True)
    gate_full = sig / denom * ROUTED_SCALE

    sel_b = jnp.where(sel, 1.0, 0.0).astype(BF16)
    carry = carry_ref[...]
    cum = _dot(sel_b, up_ref[...]) + jnp.concatenate([carry] * (tr // 128), axis=1)
    carry_new = carry + _dot(sel_b, jnp.ones((tr, 128), BF16))
    carry_ref[...] = carry_new
    cnt_ref[...] = carry_new.astype(jnp.int32)
    posi = cum.astype(jnp.int32)

    eids, gates, poss = [], [], []
    for r in range(TOP_K):
        hit = rank == r
        eids.append(jnp.sum(jnp.where(hit, eidx, 0), axis=0, keepdims=True))
        gates.append(jnp.sum(jnp.where(hit, gate_full, 0.0), axis=0, keepdims=True))
        poss.append(jnp.sum(jnp.where(hit, posi, 0), axis=0, keepdims=True))
    eid_ref[...] = jnp.concatenate(eids, axis=0)
    gate_ref[...] = jnp.concatenate(gates, axis=0)
    pos_ref[...] = jnp.concatenate(poss, axis=0)


def _router(h2b, w_router, router_bias):
    t, d = h2b.shape
    tr = _pick(t, (256, 128))
    idx = jnp.arange(tr)
    upper = (idx[:, None] < idx[None, :]).astype(BF16)
    kout = pl.BlockSpec((TOP_K, tr), lambda i: (0, i))
    return pl.pallas_call(
        functools.partial(_router_kernel, tr=tr),
        out_shape=(jax.ShapeDtypeStruct((TOP_K, t), jnp.int32), jax.ShapeDtypeStruct((TOP_K, t), F32),
                   jax.ShapeDtypeStruct((TOP_K, t), jnp.int32), jax.ShapeDtypeStruct((N_EXPERTS, 128), jnp.int32)),
        grid=(t // tr,),
        in_specs=[pl.BlockSpec((tr, d), lambda i: (i, 0)),
                  pl.BlockSpec((N_EXPERTS, d), lambda i: (0, 0)),
                  pl.BlockSpec((N_EXPERTS, 1), lambda i: (0, 0)),
                  pl.BlockSpec((tr, tr), lambda i: (0, 0))],
        out_specs=(kout, kout, kout, pl.BlockSpec((N_EXPERTS, 128), lambda i: (0, 0))),
        scratch_shapes=[pltpu.VMEM((N_EXPERTS, 128), F32)],
        compiler_params=_params(("arbitrary",), 32),
        name="router",
    )(h2b, w_router.T.astype(BF16), router_bias.reshape(N_EXPERTS, 1).astype(F32), upper)


ZERO_RUN = 64


def _dispatch_kernel(zs_ref, zn_ref, d_ref, h_ref, xs_hbm, zero_scr, sem, zsem, *, tm, n_blocks):
    i = pl.program_id(0)

    for r in range(tm):
        for k in range(TOP_K):
            pltpu.make_async_copy(h_ref.at[pl.ds(r, 1)], xs_hbm.at[pl.ds(d_ref[0, k * tm + r], 1)],
                                  sem).start(priority=(r + k) % 2)

    @pl.when(i == 0)
    def _():
        zero_scr[...] = jnp.zeros_like(zero_scr)
        for e in range(N_EXPERTS):
            nrun = zn_ref[e] // ZERO_RUN

            def zrun(u, carry, e=e):
                pltpu.make_async_copy(zero_scr.at[pl.ds(0, ZERO_RUN)],
                                      xs_hbm.at[pl.ds(zs_ref[e] + u * ZERO_RUN, ZERO_RUN)], zsem).start()
                return carry
            lax.fori_loop(0, nrun, zrun, 0)

            def zrow(u, carry, e=e, nrun=nrun):
                pltpu.make_async_copy(zero_scr.at[pl.ds(0, 1)],
                                      xs_hbm.at[pl.ds(zs_ref[e] + nrun * ZERO_RUN + u, 1)], zsem).start()
                return carry
            lax.fori_loop(0, zn_ref[e] - nrun * ZERO_RUN, zrow, 0)
        for e in range(N_EXPERTS):
            nrun = zn_ref[e] // ZERO_RUN

            def zrun_wait(u, carry):
                pltpu.make_async_copy(zero_scr.at[pl.ds(0, ZERO_RUN)], xs_hbm.at[pl.ds(0, ZERO_RUN)], zsem).wait()
                return carry
            lax.fori_loop(0, nrun, zrun_wait, 0)

            def zrow_wait(u, carry):
                pltpu.make_async_copy(zero_scr.at[pl.ds(0, 1)], xs_hbm.at[pl.ds(0, 1)], zsem).wait()
                return carry
            lax.fori_loop(0, zn_ref[e] - nrun * ZERO_RUN, zrow_wait, 0)

        def tbody(blk, carry):
            pltpu.make_async_copy(zero_scr, xs_hbm.at[pl.ds(blk * MOE_BLOCK, MOE_BLOCK)], zsem).start()
            return carry
        lax.fori_loop(zn_ref[N_EXPERTS], n_blocks, tbody, 0)

        def twait(blk, carry):
            pltpu.make_async_copy(zero_scr, xs_hbm.at[pl.ds(0, MOE_BLOCK)], zsem).wait()
            return carry
        lax.fori_loop(zn_ref[N_EXPERTS], n_blocks, twait, 0)

    for k in range(TOP_K):
        pltpu.make_async_copy(h_ref, xs_hbm.at[pl.ds(0, tm)], sem).wait()


def _dispatch(h2t, dest3, zstart, znum, n_slots, tm):
    t, nch, _ = h2t.shape
    grid_spec = pltpu.PrefetchScalarGridSpec(
        num_scalar_prefetch=2,
        grid=(t // tm,),
        in_specs=[pl.BlockSpec((None, 1, TOP_K * tm), lambda i, zs, zn: (i, 0, 0), memory_space=pltpu.SMEM),
                  pl.BlockSpec((tm, nch, 128), lambda i, zs, zn: (i, 0, 0))],
        out_specs=pl.BlockSpec(memory_space=pl.ANY),
        scratch_shapes=[pltpu.VMEM((MOE_BLOCK, nch, 128), h2t.dtype), pltpu.SemaphoreType.DMA(()),
                        pltpu.SemaphoreType.DMA(())],
    )
    return pl.pallas_call(
        functools.partial(_dispatch_kernel, tm=tm, n_blocks=n_slots // MOE_BLOCK),
        out_shape=jax.ShapeDtypeStruct((n_slots, nch, 128), h2t.dtype),
        grid_spec=grid_spec,
        compiler_params=_params(("arbitrary",), 32),
        name="dispatch",
    )(zstart, znum, dest3, h2t)


def _expert_kernel(be_ref, nu_ref, first_ref, par_ref, nxt_ref, x_ref, wg_hbm, wu_hbm, wd_hbm, o_ref,
                   wg_f, wu_f, wd_f, wg_bf, wu_bf, wd_bf, sem):
    i = pl.program_id(0)

    def weight_copies(e, s):
        return [pltpu.make_async_copy(src.at[e], dst.at[s], sem.at[s])
                for src, dst in ((wg_hbm, wg_f), (wu_hbm, wu_f), (wd_hbm, wd_f))]

    @pl.when(i == 0)
    def _():
        for cp in weight_copies(be_ref[0], 0):
            cp.start()

    @pl.when(first_ref[i] == 1)
    def _():
        p = par_ref[i]
        for cp in weight_copies(be_ref[i], p):
            cp.wait()
        wg_bf[...] = wg_f[p].astype(BF16)
        wu_bf[...] = wu_f[p].astype(BF16)
        wd_bf[...] = wd_f[p].astype(BF16)

        @pl.when(nxt_ref[i] >= 0)
        def _():
            for cp in weight_copies(nxt_ref[i], 1 - p):
                cp.start()

    @pl.when(i < nu_ref[0])
    def _():
        xb = _tiles_to_rows(x_ref[...])
        hid = _silu(_dot(xb, wg_bf[...])) * _dot(xb, wu_bf[...])
        o_ref[...] = _rows_to_tiles(_dot(hid.astype(BF16), wd_bf[...]).astype(o_ref.dtype))

    @pl.when(i >= nu_ref[0])
    def _():
        o_ref[...] = jnp.zeros_like(o_ref)


def _experts(x_sorted, block_expert, n_used, first, par, nxt, wg, wu, wd):
    n_slots, nch, _ = x_sorted.shape
    d = nch * 128
    n_blocks = block_expert.shape[0]
    de = wg.shape[-1]
    hbm = pl.BlockSpec(memory_space=pl.ANY)
    grid_spec = pltpu.PrefetchScalarGridSpec(
        num_scalar_prefetch=5,
        grid=(n_blocks,),
        in_specs=[pl.BlockSpec((MOE_BLOCK, nch, 128), lambda i, be, nu, *_: (jnp.minimum(i, nu[0] - 1), 0, 0)),
                  hbm, hbm, hbm],
        out_specs=pl.BlockSpec((MOE_BLOCK, nch, 128), lambda i, *_: (i, 0, 0)),
        scratch_shapes=[pltpu.VMEM((2, d, de), F32), pltpu.VMEM((2, d, de), F32), pltpu.VMEM((2, de, d), F32),
                        pltpu.VMEM((d, de), BF16), pltpu.VMEM((d, de), BF16), pltpu.VMEM((de, d), BF16),
                        pltpu.SemaphoreType.DMA((2,))],
    )
    return pl.pallas_call(
        _expert_kernel,
        out_shape=jax.ShapeDtypeStruct((n_slots, nch, 128), BF16),
        grid_spec=grid_spec,
        compiler_params=_params(("arbitrary",), 56),
        name="experts",
    )(block_expert, n_used, first, par, nxt, x_sorted, wg, wu, wd)


def _combine_kernel(dc_ref, dn_ref, y_hbm, gt_ref, h_ref, x1_ref, wg_ref, wu_ref, wd_ref, gf_ref, pn_ref,
                    o_ref, buf, sem, *, tm):
    i = pl.program_id(0)
    nsteps = pl.num_programs(0)
    slot = i % 2

    def issue_tile(d_ref, s):
        for r in range(tm):
            for k in range(TOP_K):
                pltpu.make_async_copy(y_hbm.at[pl.ds(d_ref[0, k * tm + r], 1)], buf.at[s, k, pl.ds(r, 1)],
                                      sem.at[s]).start(priority=(r + k) % 2)

    @pl.when(i == 0)
    def _():
        issue_tile(dc_ref, 0)

    @pl.when(i + 1 < nsteps)
    def _():
        issue_tile(dn_ref, 1 - slot)

    hb = h_ref[...]
    shared = _dot((_silu(_dot(hb, wg_ref[...])) * _dot(hb, wu_ref[...])).astype(BF16), wd_ref[...])

    for k in range(TOP_K):
        pltpu.make_async_copy(y_hbm.at[pl.ds(0, tm)], buf.at[slot, k], sem.at[slot]).wait()

    gt = gt_ref[...]
    acc = shared
    for k in range(TOP_K):
        acc = acc + gt[:, k:k + 1] * _tiles_to_rows(buf[slot, k]).astype(F32)
    o_ref[...] = x1_ref[...] + gf_ref[...] * (_rms(acc) * pn_ref[...])


def _combine(y_sorted, dest3, gate_t, h2b, x1, wgs, wus, wds, gate_f, pnf, seq, tm):
    t, d = x1.shape
    ds_ = wgs.shape[-1]
    per = seq // tm
    nt = t // tm
    smem = functools.partial(pl.BlockSpec, memory_space=pltpu.SMEM)
    row = pl.BlockSpec((tm, d), lambda i: (i, 0))
    return pl.pallas_call(
        functools.partial(_combine_kernel, tm=tm),
        out_shape=jax.ShapeDtypeStruct((t, d), F32),
        grid=(nt,),
        in_specs=[smem((None, 1, TOP_K * tm), lambda i: (i, 0, 0)),
                  smem((None, 1, TOP_K * tm), lambda i: (jnp.minimum(i + 1, nt - 1), 0, 0)),
                  pl.BlockSpec(memory_space=pl.ANY),
                  pl.BlockSpec((tm, TOP_K), lambda i: (i, 0)),
                  row, row,
                  pl.BlockSpec((d, ds_), lambda i: (0, 0)),
                  pl.BlockSpec((d, ds_), lambda i: (0, 0)),
                  pl.BlockSpec((ds_, d), lambda i: (0, 0)),
                  pl.BlockSpec((None, 1, d), lambda i: (i // per, 0, 0)),
                  pl.BlockSpec((1, d), lambda i: (0, 0))],
        out_specs=row,
        scratch_shapes=[pltpu.VMEM((2, TOP_K, tm) + y_sorted.shape[1:], y_sorted.dtype),
                        pltpu.SemaphoreType.DMA((2,))],
        compiler_params=_params(("arbitrary",), 48),
        name="combine",
    )(dest3, dest3, y_sorted, gate_t, h2b, x1, wgs, wus, wds, gate_f, pnf.reshape(1, d))


def _layer(x, c, lb, w_ada, b_ada, pre_norm_mix, post_norm_mix, w_in, hgrn_norm, w_branch_attn, w_branch_hgrn,
           w_out, pre_norm_ffn, post_norm_ffn, w_router, router_bias, w_gate_e, w_up_e, w_down_e,
           w_gate_s, w_up_s, w_down_s):
    nb, s, d = x.shape
    t = nb * s
    x2 = x.reshape(t, d)

    mod = _ada(c, w_ada, b_ada).reshape(nb, 6, 1, d)
    shift_m, scale_m, gate_m, shift_f, scale_f, gate_f = (mod[:, k] for k in range(6))

    qkv = []
    for g, (_, dil) in enumerate(ATT_GROUPS):
        cols = jnp.concatenate([w_in[:, p * ATT_WIDTH + g * ATT_GW:p * ATT_WIDTH + (g + 1) * ATT_GW]
                                for p in range(3)], axis=1).astype(BF16)
        o = _inproj(x2, pre_norm_mix, shift_m, scale_m, cols, nb, s, dil, tn=3 * ATT_GW)
        qkv.append(o.reshape(nb, dil, s // dil, 3 * ATT_GW))
    proj2 = _inproj(x2, pre_norm_mix, shift_m, scale_m, w_in[:, QKV_COLS:].astype(BF16), nb, s)
    proj3 = proj2.reshape(nb, s, -1)

    att = [_attention_group(qkv[g], g) for g in range(len(ATT_GROUPS))]
    orec = _hgrn(proj3, lb, hgrn_norm)
    merged = _branch(att, orec, proj2, w_branch_attn.astype(BF16), w_branch_hgrn.astype(BF16), d, s)
    x1, h2t, h2b = _outproj(merged, w_out.astype(BF16), x2, gate_m, post_norm_mix, pre_norm_ffn,
                            shift_f, scale_f, s)

    eid, gate, pos, cnt = _router(h2b, w_router, router_bias)

    counts = cnt[:, 0]
    padded = (counts + MOE_BLOCK - 1) // MOE_BLOCK * MOE_BLOCK
    pad_end = jnp.cumsum(padded)
    pad_start = pad_end - padded
    n_blocks = -(-(t * TOP_K) // MOE_BLOCK) + N_EXPERTS
    onehot = eid[None] == jnp.arange(N_EXPERTS, dtype=jnp.int32)[:, None, None]
    dest = jnp.sum(jnp.where(onehot, pad_start[:, None, None], 0), axis=0) + pos
    block_start = jnp.arange(n_blocks, dtype=jnp.int32) * MOE_BLOCK
    block_expert = jnp.minimum(jnp.sum((block_start[:, None] >= pad_end[None, :]).astype(jnp.int32), axis=1),
                               N_EXPERTS - 1)
    n_used = (pad_end[-1:] // MOE_BLOCK).astype(jnp.int32)

    def tile_major(tile):
        return dest.reshape(TOP_K, t // tile, tile).transpose(1, 0, 2).reshape(t // tile, 1, TOP_K * tile)

    tm = _pick(s, (128,))
    dest3 = tile_major(tm)
    znum = jnp.concatenate([padded - counts, n_used]).astype(jnp.int32)
    x_sorted = _dispatch(h2t, dest3, (pad_start + counts).astype(jnp.int32), znum, n_blocks * MOE_BLOCK, tm)
    eidx = jnp.arange(N_EXPERTS, dtype=jnp.int32)
    has = counts > 0
    rank = jnp.cumsum(has.astype(jnp.int32)) - 1
    later = (eidx[None, :] > eidx[:, None]) & has[None, :]
    nxt_e = jnp.min(jnp.where(later, eidx[None, :], N_EXPERTS), axis=1)
    nxt_e = jnp.where(nxt_e == N_EXPERTS, -1, nxt_e).astype(jnp.int32)
    oh_b = block_expert[:, None] == eidx[None, :]
    pick = lambda v: jnp.sum(jnp.where(oh_b, v[None, :], 0), axis=1).astype(jnp.int32)
    first = ((block_start == pick(pad_start)) & (block_start < pad_end[-1])).astype(jnp.int32)
    y_sorted = _experts(x_sorted, block_expert, n_used, first, pick(rank % 2), pick(nxt_e),
                        w_gate_e, w_up_e, w_down_e)
    out = _combine(y_sorted, dest3, gate.T, h2b, x1, w_gate_s.astype(BF16), w_up_s.astype(BF16),
                   w_down_s.astype(BF16), gate_f, post_norm_ffn, s, tm)
    return out.reshape(nb, s, d)


def kernel(x, c, w_ada, b_ada, pre_norm_mix, post_norm_mix, w_in, hgrn_lb_logits, hgrn_norm, w_branch_attn,
           w_branch_hgrn, w_out, pre_norm_ffn, post_norm_ffn, w_router, router_bias, w_gate_e, w_up_e, w_down_e,
           w_gate_s, w_up_s, w_down_s):
    lb_table = jnp.cumsum(jax.nn.softmax(hgrn_lb_logits.astype(F32), axis=0), axis=0)
    depth = w_ada.shape[0]
    for l in range(depth):
        x = _layer(x, c, lb_table[l], w_ada[l], b_ada[l], pre_norm_mix[l], post_norm_mix[l], w_in[l],
                   hgrn_norm[l], w_branch_attn[l], w_branch_hgrn[l], w_out[l], pre_norm_ffn[l],
                   post_norm_ffn[l], w_router[l], router_bias[l], w_gate_e[l], w_up_e[l], w_down_e[l],
                   w_gate_s[l], w_up_s[l], w_down_s[l])
    return x
```

```python
import functools

import jax
import jax.numpy as jnp
from jax import lax
from jax.experimental import pallas as pl
from jax.experimental.pallas import tpu as pltpu

F32 = jnp.float32
BF16 = jnp.bfloat16

HEAD_DIM = 128
ATT_GROUPS = ((128, 1), (512, 4), (2048, 16))
ATT_HPG = 4
ATT_HEADS = ATT_HPG * len(ATT_GROUPS)
ATT_WIDTH = ATT_HEADS * HEAD_DIM
ATT_GW = ATT_HPG * HEAD_DIM
ATT_BLOCK = 128
ATT_OUT_COLS = ATT_GW + HEAD_DIM
LSE_LANES = HEAD_DIM // ATT_HPG
HGRN_HEADS = 8
HGRN_D = 128
HGRN_WIDTH = HGRN_HEADS * HGRN_D
HGRN_CHUNK = 64
HGRN_DIRECT = 8
N_EXPERTS = 64
N_GROUPS = 8
TOPK_GROUPS = 4
TOP_K = 8
ROUTED_SCALE = 2.5
MOE_BLOCK = 512
NORM_EPS = 1e-6
V7X_VMEM_LIMIT = 56 * 1024 * 1024

QKV_COLS = 3 * ATT_WIDTH
RQ_OFF = 0
RF_OFF = RQ_OFF + HGRN_WIDTH
RI_OFF = RF_OFF + HGRN_WIDTH
RG_OFF = RI_OFF + HGRN_WIDTH
GATE_OFF = RG_OFF + HGRN_WIDTH


def _pick(n, cands):
    for c in cands:
        if n % c == 0:
            return c
    raise ValueError(f"no tile of {cands} divides {n}")


def _params(sem, vmem_mib=None):
    kw = dict(dimension_semantics=sem)
    if vmem_mib is not None:
        kw["vmem_limit_bytes"] = min(vmem_mib * 1024 * 1024, V7X_VMEM_LIMIT)
    return pltpu.CompilerParams(**kw)


def _dot(a, b):
    return jnp.dot(a, b, preferred_element_type=F32)


def _dot_nt(a, b):
    return lax.dot_general(a, b, (((1,), (1,)), ((), ())), preferred_element_type=F32)


def _dot_tn(a, b):
    return lax.dot_general(a, b, (((0,), (0,)), ((), ())), preferred_element_type=F32)


def _rms(y):
    return y * lax.rsqrt(jnp.mean(y * y, axis=-1, keepdims=True) + NORM_EPS)


def _silu(a):
    return a * jax.nn.sigmoid(a)


ADA_KCHUNK = 128


def _ada_kernel(ct_ref, w_ref, b_ref, o_ref, *, nb, d):
    a = _silu(ct_ref[...])
    tn = w_ref.shape[1]
    accs = [jnp.zeros((8, tn), F32) for _ in range(nb)]
    for kc in range(d // ADA_KCHUNK):
        wc = w_ref[kc * ADA_KCHUNK:(kc + 1) * ADA_KCHUNK, :]
        ac = a[kc * ADA_KCHUNK:(kc + 1) * ADA_KCHUNK, :]
        for b in range(nb):
            p = ac[:, b:b + 1] * wc
            accs[b] = accs[b] + p.reshape(ADA_KCHUNK // 8, 8, tn).sum(axis=0)
    rows = [jnp.sum(acc, axis=0, keepdims=True) for acc in accs]
    o_ref[...] = jnp.concatenate(rows, axis=0) + b_ref[...]


def _ada(c, w_ada, b_ada):
    nb, d = c.shape
    n = w_ada.shape[1]
    tn = _pick(n, (512, 256, 128))
    return pl.pallas_call(
        functools.partial(_ada_kernel, nb=nb, d=d),
        out_shape=jax.ShapeDtypeStruct((nb, n), F32),
        grid=(n // tn,),
        in_specs=[pl.BlockSpec((d, nb), lambda j: (0, 0)),
                  pl.BlockSpec((d, tn), lambda j: (0, j)),
                  pl.BlockSpec((1, tn), lambda j: (0, j))],
        out_specs=pl.BlockSpec((nb, tn), lambda j: (0, j)),
        compiler_params=_params(("arbitrary",), 32),
        name="ada_mod",
    )(c.T, w_ada, b_ada.reshape(1, n))


DEINTERLEAVE_STRIDE = 4


def _inproj_kernel(x_ref, g_ref, sh_ref, sc_ref, w_ref, o_ref, h_scr, *slab, dil):
    @pl.when(pl.program_id(1) == 0)
    def _():
        y = _rms(x_ref[...]) * g_ref[...]
        h_scr[...] = (y * (1.0 + sc_ref[...]) + sh_ref[...]).astype(BF16)

    o = _dot(h_scr[...], w_ref[...])
    if dil == 1:
        o_ref[...] = o.astype(o_ref.dtype)
    else:
        slab_ref, slab2_ref = slab
        tm, tn = o.shape
        s1 = DEINTERLEAVE_STRIDE if dil > DEINTERLEAVE_STRIDE else dil
        s2 = dil // s1
        for c in range(tn // 128):
            slab_ref[c] = o[:, c * 128:(c + 1) * 128]
        for c in range(tn // 128):
            if s2 == 1:
                for r in range(dil):
                    o_ref[r, :, c * 128:(c + 1) * 128] = slab_ref[c, pl.ds(r, tm // dil, stride=dil), :].astype(o_ref.dtype)
            else:
                for r1 in range(s1):
                    slab2_ref[c, r1] = slab_ref[c, pl.ds(r1, tm // s1, stride=s1), :]
                for r1 in range(s1):
                    for r2 in range(s2):
                        o_ref[r1 + s1 * r2, :, c * 128:(c + 1) * 128] = (
                            slab2_ref[c, r1, pl.ds(r2, tm // dil, stride=s2), :].astype(o_ref.dtype))


def _inproj(x2, gain, shift, scale, w_bf, nb, seq, dil=1, tn=None):
    t, d = x2.shape
    n = w_bf.shape[1]
    tm = _pick(seq, (1024, 512, 256) if tn is None else (512, 256))
    tn = tn or _pick(n, (1024, 512, 256, 128))
    per = seq // tm
    if dil == 1:
        out_shape = jax.ShapeDtypeStruct((t, n), BF16)
        out_spec = pl.BlockSpec((tm, tn), lambda i, j: (i, j))
        scratch = [pltpu.VMEM((tm, d), BF16)]
    else:
        out_shape = jax.ShapeDtypeStruct((nb, dil, seq // dil, n), BF16)
        out_spec = pl.BlockSpec((None, dil, tm // dil, tn), lambda i, j: (i // per, 0, i % per, j))
        scratch = [pltpu.VMEM((tm, d), BF16), pltpu.VMEM((tn // 128, tm, 128), F32),
                   pltpu.VMEM((tn // 128, DEINTERLEAVE_STRIDE, tm // DEINTERLEAVE_STRIDE, 128), F32)]
    return pl.pallas_call(
        functools.partial(_inproj_kernel, dil=dil),
        out_shape=out_shape,
        grid=(t // tm, n // tn),
        in_specs=[pl.BlockSpec((tm, d), lambda i, j: (i, 0)),
                  pl.BlockSpec((1, d), lambda i, j: (0, 0)),
                  pl.BlockSpec((None, 1, d), lambda i, j: (i // per, 0, 0)),
                  pl.BlockSpec((None, 1, d), lambda i, j: (i // per, 0, 0)),
                  pl.BlockSpec((d, tn), lambda i, j: (0, j))],
        out_specs=out_spec,
        scratch_shapes=scratch,
        compiler_params=_params(("parallel", "arbitrary"), 48),
        name=f"in_proj_d{dil}_n{n}",
    )(x2, gain.reshape(1, d), shift, scale, w_bf)


def _attn_kernel(q_ref, kp_ref, kc_ref, vp_ref, vc_ref, o_ref, *, dil, slopes, scale, qb):
    n = pl.program_id(2)
    qi = lax.broadcasted_iota(jnp.int32, (ATT_BLOCK, ATT_BLOCK), 0)
    ki = lax.broadcasted_iota(jnp.int32, (ATT_BLOCK, ATT_BLOCK), 1)
    jc = qi - ki
    jp = jc + ATT_BLOCK
    first_lim = jnp.where(n > 0, ATT_BLOCK, -1)
    valid_c = jc >= 0
    lane = lax.broadcasted_iota(jnp.int32, (ATT_BLOCK, HEAD_DIM), 1)
    ones = jnp.ones((ATT_BLOCK, HEAD_DIM), BF16)
    for j in range(qb):
        rows = slice(j * ATT_BLOCK, (j + 1) * ATT_BLOCK)
        prow = slice((j - 1) * ATT_BLOCK, j * ATT_BLOCK)
        valid_p = jp <= (first_lim if j == 0 else ATT_BLOCK)
        lse_blk = jnp.zeros((ATT_BLOCK, HEAD_DIM), F32)
        for h in range(ATT_HPG):
            hs = slice(h * HEAD_DIM, (h + 1) * HEAD_DIM)
            q = q_ref[rows, hs]
            k_p = kp_ref[:, hs] if j == 0 else kc_ref[prow, hs]
            v_p = vp_ref[:, hs] if j == 0 else vc_ref[prow, hs]
            bias = slopes[h] * dil
            s_c = _dot_nt(q, kc_ref[rows, hs]) * scale - bias * jc.astype(F32)
            s_p = _dot_nt(q, k_p) * scale - bias * jp.astype(F32)
            s_c = jnp.where(valid_c, s_c, -jnp.inf)
            s_p = jnp.where(valid_p, s_p, -jnp.inf)
            m = jnp.maximum(jnp.max(s_c, axis=-1, keepdims=True), jnp.max(s_p, axis=-1, keepdims=True))
            p_c = jnp.exp(s_c - m).astype(BF16)
            p_p = jnp.exp(s_p - m).astype(BF16)
            oa = (_dot(p_c, jnp.concatenate([vc_ref[rows, hs], ones], axis=1))
                  + _dot(p_p, jnp.concatenate([v_p, ones], axis=1)))
            l = oa[:, HEAD_DIM:HEAD_DIM + 1]
            o_ref[rows, hs] = oa[:, :HEAD_DIM] / l
            lse = m + jnp.log(l)
            lse_blk = jnp.where((lane >= h * LSE_LANES) & (lane < (h + 1) * LSE_LANES), lse, lse_blk)
        o_ref[rows, ATT_GW:] = lse_blk


def _attention_group(qkv, g):
    nb_, dil, l, _ = qkv.shape
    window, dil_ = ATT_GROUPS[g]
    assert dil == dil_ and window // dil == ATT_BLOCK and l % ATT_BLOCK == 0
    qb = _pick(l // ATT_BLOCK, (4, 2, 1))
    slopes = tuple(2.0 ** (-8.0 * (g * ATT_HPG + i + 1) / ATT_HEADS) for i in range(ATT_HPG))

    def spec(part, prev):
        if prev:
            return pl.BlockSpec((None, None, ATT_BLOCK, ATT_GW),
                                lambda b, r, n: (b, r, jnp.maximum(n * qb - 1, 0), part))
        return pl.BlockSpec((None, None, qb * ATT_BLOCK, ATT_GW), lambda b, r, n: (b, r, n, part))

    return pl.pallas_call(
        functools.partial(_attn_kernel, dil=float(dil), slopes=slopes, scale=HEAD_DIM ** -0.5, qb=qb),
        out_shape=jax.ShapeDtypeStruct((nb_, dil, l, ATT_OUT_COLS), F32),
        grid=(nb_, dil, l // (qb * ATT_BLOCK)),
        in_specs=[spec(0, False), spec(1, True), spec(1, False), spec(2, True), spec(2, False)],
        out_specs=pl.BlockSpec((None, None, qb * ATT_BLOCK, ATT_OUT_COLS), lambda b, r, n: (b, r, n, 0)),
        compiler_params=_params(("parallel", "parallel", "arbitrary")),
        name=f"attn_g{g}",
    )(qkv, qkv, qkv, qkv, qkv)


def _hgrn_kernel(q_ref, f_ref, i_ref, g_ref, lb_ref, gn_ref, tri_ref, o_ref, st_ref, sh_ref, *, tb, hp):
    @pl.when(pl.program_id(2) == 0)
    def _():
        st_ref[...] = jnp.zeros_like(st_ref)
        sh_ref[:, :, :, 0:HGRN_DIRECT, :] = jnp.zeros((hp, 3, tb // HGRN_DIRECT, HGRN_DIRECT, HGRN_D), F32)

    for hh in range(hp):
        cols = slice(hh * HGRN_D, (hh + 1) * HGRN_D)
        o_ref[:, cols] = _hgrn_head(q_ref[:, cols], f_ref[:, cols], i_ref[:, cols], g_ref[:, cols],
                                    lb_ref[:, cols], gn_ref[:, cols], tri_ref[...], st_ref.at[hh], sh_ref.at[hh],
                                    tb).astype(o_ref.dtype)


def _hgrn_head(q_in, f_in, i_in, g_in, lb, gn, tri, st_ref, sh_ref, tb):
    c_ = HGRN_CHUNK
    nc = tb // c_
    q = q_in.astype(F32)
    v = i_in.astype(F32)
    f = lb + (1.0 - lb) * jax.nn.sigmoid(f_in.astype(F32))
    kk = 1.0 - f
    lf = jnp.log(f)

    hi = lf.astype(BF16)
    r1 = lf - hi.astype(F32)
    mid = r1.astype(BF16)
    lo = (r1 - mid.astype(F32)).astype(BF16)
    b = _dot(tri, hi) + _dot(tri, mid) + _dot(tri, lo)

    def ref_rows(blk, row):
        b3 = b.reshape(tb // blk, blk, HGRN_D)
        return jnp.broadcast_to(b3[:, row:row + 1, :], (tb // blk, blk, HGRN_D)).reshape(tb, HGRN_D)

    ti = lax.broadcasted_iota(jnp.int32, (c_, c_), 0)
    si = lax.broadcasted_iota(jnp.int32, (c_, c_), 1)
    ssum = jnp.zeros((nc, c_, c_), F32)
    half = c_ // 2
    while half >= HGRN_DIRECT:
        bref = ref_rows(2 * half, half)
        ql = (q * jnp.exp(b - bref)).astype(BF16).reshape(nc, c_, HGRN_D)
        kl = (kk * jnp.exp(bref - b)).astype(BF16).reshape(nc, c_, HGRN_D)
        s_l = jnp.einsum("ctk,csk->cts", ql, kl, preferred_element_type=F32)
        mask = ((ti // (2 * half)) == (si // (2 * half))) & (((ti // half) % 2) == 1) & (((si // half) % 2) == 0)
        ssum = ssum + jnp.where(mask[None], s_l, 0.0)
        half //= 2
    v3 = v.astype(BF16).reshape(nc, c_, HGRN_D)
    o_acc = jnp.einsum("cts,csv->ctv", ssum.astype(BF16), v3, preferred_element_type=F32).reshape(tb, HGRN_D)

    nt8 = tb // HGRN_DIRECT
    for plane, val in enumerate((kk, f, v)):
        sh_ref[plane, :, HGRN_DIRECT:, :] = val.reshape(nt8, HGRN_DIRECT, HGRN_D)

    def shifted(plane, lag):
        st = HGRN_DIRECT - lag
        return sh_ref[plane, :, st:st + HGRN_DIRECT, :].reshape(tb, HGRN_D)

    ones = jnp.ones((HGRN_D, HGRN_D), BF16)
    decay = None
    for lag in range(HGRN_DIRECT):
        if lag == 0:
            w = q * kk
        else:
            decay = f if lag == 1 else decay * shifted(1, lag - 1)
            w = q * shifted(0, lag) * decay
        o_acc = o_acc + _dot(w.astype(BF16), ones) * (v if lag == 0 else shifted(2, lag))

    blast = ref_rows(c_, c_ - 1)
    qe = (q * jnp.exp(b)).astype(BF16)
    kt = (kk * jnp.exp(blast - b)).astype(BF16)
    vb = v.astype(BF16)
    st_t = st_ref[...]
    inter = []
    for c in range(nc):
        rows = slice(c * c_, (c + 1) * c_)
        inter.append(_dot_nt(qe[rows], st_t.astype(BF16)))
        dec = jnp.exp(blast[c * c_:c * c_ + 1, :])
        st_t = st_t * dec + _dot_tn(vb[rows], kt[rows])
    st_ref[...] = st_t
    o = o_acc + jnp.concatenate(inter, axis=0)

    return (_rms(o) * gn) * _silu(g_in.astype(F32))


HGRN_HEADS_PER_STEP = 4


def _hgrn(proj3, lb, gnorm):
    nb_, s, nc = proj3.shape
    tb = _pick(s, (512, 256, 128, 64))
    hp = HGRN_HEADS_PER_STEP
    w = hp * HGRN_D
    idx = jnp.arange(tb)
    tri = ((idx[:, None] // HGRN_CHUNK == idx[None, :] // HGRN_CHUNK) & (idx[None, :] <= idx[:, None])).astype(BF16)

    def spec(off):
        cb = off // w
        return pl.BlockSpec((None, tb, w), lambda b, h, n: (b, n, cb + h))

    vec = pl.BlockSpec((1, w), lambda b, h, n: (0, h))
    out = pl.pallas_call(
        functools.partial(_hgrn_kernel, tb=tb, hp=hp),
        out_shape=jax.ShapeDtypeStruct((nb_, s, HGRN_WIDTH), BF16),
        grid=(nb_, HGRN_HEADS // hp, s // tb),
        in_specs=[spec(RQ_OFF), spec(RF_OFF), spec(RI_OFF), spec(RG_OFF), vec, vec,
                  pl.BlockSpec((tb, tb), lambda b, h, n: (0, 0))],
        out_specs=pl.BlockSpec((None, tb, w), lambda b, h, n: (b, n, h)),
        scratch_shapes=[pltpu.VMEM((hp, HGRN_D, HGRN_D), F32),
                        pltpu.VMEM((hp, 3, tb // HGRN_DIRECT, 2 * HGRN_DIRECT, HGRN_D), F32)],
        compiler_params=_params(("parallel", "parallel", "arbitrary"), 32),
        name="hgrn2",
    )(proj3, proj3, proj3, proj3, lb.reshape(1, HGRN_WIDTH), gnorm.reshape(1, HGRN_WIDTH), tri)
    return out.reshape(nb_ * s, HGRN_WIDTH)


def _branch_kernel(a0_ref, a1_ref, a2_ref, r_ref, ga_ref, gb_ref, wa_ref, wr_ref, o_ref, att_scr, nat_scr,
                   *, dils):
    tm = att_scr.shape[0]
    nslab = ATT_OUT_COLS // 128
    for gi, (a_ref, dil) in enumerate(zip((a1_ref, a2_ref), dils)):
        for r in range(dil):
            for c in range(nslab):
                nat_scr[gi, c, pl.ds(r, tm // dil, stride=dil), :] = a_ref[r, :, c * 128:(c + 1) * 128]

    def head(g, h):
        return a0_ref[:, h * HEAD_DIM:(h + 1) * HEAD_DIM] if g == 0 else nat_scr[g - 1, h]

    def lse(g, h):
        if g == 0:
            return a0_ref[:, ATT_GW + h * LSE_LANES:ATT_GW + h * LSE_LANES + 1]
        return nat_scr[g - 1, ATT_HPG, :, h * LSE_LANES:h * LSE_LANES + 1]

    for h in range(ATT_HPG):
        ls = [lse(g, h) for g in range(3)]
        m = jnp.maximum(jnp.maximum(ls[0], ls[1]), ls[2])
        es = [jnp.exp(l - m) for l in ls]
        den = es[0] + es[1] + es[2]
        o = ((es[0] / den) * head(0, h) + (es[1] / den) * head(1, h)) + (es[2] / den) * head(2, h)
        att_scr[:, h * HEAD_DIM:(h + 1) * HEAD_DIM] = o.astype(BF16)

    a = _dot(att_scr[...], wa_ref[...])
    r = _dot(r_ref[...], wr_ref[...])
    o_ref[...] = (jax.nn.sigmoid(ga_ref[...].astype(F32)) * a
                  + jax.nn.sigmoid(gb_ref[...].astype(F32)) * r).astype(o_ref.dtype)


def _branch(att, orec, proj2, wa_bf, wr_bf, d, seq):
    t = orec.shape[0]
    tm = _pick(seq, (256,))
    per = seq // tm
    assert GATE_OFF % d == 0
    ga0 = GATE_OFF // d
    dils = tuple(dil for _, dil in ATT_GROUPS[1:])

    def aspec(dil):
        return pl.BlockSpec((None, dil, tm // dil, ATT_OUT_COLS), lambda i: (i // per, 0, i % per, 0))

    return pl.pallas_call(
        functools.partial(_branch_kernel, dils=dils),
        out_shape=jax.ShapeDtypeStruct((t, d), BF16),
        grid=(t // tm,),
        in_specs=[pl.BlockSpec((tm, ATT_OUT_COLS), lambda i: (i, 0)), aspec(dils[0]), aspec(dils[1]),
                  pl.BlockSpec((tm, HGRN_WIDTH), lambda i: (i, 0)),
                  pl.BlockSpec((tm, d), lambda i: (i, ga0)),
                  pl.BlockSpec((tm, d), lambda i: (i, ga0 + 1)),
                  pl.BlockSpec((ATT_GW, d), lambda i: (0, 0)),
                  pl.BlockSpec((HGRN_WIDTH, d), lambda i: (0, 0))],
        out_specs=pl.BlockSpec((tm, d), lambda i: (i, 0)),
        scratch_shapes=[pltpu.VMEM((tm, ATT_GW), BF16),
                        pltpu.VMEM((len(dils), ATT_OUT_COLS // 128, tm, 128), F32)],
        compiler_params=_params(("parallel",), 48),
        name="branch_merge",
    )(att[0].reshape(t, ATT_OUT_COLS), att[1], att[2], orec, proj2, proj2, wa_bf, wr_bf)


def _rows_to_tiles(a):
    m, n = a.shape
    return jnp.swapaxes(jnp.stack([a[:, c * 128:(c + 1) * 128] for c in range(n // 128)], axis=0), 0, 1)


def _tiles_to_rows(a):
    at = jnp.swapaxes(a, 0, 1)
    return jnp.concatenate([at[c] for c in range(a.shape[1])], axis=1)


def _outproj_kernel(m_ref, w_ref, x_ref, gm_ref, pnm_ref, pnf_ref, sh_ref, sc_ref, x1_ref, ht_ref, hb_ref):
    y = _dot(m_ref[...], w_ref[...])
    x1 = x_ref[...] + gm_ref[...] * (_rms(y) * pnm_ref[...])
    x1_ref[...] = x1
    h2 = ((_rms(x1) * pnf_ref[...]) * (1.0 + sc_ref[...]) + sh_ref[...]).astype(BF16)
    hb_ref[...] = h2
    ht_ref[...] = _rows_to_tiles(h2)


def _outproj(merged, w_bf, x2, gate_m, pnm, pnf, shift_f, scale_f, seq):
    t, d = x2.shape
    tm = _pick(seq, (256, 128))
    per = seq // tm
    row = pl.BlockSpec((tm, d), lambda i: (i, 0))
    vec = pl.BlockSpec((1, d), lambda i: (0, 0))
    bvec = pl.BlockSpec((None, 1, d), lambda i: (i // per, 0, 0))
    return pl.pallas_call(
        _outproj_kernel,
        out_shape=(jax.ShapeDtypeStruct((t, d), F32), jax.ShapeDtypeStruct((t, d // 128, 128), BF16),
                   jax.ShapeDtypeStruct((t, d), BF16)),
        grid=(t // tm,),
        in_specs=[row, pl.BlockSpec((d, d), lambda i: (0, 0)), row, bvec, vec, vec, bvec, bvec],
        out_specs=(row, pl.BlockSpec((tm, d // 128, 128), lambda i: (i, 0, 0)), row),
        compiler_params=_params(("parallel",), 48),
        name="out_proj",
    )(merged, w_bf, x2, gate_m, pnm.reshape(1, d), pnf.reshape(1, d), shift_f, scale_f)


def _router_kernel(h_ref, w_ref, bias_ref, up_ref, eid_ref, gate_ref, pos_ref, cnt_ref, carry_ref, *, tr):
    @pl.when(pl.program_id(0) == 0)
    def _():
        carry_ref[...] = jnp.zeros_like(carry_ref)

    per_group = N_EXPERTS // N_GROUPS
    sig = jax.nn.sigmoid(_dot_nt(w_ref[...], h_ref[...]))
    choice = sig + bias_ref[...]
    eidx = lax.broadcasted_iota(jnp.int32, (N_EXPERTS, tr), 0)

    c3 = choice.reshape(N_GROUPS, per_group, tr)
    sub = lax.broadcasted_iota(jnp.int32, (N_GROUPS, per_group, tr), 1)
    m1 = jnp.max(c3, axis=1, keepdims=True)
    first = jnp.min(jnp.where(c3 == m1, sub, per_group), axis=1, keepdims=True)
    m2 = jnp.max(jnp.where(sub == first, -jnp.inf, c3), axis=1, keepdims=True)
    gs = (m1 + m2).reshape(N_GROUPS, tr)

    gidx = lax.broadcasted_iota(jnp.int32, (N_GROUPS, tr), 0)
    grank = jnp.zeros((N_GROUPS, tr), jnp.int32)
    for g in range(N_GROUPS):
        row = gs[g:g + 1, :]
        grank = grank + ((row > gs) | ((row == gs) & (gidx > g))).astype(jnp.int32)
    gsel = jnp.where(grank < TOPK_GROUPS, 1.0, 0.0)
    emask = jnp.broadcast_to(gsel.reshape(N_GROUPS, 1, tr), (N_GROUPS, per_group, tr)).reshape(N_EXPERTS, tr)
    cm = jnp.where(emask > 0.5, choice, -jnp.inf)

    rank = jnp.zeros((N_EXPERTS, tr), jnp.int32)
    for e in range(N_EXPERTS):
        row = cm[e:e + 1, :]
        rank = rank + ((row > cm) | ((row == cm) & (eidx > e))).astype(jnp.int32)
    sel = rank < TOP_K

    denom = jnp.sum(jnp.where(sel, sig, 0.0), axis=0, keepdims=True)
    gate_full = sig / denom * ROUTED_SCALE

    sel_b = jnp.where(sel, 1.0, 0.0).astype(BF16)
    carry = carry_ref[...]
    cum = _dot(sel_b, up_ref[...]) + jnp.concatenate([carry] * (tr // 128), axis=1)
    carry_new = carry + _dot(sel_b, jnp.ones((tr, 128), BF16))
    carry_ref[...] = carry_new
    cnt_ref[...] = carry_new.astype(jnp.int32)
    posi = cum.astype(jnp.int32)

    eids, gates, poss = [], [], []
    for r in range(TOP_K):
        hit = rank == r
        eids.append(jnp.sum(jnp.where(hit, eidx, 0), axis=0, keepdims=True))
        gates.append(jnp.sum(jnp.where(hit, gate_full, 0.0), axis=0, keepdims=True))
        poss.append(jnp.sum(jnp.where(hit, posi, 0), axis=0, keepdims=True))
    eid_ref[...] = jnp.concatenate(eids, axis=0)
    gate_ref[...] = jnp.concatenate(gates, axis=0)
    pos_ref[...] = jnp.concatenate(poss, axis=0)


def _router(h2b, w_router, router_bias):
    t, d = h2b.shape
    tr = _pick(t, (256, 128))
    idx = jnp.arange(tr)
    upper = (idx[:, None] < idx[None, :]).astype(BF16)
    kout = pl.BlockSpec((TOP_K, tr), lambda i: (0, i))
    return pl.pallas_call(
        functools.partial(_router_kernel, tr=tr),
        out_shape=(jax.ShapeDtypeStruct((TOP_K, t), jnp.int32), jax.ShapeDtypeStruct((TOP_K, t), F32),
                   jax.ShapeDtypeStruct((TOP_K, t), jnp.int32), jax.ShapeDtypeStruct((N_EXPERTS, 128), jnp.int32)),
        grid=(t // tr,),
        in_specs=[pl.BlockSpec((tr, d), lambda i: (i, 0)),
                  pl.BlockSpec((N_EXPERTS, d), lambda i: (0, 0)),
                  pl.BlockSpec((N_EXPERTS, 1), lambda i: (0, 0)),
                  pl.BlockSpec((tr, tr), lambda i: (0, 0))],
        out_specs=(kout, kout, kout, pl.BlockSpec((N_EXPERTS, 128), lambda i: (0, 0))),
        scratch_shapes=[pltpu.VMEM((N_EXPERTS, 128), F32)],
        compiler_params=_params(("arbitrary",), 32),
        name="router",
    )(h2b, w_router.T.astype(BF16), router_bias.reshape(N_EXPERTS, 1).astype(F32), upper)


ZERO_RUN = 64


DISPATCH_RING = 3


def _dispatch_kernel(zs_ref, zn_ref, d_ref, h_hbm, xs_hbm, ring, zero_scr, lsem, sem, zsem, *, tm, n_blocks):
    i = pl.program_id(0)
    nsteps = pl.num_programs(0)
    cur = i % DISPATCH_RING

    def load(j, s):
        return pltpu.make_async_copy(h_hbm.at[pl.ds(j * tm, tm)], ring.at[s], lsem.at[s])

    def wait_rows(s):
        for k in range(TOP_K):
            pltpu.make_async_copy(ring.at[s], xs_hbm.at[pl.ds(0, tm)], sem.at[s]).wait()

    @pl.when(i == 0)
    def _():
        load(0, 0).start()

    @pl.when(i + 1 < nsteps)
    def _():
        load(i + 1, (i + 1) % DISPATCH_RING).start()

    load(i, cur).wait()

    for r in range(tm):
        for k in range(TOP_K):
            pltpu.make_async_copy(ring.at[cur, pl.ds(r, 1)], xs_hbm.at[pl.ds(d_ref[0, k * tm + r], 1)],
                                  sem.at[cur]).start(priority=(r + k) % 2)

    @pl.when(i == 0)
    def _():
        zero_scr[...] = jnp.zeros_like(zero_scr)
        for e in range(N_EXPERTS):
            nrun = zn_ref[e] // ZERO_RUN

            def zrun(u, carry, e=e):
                pltpu.make_async_copy(zero_scr.at[pl.ds(0, ZERO_RUN)],
                                      xs_hbm.at[pl.ds(zs_ref[e] + u * ZERO_RUN, ZERO_RUN)], zsem).start()
                return carry
            lax.fori_loop(0, nrun, zrun, 0)

            def zrow(u, carry, e=e, nrun=nrun):
                pltpu.make_async_copy(zero_scr.at[pl.ds(0, 1)],
                                      xs_hbm.at[pl.ds(zs_ref[e] + nrun * ZERO_RUN + u, 1)], zsem).start()
                return carry
            lax.fori_loop(0, zn_ref[e] - nrun * ZERO_RUN, zrow, 0)
        for e in range(N_EXPERTS):
            nrun = zn_ref[e] // ZERO_RUN

            def zrun_wait(u, carry):
                pltpu.make_async_copy(zero_scr.at[pl.ds(0, ZERO_RUN)], xs_hbm.at[pl.ds(0, ZERO_RUN)], zsem).wait()
                return carry
            lax.fori_loop(0, nrun, zrun_wait, 0)

            def zrow_wait(u, carry):
                pltpu.make_async_copy(zero_scr.at[pl.ds(0, 1)], xs_hbm.at[pl.ds(0, 1)], zsem).wait()
                return carry
            lax.fori_loop(0, zn_ref[e] - nrun * ZERO_RUN, zrow_wait, 0)

        def tbody(blk, carry):
            pltpu.make_async_copy(zero_scr, xs_hbm.at[pl.ds(blk * MOE_BLOCK, MOE_BLOCK)], zsem).start()
            return carry
        lax.fori_loop(zn_ref[N_EXPERTS], n_blocks, tbody, 0)

        def twait(blk, carry):
            pltpu.make_async_copy(zero_scr, xs_hbm.at[pl.ds(0, MOE_BLOCK)], zsem).wait()
            return carry
        lax.fori_loop(zn_ref[N_EXPERTS], n_blocks, twait, 0)

    @pl.when(i > 0)
    def _():
        wait_rows((i + DISPATCH_RING - 1) % DISPATCH_RING)

    @pl.when(i == nsteps - 1)
    def _():
        wait_rows(cur)


def _dispatch(h2t, dest3, zstart, znum, n_slots, tm):
    t, nch, _ = h2t.shape
    grid_spec = pltpu.PrefetchScalarGridSpec(
        num_scalar_prefetch=2,
        grid=(t // tm,),
        in_specs=[pl.BlockSpec((None, 1, TOP_K * tm), lambda i, zs, zn: (i, 0, 0), memory_space=pltpu.SMEM),
                  pl.BlockSpec(memory_space=pl.ANY)],
        out_specs=pl.BlockSpec(memory_space=pl.ANY),
        scratch_shapes=[pltpu.VMEM((DISPATCH_RING, tm, nch, 128), h2t.dtype),
                        pltpu.VMEM((MOE_BLOCK, nch, 128), h2t.dtype),
                        pltpu.SemaphoreType.DMA((DISPATCH_RING,)), pltpu.SemaphoreType.DMA((DISPATCH_RING,)),
                        pltpu.SemaphoreType.DMA(())],
    )
    return pl.pallas_call(
        functools.partial(_dispatch_kernel, tm=tm, n_blocks=n_slots // MOE_BLOCK),
        out_shape=jax.ShapeDtypeStruct((n_slots, nch, 128), h2t.dtype),
        grid_spec=grid_spec,
        compiler_params=_params(("arbitrary",), 32),
        name="dispatch",
    )(zstart, znum, dest3, h2t)


def _expert_kernel(be_ref, nu_ref, first_ref, par_ref, nxt_ref, x_ref, wg_hbm, wu_hbm, wd_hbm, o_ref,
                   wg_f, wu_f, wd_f, wg_bf, wu_bf, wd_bf, sem):
    i = pl.program_id(0)

    def weight_copies(e, s):
        return [pltpu.make_async_copy(src.at[e], dst.at[s], sem.at[s])
                for src, dst in ((wg_hbm, wg_f), (wu_hbm, wu_f), (wd_hbm, wd_f))]

    @pl.when(i == 0)
    def _():
        for cp in weight_copies(be_ref[0], 0):
            cp.start()

    @pl.when(first_ref[i] == 1)
    def _():
        p = par_ref[i]
        for cp in weight_copies(be_ref[i], p):
            cp.wait()
        wg_bf[...] = wg_f[p].astype(BF16)
        wu_bf[...] = wu_f[p].astype(BF16)
        wd_bf[...] = wd_f[p].astype(BF16)

        @pl.when(nxt_ref[i] >= 0)
        def _():
            for cp in weight_copies(nxt_ref[i], 1 - p):
                cp.start()

    @pl.when(i < nu_ref[0])
    def _():
        xb = _tiles_to_rows(x_ref[...])
        hid = _silu(_dot(xb, wg_bf[...])) * _dot(xb, wu_bf[...])
        o_ref[...] = _rows_to_tiles(_dot(hid.astype(BF16), wd_bf[...]).astype(o_ref.dtype))

    @pl.when(i >= nu_ref[0])
    def _():
        o_ref[...] = jnp.zeros_like(o_ref)


def _experts(x_sorted, block_expert, n_used, first, par, nxt, wg, wu, wd):
    n_slots, nch, _ = x_sorted.shape
    d = nch * 128
    n_blocks = block_expert.shape[0]
    de = wg.shape[-1]
    hbm = pl.BlockSpec(memory_space=pl.ANY)
    grid_spec = pltpu.PrefetchScalarGridSpec(
        num_scalar_prefetch=5,
        grid=(n_blocks,),
        in_specs=[pl.BlockSpec((MOE_BLOCK, nch, 128), lambda i, be, nu, *_: (jnp.minimum(i, nu[0] - 1), 0, 0)),
                  hbm, hbm, hbm],
        out_specs=pl.BlockSpec((MOE_BLOCK, nch, 128), lambda i, *_: (i, 0, 0)),
        scratch_shapes=[pltpu.VMEM((2, d, de), F32), pltpu.VMEM((2, d, de), F32), pltpu.VMEM((2, de, d), F32),
                        pltpu.VMEM((d, de), BF16), pltpu.VMEM((d, de), BF16), pltpu.VMEM((de, d), BF16),
                        pltpu.SemaphoreType.DMA((2,))],
    )
    return pl.pallas_call(
        _expert_kernel,
        out_shape=jax.ShapeDtypeStruct((n_slots, nch, 128), BF16),
        grid_spec=grid_spec,
        compiler_params=_params(("arbitrary",), 56),
        name="experts",
    )(block_expert, n_used, first, par, nxt, x_sorted, wg, wu, wd)


def _combine_kernel(dc_ref, dn_ref, y_hbm, gt_ref, h_ref, x1_ref, wg_ref, wu_ref, wd_ref, gf_ref, pn_ref,
                    o_ref, buf, sem, *, tm):
    i = pl.program_id(0)
    nsteps = pl.num_programs(0)
    slot = i % 2

    def issue_tile(d_ref, s):
        for r in range(tm):
            for k in range(TOP_K):
                pltpu.make_async_copy(y_hbm.at[pl.ds(d_ref[0, k * tm + r], 1)], buf.at[s, k, pl.ds(r, 1)],
                                      sem.at[s]).start(priority=(r + k) % 2)

    @pl.when(i == 0)
    def _():
        issue_tile(dc_ref, 0)

    @pl.when(i + 1 < nsteps)
    def _():
        issue_tile(dn_ref, 1 - slot)

    hb = h_ref[...]
    shared = _dot((_silu(_dot(hb, wg_ref[...])) * _dot(hb, wu_ref[...])).astype(BF16), wd_ref[...])

    for k in range(TOP_K):
        pltpu.make_async_copy(y_hbm.at[pl.ds(0, tm)], buf.at[slot, k], sem.at[slot]).wait()

    gt = gt_ref[...]
    acc = shared
    for k in range(TOP_K):
        acc = acc + gt[:, k:k + 1] * _tiles_to_rows(buf[slot, k]).astype(F32)
    o_ref[...] = x1_ref[...] + gf_ref[...] * (_rms(acc) * pn_ref[...])


def _combine(y_sorted, dest3, gate_t, h2b, x1, wgs, wus, wds, gate_f, pnf, seq, tm):
    t, d = x1.shape
    ds_ = wgs.shape[-1]
    per = seq // tm
    nt = t // tm
    smem = functools.partial(pl.BlockSpec, memory_space=pltpu.SMEM)
    row = pl.BlockSpec((tm, d), lambda i: (i, 0))
    return pl.pallas_call(
        functools.partial(_combine_kernel, tm=tm),
        out_shape=jax.ShapeDtypeStruct((t, d), F32),
        grid=(nt,),
        in_specs=[smem((None, 1, TOP_K * tm), lambda i: (i, 0, 0)),
                  smem((None, 1, TOP_K * tm), lambda i: (jnp.minimum(i + 1, nt - 1), 0, 0)),
                  pl.BlockSpec(memory_space=pl.ANY),
                  pl.BlockSpec((tm, TOP_K), lambda i: (i, 0)),
                  row, row,
                  pl.BlockSpec((d, ds_), lambda i: (0, 0)),
                  pl.BlockSpec((d, ds_), lambda i: (0, 0)),
                  pl.BlockSpec((ds_, d), lambda i: (0, 0)),
                  pl.BlockSpec((None, 1, d), lambda i: (i // per, 0, 0)),
                  pl.BlockSpec((1, d), lambda i: (0, 0))],
        out_specs=row,
        scratch_shapes=[pltpu.VMEM((2, TOP_K, tm) + y_sorted.shape[1:], y_sorted.dtype),
                        pltpu.SemaphoreType.DMA((2,))],
        compiler_params=_params(("arbitrary",), 48),
        name="combine",
    )(dest3, dest3, y_sorted, gate_t, h2b, x1, wgs, wus, wds, gate_f, pnf.reshape(1, d))


def _layer(x, c, lb, w_ada, b_ada, pre_norm_mix, post_norm_mix, w_in, hgrn_norm, w_branch_attn, w_branch_hgrn,
           w_out, pre_norm_ffn, post_norm_ffn, w_router, router_bias, w_gate_e, w_up_e, w_down_e,
           w_gate_s, w_up_s, w_down_s):
    nb, s, d = x.shape
    t = nb * s
    x2 = x.reshape(t, d)

    mod = _ada(c, w_ada, b_ada).reshape(nb, 6, 1, d)
    shift_m, scale_m, gate_m, shift_f, scale_f, gate_f = (mod[:, k] for k in range(6))

    qkv = []
    for g, (_, dil) in enumerate(ATT_GROUPS):
        cols = jnp.concatenate([w_in[:, p * ATT_WIDTH + g * ATT_GW:p * ATT_WIDTH + (g + 1) * ATT_GW]
                                for p in range(3)], axis=1).astype(BF16)
        o = _inproj(x2, pre_norm_mix, shift_m, scale_m, cols, nb, s, dil, tn=3 * ATT_GW)
        qkv.append(o.reshape(nb, dil, s // dil, 3 * ATT_GW))
    proj2 = _inproj(x2, pre_norm_mix, shift_m, scale_m, w_in[:, QKV_COLS:].astype(BF16), nb, s)
    proj3 = proj2.reshape(nb, s, -1)

    att = [_attention_group(qkv[g], g) for g in range(len(ATT_GROUPS))]
    orec = _hgrn(proj3, lb, hgrn_norm)
    merged = _branch(att, orec, proj2, w_branch_attn.astype(BF16), w_branch_hgrn.astype(BF16), d, s)
    x1, h2t, h2b = _outproj(merged, w_out.astype(BF16), x2, gate_m, post_norm_mix, pre_norm_ffn,
                            shift_f, scale_f, s)

    eid, gate, pos, cnt = _router(h2b, w_router, router_bias)

    counts = cnt[:, 0]
    padded = (counts + MOE_BLOCK - 1) // MOE_BLOCK * MOE_BLOCK
    pad_end = jnp.cumsum(padded)
    pad_start = pad_end - padded
    n_blocks = -(-(t * TOP_K) // MOE_BLOCK) + N_EXPERTS
    onehot = eid[None] == jnp.arange(N_EXPERTS, dtype=jnp.int32)[:, None, None]
    dest = jnp.sum(jnp.where(onehot, pad_start[:, None, None], 0), axis=0) + pos
    block_start = jnp.arange(n_blocks, dtype=jnp.int32) * MOE_BLOCK
    block_expert = jnp.minimum(jnp.sum((block_start[:, None] >= pad_end[None, :]).astype(jnp.int32), axis=1),
                               N_EXPERTS - 1)
    n_used = (pad_end[-1:] // MOE_BLOCK).astype(jnp.int32)

    def tile_major(tile):
        return dest.reshape(TOP_K, t // tile, tile).transpose(1, 0, 2).reshape(t // tile, 1, TOP_K * tile)

    tm = _pick(s, (128,))
    dest3 = tile_major(tm)
    znum = jnp.concatenate([padded - counts, n_used]).astype(jnp.int32)
    x_sorted = _dispatch(h2t, dest3, (pad_start + counts).astype(jnp.int32), znum, n_blocks * MOE_BLOCK, tm)
    eidx = jnp.arange(N_EXPERTS, dtype=jnp.int32)
    has = counts > 0
    rank = jnp.cumsum(has.astype(jnp.int32)) - 1
    later = (eidx[None, :] > eidx[:, None]) & has[None, :]
    nxt_e = jnp.min(jnp.where(later, eidx[None, :], N_EXPERTS), axis=1)
    nxt_e = jnp.where(nxt_e == N_EXPERTS, -1, nxt_e).astype(jnp.int32)
    oh_b = block_expert[:, None] == eidx[None, :]
    pick = lambda v: jnp.sum(jnp.where(oh_b, v[None, :], 0), axis=1).astype(jnp.int32)
    first = ((block_start == pick(pad_start)) & (block_start < pad_end[-1])).astype(jnp.int32)
    y_sorted = _experts(x_sorted, block_expert, n_used, first, pick(rank % 2), pick(nxt_e),
                        w_gate_e, w_up_e, w_down_e)
    out = _combine(y_sorted, dest3, gate.T, h2b, x1, w_gate_s.astype(BF16), w_up_s.astype(BF16),
                   w_down_s.astype(BF16), gate_f, post_norm_ffn, s, tm)
    return out.reshape(nb, s, d)


def kernel(x, c, w_ada, b_ada, pre_norm_mix, post_norm_mix, w_in, hgrn_lb_logits, hgrn_norm, w_branch_attn,
           w_branch_hgrn, w_out, pre_norm_ffn, post_norm_ffn, w_router, router_bias, w_gate_e, w_up_e, w_down_e,
           w_gate_s, w_up_s, w_down_s):
    lb_table = jnp.cumsum(jax.nn.softmax(hgrn_lb_logits.astype(F32), axis=0), axis=0)
    depth = w_ada.shape[0]
    for l in range(depth):
        x = _layer(x, c, lb_table[l], w_ada[l], b_ada[l], pre_norm_mix[l], post_norm_mix[l], w_in[l],
                   hgrn_norm[l], w_branch_attn[l], w_branch_hgrn[l], w_out[l], pre_norm_ffn[l],
                   post_norm_ffn[l], w_router[l], router_bias[l], w_gate_e[l], w_up_e[l], w_down_e[l],
                   w_gate_s[l], w_up_s[l], w_down_s[l])
    return x
```
